```python
import math
import jax
import jax.numpy as jnp
from jax import lax
import numpy as np

D_MODEL = 1024
BATCH = 4
SEQ = 8192
DEPTH = 2

GRID_W = 64
CTX_LEN = 256
HEAD_DIM = 64
N_MIX_HEADS = D_MODEL // HEAD_DIM
GMLP_HEADS = N_MIX_HEADS // 4
NA_HEADS = (N_MIX_HEADS - GMLP_HEADS) // 2
DN_HEADS = N_MIX_HEADS - GMLP_HEADS - NA_HEADS
GMLP_WIDTH = GMLP_HEADS * HEAD_DIM
DN_WIDTH = DN_HEADS * HEAD_DIM
NA_WIDTH = NA_HEADS * HEAD_DIM
MIX_WIDTH = GMLP_WIDTH + DN_WIDTH + NA_WIDTH
GMLP_CHUNK = 128
DN_CONV = 3
DELTA_CHUNK = 64
WIN_ROWS = 8
WIN_COLS = 16
ROPE_BASE = 10000.0
D_FF = ((8 * D_MODEL // 3) + 127) // 128 * 128
FFN_CONV = 3
GMLP_COLS = 2 * GMLP_WIDTH
DN_COLS = 4 * DN_WIDTH + 4 * DN_HEADS
NA_COLS = 3 * NA_WIDTH
IN_COLS = GMLP_COLS + DN_COLS + NA_COLS
DEEPNORM_ALPHA = (2 * DEPTH) ** 0.25
DEEPNORM_BETA = (8 * DEPTH) ** -0.25
ADA_INIT = 0.5
EPS = 1e-6

kernel_name = 'hybrid_gmlp_deltanet_natten_dit'


def layer_norm(x):
    xf = x.astype(jnp.float32)
    mu = jnp.mean(xf, -1, keepdims=True)
    var = jnp.mean(jnp.square(xf - mu), -1, keepdims=True)
    return ((xf - mu) * lax.rsqrt(var + EPS)).astype(x.dtype)


def post_norm(x, y, g, b):
    s = DEEPNORM_ALPHA * x.astype(jnp.float32) + y.astype(jnp.float32)
    return (layer_norm(s) * g + b).astype(x.dtype)


def modulate(h, shift, scale):
    return h * (1 + scale) + shift


def dwconv(x, w):
    k = w.shape[0]
    return lax.conv_general_dilated(x, w[:, None, :].astype(x.dtype), window_strides=(1,),
                                    padding=[(k // 2, k // 2)],
                                    dimension_numbers=('NWC', 'WIO', 'NWC'),
                                    feature_group_count=x.shape[-1])


def l2norm(t):
    return t * lax.rsqrt(jnp.sum(jnp.square(t), -1, keepdims=True) + EPS)


def axial_rope(t):
    n_tok = t.shape[2]
    half = HEAD_DIM // 2
    nf = half // 2
    pos = jnp.arange(n_tok)
    inv = ROPE_BASE ** (-jnp.arange(nf, dtype=jnp.float32) / nf)

    def rot(xh, p):
        ang = p[:, None].astype(jnp.float32) * inv
        cos, sin = jnp.cos(ang), jnp.sin(ang)
        x1, x2 = xh[..., :nf], xh[..., nf:]
        return jnp.concatenate([x1 * cos - x2 * sin, x2 * cos + x1 * sin], -1)

    return jnp.concatenate([rot(t[..., :half], pos // GRID_W), rot(t[..., half:], pos % GRID_W)], -1)


def gmlp_mix(z, ln_g, ln_b, ws, bs):
    z = jax.nn.gelu(z)
    u, v = jnp.split(z, 2, axis=-1)
    bn, n_tok, _ = v.shape
    v = v.reshape(bn, n_tok, GMLP_HEADS, HEAD_DIM)
    v = layer_norm(v) * ln_g.reshape(GMLP_HEADS, HEAD_DIM) + ln_b.reshape(GMLP_HEADS, HEAD_DIM)
    v = v.reshape(bn, n_tok // GMLP_CHUNK, GMLP_CHUNK, GMLP_HEADS, HEAD_DIM)
    mixed = jnp.einsum('gpq,bnqgc->bnpgc', ws, v) + bs.T[None, None, :, :, None]
    return u * mixed.reshape(bn, n_tok, GMLP_WIDTH)


def chunk_gated_delta(q, k, v, g, beta, s0, with_output):
    bn, nh, n_tok, dk = k.shape
    cs = DELTA_CHUNK
    n = n_tok // cs
    ch = lambda t: jnp.moveaxis(t.reshape(bn, nh, n, cs, *t.shape[3:]), 2, 0)
    k, v, g, beta = ch(k), ch(v), ch(g), ch(beta)
    gc = jnp.cumsum(g, axis=-1)
    causal = jnp.tril(jnp.ones((cs, cs), bool))
    strict = jnp.tril(jnp.ones((cs, cs), bool), -1)
    decay = jnp.exp(jnp.where(causal, gc[..., :, None] - gc[..., None, :], -jnp.inf))
    kb = k * beta[..., None]
    a = jnp.where(strict, jnp.einsum('nbhik,nbhjk->nbhij', kb, k) * decay, 0.0)
    eye = jnp.eye(cs, dtype=k.dtype)
    t_inv = lax.linalg.triangular_solve(eye + a, jnp.broadcast_to(eye, a.shape), left_side=True,
                                        lower=True, unit_diagonal=True)
    u = jnp.einsum('nbhij,nbhjv->nbhiv', t_inv, v * beta[..., None])
    w = jnp.einsum('nbhij,nbhjk->nbhik', t_inv, kb * jnp.exp(gc)[..., None])
    g_last = gc[..., -1]
    k_dec = k * jnp.exp(g_last[..., None] - gc)[..., None]
    if with_output:
        qs = ch(q) * dk ** -0.5
        q_dec = qs * jnp.exp(gc)[..., None]
        attn = jnp.where(causal, jnp.einsum('nbhik,nbhjk->nbhij', qs, k) * decay, 0.0)
        xs = (u, w, k_dec, g_last, q_dec, attn)
    else:
        xs = (u, w, k_dec, g_last)

    def step(s, xc):
        u_n, w_n, kd_n, gl_n = xc[:4]
        v_new = u_n - jnp.einsum('bhik,bhkv->bhiv', w_n, s)
        s_next = s * jnp.exp(gl_n)[..., None, None] + jnp.einsum('bhik,bhiv->bhkv', kd_n, v_new)
        if with_output:
            qd_n, at_n = xc[4], xc[5]
            o = jnp.einsum('bhik,bhkv->bhiv', qd_n, s) + jnp.einsum('bhij,bhjv->bhiv', at_n, v_new)
            return s_next, o
        return s_next, None

    s_fin, o = lax.scan(step, s0, xs)
    if with_output:
        o = jnp.moveaxis(o, 0, 2).reshape(bn, nh, n_tok, v.shape[-1])
    return o, s_fin


def dn_prepare(zz, conv_w, a_log, dt_bias, rotary):
    bn, n_tok, _ = zz.shape
    qkv = jax.nn.silu(dwconv(zz[..., :3 * DN_WIDTH], conv_w)).astype(jnp.float32)
    heads = lambda t: t.reshape(bn, n_tok, DN_HEADS, HEAD_DIM).transpose(0, 2, 1, 3)
    q, k, v = (heads(t) for t in jnp.split(qkv, 3, axis=-1))
    q, k = l2norm(q), l2norm(k)
    if rotary:
        q, k = axial_rope(q), axial_rope(k)
    gate = zz[..., 3 * DN_WIDTH:4 * DN_WIDTH]
    ab = zz[..., 4 * DN_WIDTH:].astype(jnp.float32).reshape(bn, n_tok, 2, 2, DN_HEADS)
    log_decay = -jnp.exp(a_log.astype(jnp.float32)) * jax.nn.softplus(ab[:, :, 0] + dt_bias.astype(jnp.float32))
    beta = jax.nn.sigmoid(ab[:, :, 1])
    return q, k, v, gate, log_decay.transpose(2, 0, 3, 1), beta.transpose(2, 0, 3, 1)


def gated_rmsnorm(o, gate, norm_g):
    bn, _, n_tok, _ = o.shape
    o = o * lax.rsqrt(jnp.mean(jnp.square(o), -1, keepdims=True) + EPS) * norm_g.astype(jnp.float32)
    o = o.transpose(0, 2, 1, 3).reshape(bn, n_tok, DN_WIDTH)
    return (o * jax.nn.silu(gate.astype(jnp.float32))).astype(gate.dtype)


def gated_deltanet(z, zc, conv_w, a_log, dt_bias, norm_g, ctx_out):
    q, k, v, gate, gd, bt = dn_prepare(z, conv_w, a_log, dt_bias, True)
    qc, kc, vc, gate_c, gdc, btc = dn_prepare(zc, conv_w, a_log, dt_bias, False)
    s0 = jnp.zeros((z.shape[0], DN_HEADS, HEAD_DIM, HEAD_DIM), jnp.float32)
    o = 0.0
    oc = 0.0
    for d in range(2):
        fl = (lambda t: jnp.flip(t, axis=2)) if d else (lambda t: t)
        o_c, s_c = chunk_gated_delta(fl(qc), fl(kc), fl(vc), fl(gdc[d]), fl(btc[d]), s0, ctx_out)
        o_l, _ = chunk_gated_delta(fl(q), fl(k), fl(v), fl(gd[d]), fl(bt[d]), s_c, True)
        o = o + fl(o_l)
        if ctx_out:
            oc = oc + fl(o_c)
    y = gated_rmsnorm(o, gate, norm_g)
    yc = gated_rmsnorm(oc, gate_c, norm_g) if ctx_out else None
    return y, yc


def neighbourhood_attention(z, zc, rpb, ctx_out):
    bn, n_tok, _ = z.shape
    rows = n_tok // GRID_W
    wr = min(WIN_ROWS, rows)
    scale = HEAD_DIM ** -0.5
    grid = lambda t: t.reshape(bn, rows, GRID_W, NA_HEADS, HEAD_DIM).transpose(0, 3, 1, 2, 4)
    qg, kg, vg = (grid(t) for t in jnp.split(z, 3, axis=-1))
    seq = lambda t: t.reshape(bn, -1, NA_HEADS, HEAD_DIM).transpose(0, 2, 1, 3)
    qc, kc, vc = (seq(t) for t in jnp.split(zc, 3, axis=-1))
    col = np.arange(GRID_W)
    col_idx = np.clip(col - WIN_COLS // 2, 0, GRID_W - WIN_COLS)[:, None] + np.arange(WIN_COLS)
    dc_idx = col_idx - col[:, None] + (WIN_COLS - 1)
    n_win = wr * WIN_COLS

    def row_block(r):
        rs = jnp.clip(r - wr // 2, 0, rows - wr)
        kw = lax.dynamic_slice_in_dim(kg, rs, wr, axis=2)[:, :, :, col_idx]
        vw = lax.dynamic_slice_in_dim(vg, rs, wr, axis=2)[:, :, :, col_idx]
        qr = lax.dynamic_index_in_dim(qg, r, axis=2, keepdims=False)
        dr_idx = rs + jnp.arange(wr) - r + (WIN_ROWS - 1)
        bias = rpb[:, dr_idx[None, :, None], dc_idx[:, None, :]]
        s_win = jnp.einsum('bhqd,bhrqcd->bhqrc', qr, kw).astype(jnp.float32) * scale + bias
        s_ctx = jnp.einsum('bhqd,bhkd->bhqk', qr, kc).astype(jnp.float32) * scale
        s = jnp.concatenate([s_win.reshape(bn, NA_HEADS, GRID_W, n_win), s_ctx], -1)
        p = jax.nn.softmax(s, axis=-1).astype(vg.dtype)
        p_win = p[..., :n_win].reshape(bn, NA_HEADS, GRID_W, wr, WIN_COLS)
        return (jnp.einsum('bhqrc,bhrqcd->bhqd', p_win, vw)
                + jnp.einsum('bhqk,bhkd->bhqd', p[..., n_win:], vc))

    o = lax.map(row_block, jnp.arange(rows))
    y = o.transpose(1, 0, 3, 2, 4).reshape(bn, n_tok, NA_WIDTH)
    if not ctx_out:
        return y, None
    pc = jax.nn.softmax(jnp.einsum('bhqd,bhkd->bhqk', qc, kc).astype(jnp.float32) * scale, axis=-1).astype(vc.dtype)
    yc = jnp.einsum('bhqk,bhkd->bhqd', pc, vc).transpose(0, 2, 1, 3).reshape(bn, -1, NA_WIDTH)
    return y, yc


def hybrid_mixer(h, hc, w_in, gmlp_ln_g, gmlp_ln_b, gmlp_ws, gmlp_bs, dn_conv, dn_a_log,
                 dn_dt_bias, dn_norm_g, na_rpb, w_out, ctx_out):
    z = h @ w_in
    zc = hc @ w_in
    o1, o2 = GMLP_COLS, GMLP_COLS + DN_COLS
    ya = gmlp_mix(z[..., :o1], gmlp_ln_g, gmlp_ln_b, gmlp_ws, gmlp_bs)
    yb, ybc = gated_deltanet(z[..., o1:o2], zc[..., o1:o2], dn_conv, dn_a_log, dn_dt_bias, dn_norm_g, ctx_out)
    yn, ync = neighbourhood_attention(z[..., o2:], zc[..., o2:], na_rpb, ctx_out)
    y = jnp.concatenate([ya, yb.astype(ya.dtype), yn.astype(ya.dtype)], -1) @ w_out
    if not ctx_out:
        return y, None
    yac = gmlp_mix(zc[..., :o1], gmlp_ln_g, gmlp_ln_b, gmlp_ws, gmlp_bs)
    yc = jnp.concatenate([yac, ybc.astype(yac.dtype), ync.astype(yac.dtype)], -1) @ w_out
    return y, yc


def conv_glu(h, w_up, conv_w, conv_b, w_down):
    a = h @ w_up[:, :D_FF]
    v = h @ w_up[:, D_FF:]
    return (jax.nn.silu(dwconv(a, conv_w) + conv_b) * v) @ w_down


def setup_inputs(seed: int = 0) -> dict:
    key = jax.random.key(seed)
    ks = jax.random.split(key, 26)
    nrm = lambda k, shape, s: jax.random.normal(k, shape, jnp.float32) * s
    L, D = DEPTH, D_MODEL
    dt = jnp.exp(jax.random.uniform(ks[13], (L, 2, DN_HEADS), jnp.float32,
                                    minval=math.log(1e-3), maxval=math.log(1e-1)))
    return {
        'x': nrm(ks[0], (BATCH, SEQ, D), 1.0),
        'c': nrm(ks[1], (BATCH, D), 1.0),
        'ctx': nrm(ks[2], (BATCH, CTX_LEN, D), 1.0),
        'c_ctx': nrm(ks[3], (D,), 1.0),
        'w_ada': nrm(ks[4], (L, D, 6 * D), ADA_INIT * D ** -0.5),
        'b_ada': nrm(ks[5], (L, 6 * D), 0.01),
        'w_in': nrm(ks[6], (L, D, IN_COLS), D ** -0.5),
        'gmlp_ln_g': 1.0 + nrm(ks[7], (L, GMLP_WIDTH), 0.01),
        'gmlp_ln_b': nrm(ks[8], (L, GMLP_WIDTH), 0.01),
        'gmlp_ws': nrm(ks[9], (L, GMLP_HEADS, GMLP_CHUNK, GMLP_CHUNK), GMLP_CHUNK ** -0.5),
        'gmlp_bs': 1.0 + nrm(ks[10], (L, GMLP_HEADS, GMLP_CHUNK), 0.01),
        'dn_conv': nrm(ks[11], (L, DN_CONV, 3 * DN_WIDTH), DN_CONV ** -0.5),
        'dn_a_log': jnp.log(jax.random.uniform(ks[12], (L, 2, DN_HEADS), jnp.float32, minval=1.0, maxval=16.0)),
        'dn_dt_bias': dt + jnp.log(-jnp.expm1(-dt)),
        'dn_norm_g': 1.0 + nrm(ks[14], (L, HEAD_DIM), 0.01),
        'na_rpb': nrm(ks[15], (L, NA_HEADS, 2 * WIN_ROWS - 1, 2 * WIN_COLS - 1), 0.02),
        'w_out': nrm(ks[16], (L, MIX_WIDTH, D), DEEPNORM_BETA * MIX_WIDTH ** -0.5),
        'ln1_g': 1.0 + nrm(ks[17], (L, D), 0.01),
        'ln1_b': nrm(ks[18], (L, D), 0.01),
        'ffn_up': nrm(ks[19], (L, D, 2 * D_FF), D ** -0.5),
        'ffn_conv': nrm(ks[20], (L, FFN_CONV, D_FF), FFN_CONV ** -0.5),
        'ffn_conv_b': nrm(ks[21], (L, D_FF), 0.01),
        'ffn_down': nrm(ks[22], (L, D_FF, D), DEEPNORM_BETA * D_FF ** -0.5),
        'ln2_g': 1.0 + nrm(ks[23], (L, D), 0.01),
        'ln2_b': nrm(ks[24], (L, D), 0.01),
    }


def reference(x, c, ctx, c_ctx, w_ada, b_ada, w_in, gmlp_ln_g, gmlp_ln_b, gmlp_ws, gmlp_bs,
              dn_conv, dn_a_log, dn_dt_bias, dn_norm_g, na_rpb, w_out, ln1_g, ln1_b,
              ffn_up, ffn_conv, ffn_conv_b, ffn_down, ln2_g, ln2_b):
    for l in range(DEPTH):
        ctx_out = l < DEPTH - 1
        mod = jax.nn.silu(c) @ w_ada[l] + b_ada[l]
        mod_c = jax.nn.silu(c_ctx) @ w_ada[l] + b_ada[l]
        sh1, sc1, g1, sh2, sc2, g2 = jnp.split(mod[:, None, :], 6, axis=-1)
        ch1, cs1, cg1, ch2, cs2, cg2 = jnp.split(mod_c, 6, axis=-1)
        h = modulate(layer_norm(x), sh1, sc1)
        hc = modulate(layer_norm(ctx), ch1, cs1)
        y, yc = hybrid_mixer(h, hc, w_in[l], gmlp_ln_g[l], gmlp_ln_b[l], gmlp_ws[l], gmlp_bs[l],
                             dn_conv[l], dn_a_log[l], dn_dt_bias[l], dn_norm_g[l], na_rpb[l], w_out[l], ctx_out)
        x = post_norm(x, g1 * y, ln1_g[l], ln1_b[l])
        h = modulate(layer_norm(x), sh2, sc2)
        x = post_norm(x, g2 * conv_glu(h, ffn_up[l], ffn_conv[l], ffn_conv_b[l], ffn_down[l]), ln2_g[l], ln2_b[l])
        if ctx_out:
            ctx = post_norm(ctx, cg1 * yc, ln1_g[l], ln1_b[l])
            hc = modulate(layer_norm(ctx), ch2, cs2)
            ctx = post_norm(ctx, cg2 * conv_glu(hc, ffn_up[l], ffn_conv[l], ffn_conv_b[l], ffn_down[l]), ln2_g[l], ln2_b[l])
    return x
```

```python
import functools
import math

import numpy as np
import jax
import jax.numpy as jnp
from jax import lax
from jax.experimental import pallas as pl
from jax.experimental.pallas import tpu as pltpu

F32 = jnp.float32
BF16 = jnp.bfloat16

HEAD_DIM = 64
GRID_W = 64
GMLP_HEADS = 4
GMLP_WIDTH = GMLP_HEADS * HEAD_DIM
GMLP_CHUNK = 128
DN_HEADS = 6
DN_WIDTH = DN_HEADS * HEAD_DIM
NA_HEADS = 6
NA_WIDTH = NA_HEADS * HEAD_DIM
DELTA_CHUNK = 64
WIN_ROWS = 8
WIN_COLS = 16
ROPE_BASE = 10000.0
EPS = 1e-6
NEG_BIG = -1e30
LANES = 128
VMEM_LIMIT = 56 * 1024 * 1024

GMLP_COLS = 2 * GMLP_WIDTH
DN_MAIN_COLS = 4 * DN_WIDTH
DN_AB_COLS = 4 * DN_HEADS
NA_COLS = 3 * NA_WIDTH


def _cparams(sem):
    return pltpu.CompilerParams(dimension_semantics=sem, vmem_limit_bytes=VMEM_LIMIT)


def _split2(x):
    hi = x.astype(BF16)
    lo = (x - hi.astype(F32)).astype(BF16)
    return hi, lo


def _split3(x):
    a = x.astype(BF16)
    r = x - a.astype(F32)
    b = r.astype(BF16)
    c = (r - b.astype(F32)).astype(BF16)
    return a, b, c


def _dot(a, b):
    return jnp.dot(a, b, preferred_element_type=F32)


def _dot_nt(a, b):
    return lax.dot_general(a, b, (((1,), (1,)), ((), ())), preferred_element_type=F32)


def _dot_tn(a, b):
    return lax.dot_general(a, b, (((0,), (0,)), ((), ())), preferred_element_type=F32)


def _dot_x3(x, m):
    a, b, c = _split3(x)
    return _dot(a, m) + _dot(b, m) + _dot(c, m)


def _dot_x2(x, m):
    a, b = _split2(x)
    return _dot(a, m) + _dot(b, m)


def _silu(x):
    return x * jax.nn.sigmoid(x)


def _layer_norm_rows(x):
    mu = jnp.mean(x, axis=-1, keepdims=True)
    xc = x - mu
    var = jnp.mean(xc * xc, axis=-1, keepdims=True)
    return xc * lax.rsqrt(var + EPS)


def _block_diag_np(n_blocks, rows, cols, value):
    m = np.zeros((n_blocks * rows, n_blocks * cols), np.float32)
    for g in range(n_blocks):
        m[g * rows:(g + 1) * rows, g * cols:(g + 1) * cols] = value
    return m


def _mod_kernel(c_ref, w_ref, b_ref, o_ref):
    a = _silu(c_ref[...])
    w = w_ref[0]
    a1, a2 = _split2(a)
    w1, w2 = _split2(w)
    o_ref[0] = _dot(a1, w1) + _dot(a1, w2) + _dot(a2, w1) + b_ref[0]


def _mod_all(c, c_ctx, w_ada, b_ada):
    n_layers, d, n6 = w_ada.shape
    bsz = c.shape[0]
    rows = jnp.concatenate([c, c_ctx[None, :], jnp.zeros((8 - bsz - 1, d), F32)], axis=0)
    tn = 1536
    return pl.pallas_call(
        _mod_kernel,
        grid=(n_layers, n6 // tn),
        in_specs=[
            pl.BlockSpec((8, d), lambda l, j: (0, 0)),
            pl.BlockSpec((1, d, tn), lambda l, j: (l, 0, j)),
            pl.BlockSpec((1, 1, tn), lambda l, j: (l, 0, j)),
        ],
        out_specs=pl.BlockSpec((1, 8, tn), lambda l, j: (l, 0, j)),
        out_shape=jax.ShapeDtypeStruct((n_layers, 8, n6), F32),
        compiler_params=_cparams(("parallel", "parallel")),
        name="adaln_mod",
    )(rows, w_ada, b_ada.reshape(n_layers, 1, n6))


def _mod_spec(chunk, d, is_ctx, ctx_row):
    if is_ctx:
        return pl.BlockSpec((1, 1, d), lambda b, i: (ctx_row, 0, chunk))
    return pl.BlockSpec((1, 1, d), lambda b, i: (b, 0, chunk))


def _inproj_kernel(x_ref, sh_ref, sc_ref, w_ref, zg_ref, zdn_ref, zna_ref, zab_ref):
    h = _layer_norm_rows(x_ref[0]) * (1.0 + sc_ref[0]) + sh_ref[0]
    h = h.astype(BF16)
    o1 = GMLP_COLS
    o2 = o1 + DN_MAIN_COLS
    o3 = o2 + NA_COLS
    zg_ref[0] = _dot(h, w_ref[:, 0:o1]).astype(BF16)
    zdn_ref[0] = _dot(h, w_ref[:, o1:o2]).astype(BF16)
    zna_ref[0] = _dot(h, w_ref[:, o2:o3]).astype(BF16)
    zab_ref[0] = _dot(h, w_ref[:, o3:o3 + LANES])


def _in_proj(x, mod3, w_in_p, is_ctx, ctx_row, tm):
    bsz, seq, d = x.shape
    ncols = w_in_p.shape[1]
    tok = lambda w: pl.BlockSpec((1, tm, w), lambda b, i: (b, i, 0))
    return pl.pallas_call(
        _inproj_kernel,
        grid=(bsz, seq // tm),
        in_specs=[
            tok(d),
            _mod_spec(0, d, is_ctx, ctx_row),
            _mod_spec(1, d, is_ctx, ctx_row),
            pl.BlockSpec((d, ncols), lambda b, i: (0, 0)),
        ],
        out_specs=[tok(GMLP_COLS), tok(DN_MAIN_COLS), tok(NA_COLS), tok(LANES)],
        out_shape=[
            jax.ShapeDtypeStruct((bsz, seq, GMLP_COLS), BF16),
            jax.ShapeDtypeStruct((bsz, seq, DN_MAIN_COLS), BF16),
            jax.ShapeDtypeStruct((bsz, seq, NA_COLS), BF16),
            jax.ShapeDtypeStruct((bsz, seq, LANES), F32),
        ],
        compiler_params=_cparams(("parallel", "parallel")),
        name="in_proj",
    )(x, mod3, mod3, w_in_p)


def _gmlp_kernel(z_ref, g_ref, lng_ref, lnb_ref, wsp_ref, bsx_ref, o_ref, *, tm):
    z = jax.nn.gelu(z_ref[0].astype(F32))
    u = z[:, :GMLP_WIDTH]
    v = z[:, GMLP_WIDTH:]
    gmat = g_ref[...]
    mu = _dot_x2(v, gmat)
    vc = v - mu
    var = _dot_x2(vc * vc, gmat)
    vn = vc * lax.rsqrt(var + EPS) * lng_ref[...] + lnb_ref[...]
    vn = vn.astype(BF16)
    nrow = GMLP_HEADS * GMLP_CHUNK
    ri = lax.broadcasted_iota(jnp.int32, (nrow, GMLP_WIDTH), 0) // GMLP_CHUNK
    ci = lax.broadcasted_iota(jnp.int32, (nrow, GMLP_WIDTH), 1) // HEAD_DIM
    diag = ri == ci
    wsp = wsp_ref[...]
    bsx = bsx_ref[...]
    for n in range(tm // GMLP_CHUNK):
        rows = slice(n * GMLP_CHUNK, (n + 1) * GMLP_CHUNK)
        vch = vn[rows]
        bd = jnp.where(diag, jnp.concatenate([vch] * GMLP_HEADS, axis=0), jnp.zeros_like(vch[:1, :1]))
        mixed = _dot(wsp, bd) + bsx
        o_ref[0, rows, :] = (u[rows] * mixed).astype(BF16)


def _gmlp(zg, gm256, lng, lnb, wsp, bsx, tm):
    bsz, seq, _ = zg.shape
    full = lambda a: pl.BlockSpec(a.shape, lambda b, i: (0,) * a.ndim)
    return pl.pallas_call(
        functools.partial(_gmlp_kernel, tm=tm),
        grid=(bsz, seq // tm),
        in_specs=[pl.BlockSpec((1, tm, GMLP_COLS), lambda b, i: (b, i, 0)),
                  full(gm256), full(lng), full(lnb), full(wsp), full(bsx)],
        out_specs=pl.BlockSpec((1, tm, GMLP_WIDTH), lambda b, i: (b, i, 0)),
        out_shape=jax.ShapeDtypeStruct((bsz, seq, GMLP_WIDTH), BF16),
        compiler_params=_cparams(("parallel", "parallel")),
        name="gmlp_mix",
    )(zg, gm256, lng, lnb, wsp, bsx)


def _dn_prep_kernel(*refs, tm, n_tiles, rotary, halo):
    if rotary:
        (z_ref, zp_ref, zn_ref, ab_ref, cw_ref, aexp_ref, dtb_ref, g_ref, cos_ref, sin_ref,
         qkv_ref, gb_ref) = refs
    else:
        (z_ref, zp_ref, zn_ref, ab_ref, cw_ref, aexp_ref, dtb_ref, g_ref, qkv_ref, gb_ref) = refs
    i = pl.program_id(1)
    has_prev = (i > 0).astype(F32)
    has_next = (i < n_tiles - 1).astype(F32)
    x_ext = jnp.concatenate([zp_ref[0].astype(F32) * has_prev, z_ref[0].astype(F32),
                             zn_ref[0].astype(F32) * has_next], axis=0)
    n_ext = tm + 2 * halo
    x_m1 = pltpu.roll(x_ext, 1, 0)[halo:halo + tm]
    x_p1 = pltpu.roll(x_ext, n_ext - 1, 0)[halo:halo + tm]
    x_0 = x_ext[halo:halo + tm]
    cw = cw_ref[...]
    y = _silu(x_m1 * cw[0:1] + x_0 * cw[1:2] + x_p1 * cw[2:3])
    q = y[:, 0:DN_WIDTH]
    k = y[:, DN_WIDTH:2 * DN_WIDTH]
    v = y[:, 2 * DN_WIDTH:3 * DN_WIDTH]
    gmat = g_ref[...]
    q = q * lax.rsqrt(_dot_x2(q * q, gmat) + EPS)
    k = k * lax.rsqrt(_dot_x2(k * k, gmat) + EPS)
    if rotary:
        cos2 = cos_ref[...]
        sin2 = sin_ref[...]
        lane = lax.broadcasted_iota(jnp.int32, (tm, LANES), 1)
        first = (lane % 32) < 16

        def rope(t):
            parts = []
            for p in range(DN_WIDTH // LANES):
                tp = t[:, p * LANES:(p + 1) * LANES]
                partner = jnp.where(first, pltpu.roll(tp, LANES - 16, 1), pltpu.roll(tp, 16, 1))
                parts.append(tp * cos2 + partner * sin2)
            return jnp.concatenate(parts, axis=1)

        q = rope(q)
        k = rope(k)
    q = q * (HEAD_DIM ** -0.5)
    qkv_ref[0, :, 0:DN_WIDTH] = q.astype(BF16)
    qkv_ref[0, :, DN_WIDTH:2 * DN_WIDTH] = k.astype(BF16)
    qkv_ref[0, :, 2 * DN_WIDTH:3 * DN_WIDTH] = v.astype(BF16)
    ab = ab_ref[0]
    lane_ab = lax.broadcasted_iota(jnp.int32, ab.shape, 1)
    xs = ab + dtb_ref[...]
    softplus = jnp.maximum(xs, 0.0) + jnp.log1p(jnp.exp(-jnp.abs(xs)))
    log_decay = -aexp_ref[...] * softplus
    beta = jax.nn.sigmoid(ab)
    gb_ref[0] = jnp.where(lane_ab < 2 * DN_HEADS, log_decay, beta)


def _dn_prep(zdn, zab, conv_w, aexp_row, dtb_row, gm384, rope_tabs, tm):
    bsz, seq, _ = zdn.shape
    halo = 16
    n_tiles = seq // tm
    hb = tm // halo
    qkv_w = 3 * DN_WIDTH
    rotary = rope_tabs is not None
    full = lambda a: pl.BlockSpec(a.shape, lambda b, i: (0,) * a.ndim)
    in_specs = [
        pl.BlockSpec((1, tm, qkv_w), lambda b, i: (b, i, 0)),
        pl.BlockSpec((1, halo, qkv_w), lambda b, i: (b, jnp.maximum(i * hb - 1, 0), 0)),
        pl.BlockSpec((1, halo, qkv_w), lambda b, i: (b, jnp.minimum((i + 1) * hb, seq // halo - 1), 0)),
        pl.BlockSpec((1, tm, LANES), lambda b, i: (b, i, 0)),
        full(conv_w), full(aexp_row), full(dtb_row), full(gm384),
    ]
    args = [zdn, zdn, zdn, zab, conv_w, aexp_row, dtb_row, gm384]
    if rotary:
        in_specs += [pl.BlockSpec((tm, LANES), lambda b, i: (i, 0))] * 2
        args += list(rope_tabs)
    return pl.pallas_call(
        functools.partial(_dn_prep_kernel, tm=tm, n_tiles=n_tiles, rotary=rotary, halo=halo),
        grid=(bsz, n_tiles),
        in_specs=in_specs,
        out_specs=[pl.BlockSpec((1, tm, qkv_w), lambda b, i: (b, i, 0)),
                   pl.BlockSpec((1, tm, LANES), lambda b, i: (b, i, 0))],
        out_shape=[jax.ShapeDtypeStruct((bsz, seq, qkv_w), BF16),
                   jax.ShapeDtypeStruct((bsz, seq, LANES), F32)],
        compiler_params=_cparams(("parallel", "parallel")),
        name="dn_prep",
    )(*args)


def _dn_scan_kernel(qkv_ref, gate_ref, gb_ref, eg_ref, eb_ref, ng_ref, gm_ref, s0a_ref, s0b_ref,
                    y_ref, sa_out_ref, sb_out_ref, sa_scr, sb_scr, of_scr, *, n_chunks):
    cs = DELTA_CHUNK
    wa = 4 * HEAD_DIM
    w = DN_WIDTH
    d = pl.program_id(1)
    i = pl.program_id(2)
    sgn = 1 - 2 * d

    @pl.when(i == 0)
    def _():
        sa_scr[...] = s0a_ref[0, 0]
        sb_scr[...] = s0b_ref[0, 0]

    qkv = qkv_ref[0].astype(F32)
    q = qkv[:, 0:w]
    k = qkv[:, w:2 * w]
    v = qkv[:, 2 * w:3 * w]
    gb = gb_ref[0]

    ii = lax.broadcasted_iota(jnp.int32, (cs, cs), 0)
    mm = lax.broadcasted_iota(jnp.int32, (cs, cs), 1)
    lmat = jnp.where((ii - mm) * sgn >= 0, 1.0, 0.0).astype(BF16)
    g1, g2, g3 = _split3(gb)
    gc_n = _dot(lmat, g1) + _dot(lmat, g2) + _dot(lmat, g3)
    gcx = _dot_x3(gc_n, eg_ref[0])
    bx = _dot_x2(gb, eb_ref[0])

    i3 = lax.broadcasted_iota(jnp.int32, (cs, w), 0)
    j3 = lax.broadcasted_iota(jnp.int32, (cs, w), 1) % HEAD_DIM
    rel = (i3 - j3) * sgn
    causal = rel >= 0
    strict = rel > 0
    gc_t = jnp.sum(jnp.where(i3 == j3, gcx, 0.0), axis=0, keepdims=True)
    dm = jnp.where(causal, jnp.exp(jnp.where(causal, gcx - gc_t, 0.0)), 0.0)
    last_row = jnp.where(d == 0, cs - 1, 0)
    gl = jnp.sum(jnp.where(i3 == last_row, gcx, 0.0), axis=0, keepdims=True)
    egx = jnp.exp(gcx)
    kdx = jnp.exp(gl - gcx)

    ra = lax.broadcasted_iota(jnp.int32, (wa, wa), 0) // HEAD_DIM
    ca = lax.broadcasted_iota(jnp.int32, (wa, wa), 1) // HEAD_DIM
    diag_a = ra == ca
    diag_b = diag_a[:w - wa, :w - wa]
    zero = jnp.zeros((1, 1), BF16)

    def bd(x):
        xa = x[:, :wa].astype(BF16)
        xb = x[:, wa:].astype(BF16)
        bda = jnp.where(diag_a, jnp.concatenate([xa] * 4, axis=0), zero)
        bdb = jnp.where(diag_b, jnp.concatenate([xb] * 2, axis=0), zero)
        return bda, bdb

    def bdmm(lhs, x):
        bda, bdb = bd(x)
        lhs = lhs.astype(BF16)
        return jnp.concatenate([_dot(lhs[:, :wa], bda), _dot(lhs[:, wa:], bdb)], axis=1)

    kb = k * bx
    vb = v * bx
    bka, bkb = bd(k)
    lhs = jnp.concatenate([kb, q], axis=0).astype(BF16)
    p1 = jnp.concatenate([_dot_nt(lhs[:, :wa], bka), _dot_nt(lhs[:, wa:], bkb)], axis=1)
    a_mat = jnp.where(strict, p1[:cs] * dm, 0.0)
    attn = jnp.where(causal, p1[cs:] * dm, 0.0)

    t_mat = jnp.where(i3 == j3, 1.0, 0.0) - a_mat
    b_pow = bdmm(a_mat, a_mat)
    for _ in range(int(math.log2(cs)) - 2):
        pp = bdmm(jnp.concatenate([t_mat, b_pow], axis=0), b_pow)
        t_mat = t_mat + pp[:cs]
        b_pow = pp[cs:]
    t_mat = t_mat + bdmm(t_mat, b_pow)

    u = bdmm(t_mat, vb)
    wk = bdmm(t_mat, kb * egx)
    sa = sa_scr[...]
    sb = sb_scr[...]
    lhs2 = jnp.concatenate([wk, q * egx], axis=0).astype(BF16)
    p2 = jnp.concatenate([_dot(lhs2[:, :wa], sa.astype(BF16)), _dot(lhs2[:, wa:], sb.astype(BF16))], axis=1)
    v_new = u - p2[:cs]
    o = p2[cs:] + bdmm(attn, v_new)

    kd = (k * kdx).astype(BF16)
    vnb = v_new.astype(BF16)
    egl = jnp.exp(gl)
    sa_scr[...] = sa * egl[:, :wa] + jnp.where(diag_a, _dot_tn(kd[:, :wa], vnb[:, :wa]), 0.0)
    sb_scr[...] = sb * egl[:, wa:] + jnp.where(diag_b, _dot_tn(kd[:, wa:], vnb[:, wa:]), 0.0)

    chunk = jnp.where(d == 0, i, n_chunks - 1 - i)
    row0 = pl.multiple_of(chunk * cs, cs)

    @pl.when(d == 0)
    def _():
        of_scr[pl.ds(row0, cs), :] = o

    @pl.when(d == 1)
    def _():
        ot = of_scr[pl.ds(row0, cs), :] + o
        ms = _dot_x2(ot * ot, gm_ref[...])
        gate = gate_ref[0].astype(F32)
        y_ref[0] = (ot * lax.rsqrt(ms + EPS) * ng_ref[...] * _silu(gate)).astype(BF16)

    @pl.when(i == n_chunks - 1)
    def _():
        sa_out_ref[0, 0] = sa_scr[...]
        sb_out_ref[0, 0] = sb_scr[...]


def _dn_scan(qkv, zdn, gb, e_g, e_b, ng_row, gm384_mean, s0a, s0b):
    bsz, seq, _ = qkv.shape
    cs = DELTA_CHUNK
    n = seq // cs
    wa = 4 * HEAD_DIM
    wb = DN_WIDTH - wa
    chunk_of = lambda dd, i: jnp.where(dd == 0, i, n - 1 - i)
    full = lambda a: pl.BlockSpec(a.shape, lambda b, dd, i: (0,) * a.ndim)
    return pl.pallas_call(
        functools.partial(_dn_scan_kernel, n_chunks=n),
        grid=(bsz, 2, n),
        in_specs=[
            pl.BlockSpec((1, cs, 3 * DN_WIDTH), lambda b, dd, i: (b, chunk_of(dd, i), 0)),
            pl.BlockSpec((1, cs, DN_WIDTH), lambda b, dd, i: (b, jnp.where(dd == 0, n - 1, n - 1 - i), 3)),
            pl.BlockSpec((1, cs, LANES), lambda b, dd, i: (b, chunk_of(dd, i), 0)),
            pl.BlockSpec((1, LANES, DN_WIDTH), lambda b, dd, i: (dd, 0, 0)),
            pl.BlockSpec((1, LANES, DN_WIDTH), lambda b, dd, i: (dd, 0, 0)),
            full(ng_row), full(gm384_mean),
            pl.BlockSpec((1, 1, wa, wa), lambda b, dd, i: (b, dd, 0, 0)),
            pl.BlockSpec((1, 1, wb, wb), lambda b, dd, i: (b, dd, 0, 0)),
        ],
        out_specs=[
            pl.BlockSpec((1, cs, DN_WIDTH), lambda b, dd, i: (b, jnp.where(dd == 0, n - 1, n - 1 - i), 0)),
            pl.BlockSpec((1, 1, wa, wa), lambda b, dd, i: (b, dd, 0, 0)),
            pl.BlockSpec((1, 1, wb, wb), lambda b, dd, i: (b, dd, 0, 0)),
        ],
        out_shape=[
            jax.ShapeDtypeStruct((bsz, seq, DN_WIDTH), BF16),
            jax.ShapeDtypeStruct((bsz, 2, wa, wa), F32),
            jax.ShapeDtypeStruct((bsz, 2, wb, wb), F32),
        ],
        scratch_shapes=[pltpu.VMEM((wa, wa), F32), pltpu.VMEM((wb, wb), F32), pltpu.VMEM((seq, DN_WIDTH), F32)],
        compiler_params=_cparams(("parallel", "arbitrary", "arbitrary")),
        name="dn_scan",
    )(qkv, zdn, gb, e_g, e_b, ng_row, gm384_mean, s0a, s0b)


def _na_kernel(q_ref, k_ref, v_ref, kc_ref, vc_ref, tbl_ref, o_ref, *, rows_per_step, n_rows, win_rows):
    i = pl.program_id(1)
    gw = GRID_W
    lane = lax.broadcasted_iota(jnp.int32, (1, LANES), 1)
    first_head = lane < HEAD_DIM
    scale = HEAD_DIM ** -0.5

    def row_body(rq, carry):
        r = i * rows_per_step + rq
        rs = jnp.clip(r - win_rows // 2, 0, n_rows - win_rows)
        var = r - rs
        k0 = pl.multiple_of(rs * gw, gw)
        q0 = pl.multiple_of(rq * gw, gw)
        for p in range(NA_WIDTH // LANES):
            ls = slice(p * LANES, (p + 1) * LANES)
            qp = q_ref[0, pl.ds(q0, gw), ls] * scale
            kw = k_ref[0, pl.ds(k0, win_rows * gw), ls]
            vw = v_ref[0, pl.ds(k0, win_rows * gw), ls]
            kc = kc_ref[0, :, ls]
            vc = vc_ref[0, :, ls]
            outs = []
            for hh in range(2):
                msk = first_head if hh == 0 else jnp.logical_not(first_head)
                qm = jnp.where(msk, qp, jnp.zeros_like(qp[:1, :1]))
                s_w = _dot_nt(qm, kw) + tbl_ref[2 * p + hh, var]
                s_c = _dot_nt(qm, kc)
                m = jnp.maximum(jnp.max(s_w, axis=-1, keepdims=True), jnp.max(s_c, axis=-1, keepdims=True))
                p_w = jnp.exp(s_w - m)
                p_c = jnp.exp(s_c - m)
                den = jnp.sum(p_w, axis=-1, keepdims=True) + jnp.sum(p_c, axis=-1, keepdims=True)
                o = _dot(p_w.astype(BF16), vw) + _dot(p_c.astype(BF16), vc)
                outs.append(o / den)
            o_ref[0, pl.ds(q0, gw), ls] = jnp.where(first_head, outs[0], outs[1]).astype(BF16)
        return carry

    lax.fori_loop(0, rows_per_step, row_body, 0)


def _na(zna, zna_ctx, tbl, rows_per_step):
    bsz, seq, _ = zna.shape
    n_ctx = zna_ctx.shape[1]
    n_rows = seq // GRID_W
    win_rows = min(WIN_ROWS, n_rows)
    tq = rows_per_step * GRID_W
    return pl.pallas_call(
        functools.partial(_na_kernel, rows_per_step=rows_per_step, n_rows=n_rows, win_rows=win_rows),
        grid=(bsz, n_rows // rows_per_step),
        in_specs=[
            pl.BlockSpec((1, tq, NA_WIDTH), lambda b, i: (b, i, 0)),
            pl.BlockSpec((1, seq, NA_WIDTH), lambda b, i: (b, 0, 1)),
            pl.BlockSpec((1, seq, NA_WIDTH), lambda b, i: (b, 0, 2)),
            pl.BlockSpec((1, n_ctx, NA_WIDTH), lambda b, i: (b, 0, 1)),
            pl.BlockSpec((1, n_ctx, NA_WIDTH), lambda b, i: (b, 0, 2)),
            pl.BlockSpec(tbl.shape, lambda b, i: (0, 0, 0, 0)),
        ],
        out_specs=pl.BlockSpec((1, tq, NA_WIDTH), lambda b, i: (b, i, 0)),
        out_shape=jax.ShapeDtypeStruct((bsz, seq, NA_WIDTH), BF16),
        compiler_params=_cparams(("parallel", "arbitrary")),
        name="nbr_attn",
    )(zna, zna, zna, zna_ctx, zna_ctx, tbl)


def _na_ctx_kernel(q_ref, k_ref, v_ref, o_ref):
    lane = lax.broadcasted_iota(jnp.int32, (1, LANES), 1)
    first_head = lane < HEAD_DIM
    scale = HEAD_DIM ** -0.5
    for p in range(NA_WIDTH // LANES):
        ls = slice(p * LANES, (p + 1) * LANES)
        qp = q_ref[0, :, ls] * scale
        kp = k_ref[0, :, ls]
        vp = v_ref[0, :, ls]
        outs = []
        for hh in range(2):
            msk = first_head if hh == 0 else jnp.logical_not(first_head)
            qm = jnp.where(msk, qp, jnp.zeros_like(qp[:1, :1]))
            s = _dot_nt(qm, kp)
            e = jnp.exp(s - jnp.max(s, axis=-1, keepdims=True))
            den = jnp.sum(e, axis=-1, keepdims=True)
            outs.append(_dot(e.astype(BF16), vp) / den)
        o_ref[0, :, ls] = jnp.where(first_head, outs[0], outs[1]).astype(BF16)


def _na_ctx(zna_ctx):
    bsz, n_ctx, _ = zna_ctx.shape
    col = lambda j: pl.BlockSpec((1, n_ctx, NA_WIDTH), lambda b: (b, 0, j))
    return pl.pallas_call(
        _na_ctx_kernel,
        grid=(bsz,),
        in_specs=[col(0), col(1), col(2)],
        out_specs=pl.BlockSpec((1, n_ctx, NA_WIDTH), lambda b: (b, 0, 0)),
        out_shape=jax.ShapeDtypeStruct((bsz, n_ctx, NA_WIDTH), BF16),
        compiler_params=_cparams(("parallel",)),
        name="ctx_attn",
    )(zna_ctx, zna_ctx, zna_ctx)


def _outproj_kernel(ya_ref, yb_ref, yn_ref, w_ref, x_ref, g_ref, lg_ref, lb_ref, o_ref, *, alpha):
    ycat = jnp.concatenate([ya_ref[0], yb_ref[0], yn_ref[0]], axis=1)
    y = _dot(ycat, w_ref[...])
    s = alpha * x_ref[0] + g_ref[0] * y
    o_ref[0] = _layer_norm_rows(s) * lg_ref[...] + lb_ref[...]


def _out_proj(ya, yb, yn, w_out_b, x, mod3, lg, lb, is_ctx, ctx_row, tm, alpha):
    bsz, seq, d = x.shape
    tok = lambda w: pl.BlockSpec((1, tm, w), lambda b, i: (b, i, 0))
    full = lambda a: pl.BlockSpec(a.shape, lambda b, i: (0,) * a.ndim)
    return pl.pallas_call(
        functools.partial(_outproj_kernel, alpha=alpha),
        grid=(bsz, seq // tm),
        in_specs=[tok(GMLP_WIDTH), tok(DN_WIDTH), tok(NA_WIDTH), full(w_out_b), tok(d),
                  _mod_spec(2, d, is_ctx, ctx_row), full(lg), full(lb)],
        out_specs=tok(d),
        out_shape=jax.ShapeDtypeStruct((bsz, seq, d), F32),
        compiler_params=_cparams(("parallel", "parallel")),
        name="out_proj",
    )(ya, yb, yn, w_out_b, x, mod3, lg, lb)


def _ffn_kernel(x_ref, xp_ref, xn_ref, sh_ref, sc_ref, g_ref, wa_ref, wv_ref, cw_ref, cb_ref, wd_ref,
                lg_ref, lb_ref, o_ref, acc_ref, *, tm, n_tiles, n_chunks, alpha):
    halo = 8
    i = pl.program_id(1)
    has_prev = (i > 0).astype(F32)
    has_next = (i < n_tiles - 1).astype(F32)
    x = x_ref[0]
    sc = 1.0 + sc_ref[0]
    sh = sh_ref[0]
    h_mid = _layer_norm_rows(x) * sc + sh
    h_prev = (_layer_norm_rows(xp_ref[0]) * sc + sh) * has_prev
    h_next = (_layer_norm_rows(xn_ref[0]) * sc + sh) * has_next
    h_ext = jnp.concatenate([h_prev, h_mid, h_next], axis=0).astype(BF16)
    h_b = h_mid.astype(BF16)
    n_ext = tm + 2 * halo
    acc_ref[...] = jnp.zeros_like(acc_ref)

    def chunk_body(j, carry):
        a = _dot(h_ext, wa_ref[j])
        vv = _dot(h_b, wv_ref[j])
        cw = cw_ref[j]
        a_m1 = pltpu.roll(a, 1, 0)[halo:halo + tm]
        a_p1 = pltpu.roll(a, n_ext - 1, 0)[halo:halo + tm]
        conv = a_m1 * cw[0:1] + a[halo:halo + tm] * cw[1:2] + a_p1 * cw[2:3] + cb_ref[j]
        gated = (_silu(conv) * vv).astype(BF16)
        acc_ref[...] += _dot(gated, wd_ref[j])
        return carry

    lax.fori_loop(0, n_chunks, chunk_body, 0)
    s = alpha * x + g_ref[0] * acc_ref[...]
    o_ref[0] = _layer_norm_rows(s) * lg_ref[...] + lb_ref[...]


def _ffn(x, mod3, wa, wv, cw, cb, wd, lg, lb, is_ctx, ctx_row, tm, alpha):
    bsz, seq, d = x.shape
    n_tiles = seq // tm
    n_chunks = wa.shape[0]
    hb = tm // 8
    tok = pl.BlockSpec((1, tm, d), lambda b, i: (b, i, 0))
    full = lambda a: pl.BlockSpec(a.shape, lambda b, i: (0,) * a.ndim)
    return pl.pallas_call(
        functools.partial(_ffn_kernel, tm=tm, n_tiles=n_tiles, n_chunks=n_chunks, alpha=alpha),
        grid=(bsz, n_tiles),
        in_specs=[
            tok,
            pl.BlockSpec((1, 8, d), lambda b, i: (b, jnp.maximum(i * hb - 1, 0), 0)),
            pl.BlockSpec((1, 8, d), lambda b, i: (b, jnp.minimum((i + 1) * hb, seq // 8 - 1), 0)),
            _mod_spec(3, d, is_ctx, ctx_row), _mod_spec(4, d, is_ctx, ctx_row), _mod_spec(5, d, is_ctx, ctx_row),
            full(wa), full(wv), full(cw), full(cb), full(wd), full(lg), full(lb),
        ],
        out_specs=tok,
        out_shape=jax.ShapeDtypeStruct((bsz, seq, d), F32),
        scratch_shapes=[pltpu.VMEM((tm, d), F32)],
        compiler_params=_cparams(("parallel", "parallel")),
        name="conv_glu",
    )(x, x, x, mod3, mod3, mod3, wa, wv, cw, cb, wd, lg, lb)


def _rope_tables(seq):
    half = HEAD_DIM // 2
    nf = half // 2
    pos = jnp.arange(seq)
    inv = ROPE_BASE ** (-jnp.arange(nf, dtype=F32) / nf)
    ang_r = (pos // GRID_W)[:, None].astype(F32) * inv
    ang_c = (pos % GRID_W)[:, None].astype(F32) * inv
    cos_h = jnp.concatenate([jnp.cos(ang_r)] * 2 + [jnp.cos(ang_c)] * 2, axis=1)
    sin_h = jnp.concatenate([-jnp.sin(ang_r), jnp.sin(ang_r), -jnp.sin(ang_c), jnp.sin(ang_c)], axis=1)
    reps = LANES // HEAD_DIM
    return jnp.tile(cos_h, (1, reps)), jnp.tile(sin_h, (1, reps))


def _bias_table(rpb, n_rows):
    wr = min(WIN_ROWS, n_rows)
    col = np.arange(GRID_W)
    cs = np.clip(col - WIN_COLS // 2, 0, GRID_W - WIN_COLS)
    kc = np.arange(GRID_W)
    valid = (kc[None, :] >= cs[:, None]) & (kc[None, :] < cs[:, None] + WIN_COLS)
    dc = np.clip(kc[None, :] - col[:, None] + (WIN_COLS - 1), 0, 2 * WIN_COLS - 2)
    var = np.arange(wr)
    j = np.arange(wr)
    dr = np.clip(j[None, :] - var[:, None] + (WIN_ROWS - 1), 0, 2 * WIN_ROWS - 2)
    t = rpb[:, dr[:, :, None, None], dc[None, None, :, :]]
    t = jnp.where(valid[None, None, None], t, NEG_BIG)
    t = t.transpose(0, 1, 3, 2, 4)
    return t.reshape(rpb.shape[0], wr, GRID_W, wr * GRID_W).astype(F32)


def _expand_mats():
    e_g = np.zeros((2, LANES, DN_WIDTH), np.float32)
    e_b = np.zeros((2, LANES, DN_WIDTH), np.float32)
    for dd in range(2):
        for h in range(DN_HEADS):
            e_g[dd, dd * DN_HEADS + h, h * HEAD_DIM:(h + 1) * HEAD_DIM] = 1.0
            e_b[dd, 2 * DN_HEADS + dd * DN_HEADS + h, h * HEAD_DIM:(h + 1) * HEAD_DIM] = 1.0
    return jnp.asarray(e_g, BF16), jnp.asarray(e_b, BF16)


def _pad_lanes(v):
    return jnp.zeros((1, LANES), F32).at[0, :v.shape[0]].set(v.astype(F32))


def _token_tile(seq):
    return 512 if seq % 512 == 0 else 256


def kernel(x, c, ctx, c_ctx, w_ada, b_ada, w_in, gmlp_ln_g, gmlp_ln_b, gmlp_ws, gmlp_bs, dn_conv, dn_a_log,
           dn_dt_bias, dn_norm_g, na_rpb, w_out, ln1_g, ln1_b, ffn_up, ffn_conv, ffn_conv_b, ffn_down,
           ln2_g, ln2_b):
    depth = w_ada.shape[0]
    bsz, seq, d = x.shape
    n_ctx = ctx.shape[1]
    d_ff = ffn_down.shape[1]
    alpha = (2 * depth) ** 0.25
    ctx_row = bsz
    ffn_cw = 256
    n_ffn_chunks = d_ff // ffn_cw
    tm = _token_tile(seq)
    tm_c = _token_tile(n_ctx)

    mod = _mod_all(c, c_ctx, w_ada, b_ada)
    rope_tabs = _rope_tables(seq)
    e_g, e_b = _expand_mats()
    gm256_mean = jnp.asarray(_block_diag_np(GMLP_HEADS, HEAD_DIM, HEAD_DIM, 1.0 / HEAD_DIM), BF16)
    gm384_sum = jnp.asarray(_block_diag_np(DN_HEADS, HEAD_DIM, HEAD_DIM, 1.0), BF16)
    gm384_mean = jnp.asarray(_block_diag_np(DN_HEADS, HEAD_DIM, HEAD_DIM, 1.0 / HEAD_DIM), BF16)
    wa4 = 4 * HEAD_DIM
    wb2 = DN_WIDTH - wa4
    zeros_a = jnp.zeros((bsz, 2, wa4, wa4), F32)
    zeros_b = jnp.zeros((bsz, 2, wb2, wb2), F32)
    o1 = GMLP_COLS
    o2 = o1 + DN_MAIN_COLS
    o3 = o2 + DN_AB_COLS

    for l in range(depth):
        ctx_out = l < depth - 1
        mod3 = mod[l].reshape(8, 1, 6 * d)
        wl = w_in[l]
        w_in_p = jnp.concatenate(
            [wl[:, :o2], wl[:, o3:], wl[:, o2:o3], jnp.zeros((d, LANES - DN_AB_COLS), F32)], axis=1).astype(BF16)
        lng = gmlp_ln_g[l].reshape(1, GMLP_WIDTH)
        lnb = gmlp_ln_b[l].reshape(1, GMLP_WIDTH)
        wsp = gmlp_ws[l].transpose(1, 0, 2).reshape(GMLP_CHUNK, GMLP_HEADS * GMLP_CHUNK).astype(BF16)
        bsx = jnp.repeat(gmlp_bs[l].T, HEAD_DIM, axis=1)
        aexp_row = _pad_lanes(jnp.exp(dn_a_log[l].astype(F32)).reshape(-1))
        dtb_row = _pad_lanes(dn_dt_bias[l].reshape(-1))
        ng_row = jnp.tile(dn_norm_g[l].astype(F32), DN_HEADS).reshape(1, DN_WIDTH)
        tbl = _bias_table(na_rpb[l], seq // GRID_W)
        w_out_b = w_out[l].astype(BF16)
        lg1 = ln1_g[l].reshape(1, d)
        lb1 = ln1_b[l].reshape(1, d)
        lg2 = ln2_g[l].reshape(1, d)
        lb2 = ln2_b[l].reshape(1, d)
        up = ffn_up[l]
        wa = up[:, :d_ff].reshape(d, n_ffn_chunks, ffn_cw).transpose(1, 0, 2).astype(BF16)
        wv = up[:, d_ff:].reshape(d, n_ffn_chunks, ffn_cw).transpose(1, 0, 2).astype(BF16)
        wd = ffn_down[l].reshape(n_ffn_chunks, ffn_cw, d).astype(BF16)
        cw = ffn_conv[l].reshape(3, n_ffn_chunks, ffn_cw).transpose(1, 0, 2)
        cb = ffn_conv_b[l].reshape(n_ffn_chunks, 1, ffn_cw)

        zg_c, zdn_c, zna_c, zab_c = _in_proj(ctx, mod3, w_in_p, True, ctx_row, tm_c)
        qkv_c, gb_c = _dn_prep(zdn_c, zab_c, dn_conv[l], aexp_row, dtb_row, gm384_sum, None, tm_c)
        yb_c, s_a, s_b = _dn_scan(qkv_c, zdn_c, gb_c, e_g, e_b, ng_row, gm384_mean, zeros_a, zeros_b)

        zg, zdn, zna, zab = _in_proj(x, mod3, w_in_p, False, ctx_row, tm)
        ya = _gmlp(zg, gm256_mean, lng, lnb, wsp, bsx, tm)
        qkv, gb = _dn_prep(zdn, zab, dn_conv[l], aexp_row, dtb_row, gm384_sum, rope_tabs, tm)
        yb, _, _ = _dn_scan(qkv, zdn, gb, e_g, e_b, ng_row, gm384_mean, s_a, s_b)
        yn = _na(zna, zna_c, tbl, 8)
        x1 = _out_proj(ya, yb, yn, w_out_b, x, mod3, lg1, lb1, False, ctx_row, tm, alpha)
        x_next = _ffn(x1, mod3, wa, wv, cw, cb, wd, lg2, lb2, False, ctx_row, tm, alpha)

        if ctx_out:
            ya_c = _gmlp(zg_c, gm256_mean, lng, lnb, wsp, bsx, tm_c)
            yn_c = _na_ctx(zna_c)
            c1 = _out_proj(ya_c, yb_c, yn_c, w_out_b, ctx, mod3, lg1, lb1, True, ctx_row, tm_c, alpha)
            ctx = _ffn(c1, mod3, wa, wv, cw, cb, wd, lg2, lb2, True, ctx_row, tm_c, alpha)
        x = x_next
    return x
```

```python
import functools
import math

import numpy as np
import jax
import jax.numpy as jnp
from jax import lax
from jax.experimental import pallas as pl
from jax.experimental.pallas import tpu as pltpu

F32 = jnp.float32
BF16 = jnp.bfloat16

HEAD_DIM = 64
GRID_W = 64
GMLP_HEADS = 4
GMLP_WIDTH = GMLP_HEADS * HEAD_DIM
GMLP_CHUNK = 128
DN_HEADS = 6
DN_WIDTH = DN_HEADS * HEAD_DIM
NA_HEADS = 6
NA_WIDTH = NA_HEADS * HEAD_DIM
DELTA_CHUNK = 64
WIN_ROWS = 8
WIN_COLS = 16
ROPE_BASE = 10000.0
EPS = 1e-6
NEG_BIG = -1e30
LANES = 128
VMEM_LIMIT = 56 * 1024 * 1024

GMLP_COLS = 2 * GMLP_WIDTH
DN_MAIN_COLS = 4 * DN_WIDTH
DN_AB_COLS = 4 * DN_HEADS
NA_COLS = 3 * NA_WIDTH


def _cparams(sem):
    return pltpu.CompilerParams(dimension_semantics=sem, vmem_limit_bytes=VMEM_LIMIT)


def _split2(x):
    hi = x.astype(BF16)
    lo = (x - hi.astype(F32)).astype(BF16)
    return hi, lo


def _split3(x):
    a = x.astype(BF16)
    r = x - a.astype(F32)
    b = r.astype(BF16)
    c = (r - b.astype(F32)).astype(BF16)
    return a, b, c


def _dot(a, b):
    return jnp.dot(a, b, preferred_element_type=F32)


def _dot_nt(a, b):
    return lax.dot_general(a, b, (((1,), (1,)), ((), ())), preferred_element_type=F32)


def _dot_tn(a, b):
    return lax.dot_general(a, b, (((0,), (0,)), ((), ())), preferred_element_type=F32)


def _dot_x3(x, m):
    a, b, c = _split3(x)
    return _dot(a, m) + _dot(b, m) + _dot(c, m)


def _dot_x2(x, m):
    a, b = _split2(x)
    return _dot(a, m) + _dot(b, m)


def _silu(x):
    return x * jax.nn.sigmoid(x)


def _layer_norm_rows(x):
    mu = jnp.mean(x, axis=-1, keepdims=True)
    xc = x - mu
    var = jnp.mean(xc * xc, axis=-1, keepdims=True)
    return xc * lax.rsqrt(var + EPS)


def _block_diag_np(n_blocks, rows, cols, value):
    m = np.zeros((n_blocks * rows, n_blocks * cols), np.float32)
    for g in range(n_blocks):
        m[g * rows:(g + 1) * rows, g * cols:(g + 1) * cols] = value
    return m


def _mod_kernel(c_ref, w_ref, b_ref, o_ref):
    a = _silu(c_ref[...])
    w = w_ref[0]
    a1, a2 = _split2(a)
    w1, w2 = _split2(w)
    o_ref[0] = _dot(a1, w1) + _dot(a1, w2) + _dot(a2, w1) + b_ref[0]


def _mod_all(c, c_ctx, w_ada, b_ada):
    n_layers, d, n6 = w_ada.shape
    bsz = c.shape[0]
    rows = jnp.concatenate([c, c_ctx[None, :], jnp.zeros((8 - bsz - 1, d), F32)], axis=0)
    tn = 1536
    return pl.pallas_call(
        _mod_kernel,
        grid=(n_layers, n6 // tn),
        in_specs=[
            pl.BlockSpec((8, d), lambda l, j: (0, 0)),
            pl.BlockSpec((1, d, tn), lambda l, j: (l, 0, j)),
            pl.BlockSpec((1, 1, tn), lambda l, j: (l, 0, j)),
        ],
        out_specs=pl.BlockSpec((1, 8, tn), lambda l, j: (l, 0, j)),
        out_shape=jax.ShapeDtypeStruct((n_layers, 8, n6), F32),
        compiler_params=_cparams(("parallel", "parallel")),
        name="adaln_mod",
    )(rows, w_ada, b_ada.reshape(n_layers, 1, n6))


def _mod_spec(chunk, d, is_ctx, ctx_row):
    if is_ctx:
        return pl.BlockSpec((1, 1, d), lambda b, i: (ctx_row, 0, chunk))
    return pl.BlockSpec((1, 1, d), lambda b, i: (b, 0, chunk))


def _inproj_kernel(x_ref, sh_ref, sc_ref, w_ref, zg_ref, zdn_ref, zna_ref, zab_ref):
    h = _layer_norm_rows(x_ref[0]) * (1.0 + sc_ref[0]) + sh_ref[0]
    h = h.astype(BF16)
    o1 = GMLP_COLS
    o2 = o1 + DN_MAIN_COLS
    o3 = o2 + NA_COLS
    zg_ref[0] = _dot(h, w_ref[:, 0:o1]).astype(BF16)
    zdn_ref[0] = _dot(h, w_ref[:, o1:o2]).astype(BF16)
    zna_ref[0] = _dot(h, w_ref[:, o2:o3]).astype(BF16)
    zab_ref[0] = _dot(h, w_ref[:, o3:o3 + LANES])


def _in_proj(x, mod3, w_in_p, is_ctx, ctx_row, tm):
    bsz, seq, d = x.shape
    ncols = w_in_p.shape[1]
    tok = lambda w: pl.BlockSpec((1, tm, w), lambda b, i: (b, i, 0))
    return pl.pallas_call(
        _inproj_kernel,
        grid=(bsz, seq // tm),
        in_specs=[
            tok(d),
            _mod_spec(0, d, is_ctx, ctx_row),
            _mod_spec(1, d, is_ctx, ctx_row),
            pl.BlockSpec((d, ncols), lambda b, i: (0, 0)),
        ],
        out_specs=[tok(GMLP_COLS), tok(DN_MAIN_COLS), tok(NA_COLS), tok(LANES)],
        out_shape=[
            jax.ShapeDtypeStruct((bsz, seq, GMLP_COLS), BF16),
            jax.ShapeDtypeStruct((bsz, seq, DN_MAIN_COLS), BF16),
            jax.ShapeDtypeStruct((bsz, seq, NA_COLS), BF16),
            jax.ShapeDtypeStruct((bsz, seq, LANES), F32),
        ],
        compiler_params=_cparams(("parallel", "parallel")),
        name="in_proj",
    )(x, mod3, mod3, w_in_p)


def _gmlp_kernel(z_ref, g_ref, lng_ref, lnb_ref, wsp_ref, bsx_ref, o_ref, *, tm):
    z = jax.nn.gelu(z_ref[0].astype(F32))
    u = z[:, :GMLP_WIDTH]
    v = z[:, GMLP_WIDTH:]
    gmat = g_ref[...]
    mu = _dot_x2(v, gmat)
    vc = v - mu
    var = _dot_x2(vc * vc, gmat)
    vn = vc * lax.rsqrt(var + EPS) * lng_ref[...] + lnb_ref[...]
    vn = vn.astype(BF16)
    nrow = GMLP_HEADS * GMLP_CHUNK
    ri = lax.broadcasted_iota(jnp.int32, (nrow, GMLP_WIDTH), 0) // GMLP_CHUNK
    ci = lax.broadcasted_iota(jnp.int32, (nrow, GMLP_WIDTH), 1) // HEAD_DIM
    diag = ri == ci
    wsp = wsp_ref[...]
    bsx = bsx_ref[...]
    for n in range(tm // GMLP_CHUNK):
        rows = slice(n * GMLP_CHUNK, (n + 1) * GMLP_CHUNK)
        vch = vn[rows]
        bd = jnp.where(diag, jnp.concatenate([vch] * GMLP_HEADS, axis=0), jnp.zeros_like(vch[:1, :1]))
        mixed = _dot(wsp, bd) + bsx
        o_ref[0, rows, :] = (u[rows] * mixed).astype(BF16)


def _gmlp(zg, gm256, lng, lnb, wsp, bsx, tm):
    bsz, seq, _ = zg.shape
    full = lambda a: pl.BlockSpec(a.shape, lambda b, i: (0,) * a.ndim)
    return pl.pallas_call(
        functools.partial(_gmlp_kernel, tm=tm),
        grid=(bsz, seq // tm),
        in_specs=[pl.BlockSpec((1, tm, GMLP_COLS), lambda b, i: (b, i, 0)),
                  full(gm256), full(lng), full(lnb), full(wsp), full(bsx)],
        out_specs=pl.BlockSpec((1, tm, GMLP_WIDTH), lambda b, i: (b, i, 0)),
        out_shape=jax.ShapeDtypeStruct((bsz, seq, GMLP_WIDTH), BF16),
        compiler_params=_cparams(("parallel", "parallel")),
        name="gmlp_mix",
    )(zg, gm256, lng, lnb, wsp, bsx)


def _dn_prep_kernel(*refs, tm, n_tiles, rotary, halo):
    if rotary:
        (z_ref, zp_ref, zn_ref, ab_ref, cw_ref, aexp_ref, dtb_ref, g_ref, pk_ref, cos_ref, sin_ref,
         qkv_ref, gp_ref) = refs
    else:
        (z_ref, zp_ref, zn_ref, ab_ref, cw_ref, aexp_ref, dtb_ref, g_ref, pk_ref, qkv_ref, gp_ref) = refs
    i = pl.program_id(1)
    has_prev = (i > 0).astype(F32)
    has_next = (i < n_tiles - 1).astype(F32)
    x_ext = jnp.concatenate([zp_ref[0].astype(F32) * has_prev, z_ref[0].astype(F32),
                             zn_ref[0].astype(F32) * has_next], axis=0)
    n_ext = tm + 2 * halo
    x_m1 = pltpu.roll(x_ext, 1, 0)[halo:halo + tm]
    x_p1 = pltpu.roll(x_ext, n_ext - 1, 0)[halo:halo + tm]
    x_0 = x_ext[halo:halo + tm]
    cw = cw_ref[...]
    y = _silu(x_m1 * cw[0:1] + x_0 * cw[1:2] + x_p1 * cw[2:3])
    q = y[:, 0:DN_WIDTH]
    k = y[:, DN_WIDTH:2 * DN_WIDTH]
    v = y[:, 2 * DN_WIDTH:3 * DN_WIDTH]
    gmat = g_ref[...]
    q = q * lax.rsqrt(_dot_x2(q * q, gmat) + EPS)
    k = k * lax.rsqrt(_dot_x2(k * k, gmat) + EPS)
    if rotary:
        cos2 = cos_ref[...]
        sin2 = sin_ref[...]
        lane = lax.broadcasted_iota(jnp.int32, (tm, LANES), 1)
        first = (lane % 32) < 16

        def rope(t):
            parts = []
            for p in range(DN_WIDTH // LANES):
                tp = t[:, p * LANES:(p + 1) * LANES]
                partner = jnp.where(first, pltpu.roll(tp, LANES - 16, 1), pltpu.roll(tp, 16, 1))
                parts.append(tp * cos2 + partner * sin2)
            return jnp.concatenate(parts, axis=1)

        q = rope(q)
        k = rope(k)
    q = q * (HEAD_DIM ** -0.5)
    qkv_ref[0, :, 0:DN_WIDTH] = q.astype(BF16)
    qkv_ref[0, :, DN_WIDTH:2 * DN_WIDTH] = k.astype(BF16)
    qkv_ref[0, :, 2 * DN_WIDTH:3 * DN_WIDTH] = v.astype(BF16)
    ab = ab_ref[0]
    lane_ab = lax.broadcasted_iota(jnp.int32, ab.shape, 1)
    xs = ab + dtb_ref[...]
    softplus = jnp.maximum(xs, 0.0) + jnp.log1p(jnp.exp(-jnp.abs(xs)))
    log_decay = -aexp_ref[...] * softplus
    beta = jax.nn.sigmoid(ab)
    ti = lax.broadcasted_iota(jnp.int32, (tm, tm), 0)
    tj = lax.broadcasted_iota(jnp.int32, (tm, tm), 1)
    same_chunk = (ti // DELTA_CHUNK) == (tj // DELTA_CHUNK)
    l_fwd = jnp.where(same_chunk, jnp.where(ti >= tj, 1.0, 0.0), 0.0).astype(BF16)
    l_bwd = jnp.where(same_chunk, jnp.where(ti <= tj, 1.0, 0.0), 0.0).astype(BF16)
    g1, g2, g3 = _split3(log_decay)
    gc_f = _dot(l_fwd, g1) + _dot(l_fwd, g2) + _dot(l_fwd, g3)
    gc_b = _dot(l_bwd, g1) + _dot(l_bwd, g2) + _dot(l_bwd, g3)
    gc = jnp.where(lane_ab < DN_HEADS, gc_f, gc_b)
    c1, c2, c3 = _split3(gc)
    b1, b2 = _split2(beta)
    packed = (_dot(c1, pk_ref[0]) + _dot(c2, pk_ref[1]) + _dot(c3, pk_ref[2])
              + _dot(b1, pk_ref[3]) + _dot(b2, pk_ref[4]))
    gp_ref[0] = packed.astype(BF16)


def _dn_prep(zdn, zab, conv_w, aexp_row, dtb_row, gm384, place, rope_tabs, tm):
    bsz, seq, _ = zdn.shape
    halo = 16
    n_tiles = seq // tm
    hb = tm // halo
    qkv_w = 3 * DN_WIDTH
    rotary = rope_tabs is not None
    full = lambda a: pl.BlockSpec(a.shape, lambda b, i: (0,) * a.ndim)
    in_specs = [
        pl.BlockSpec((1, tm, qkv_w), lambda b, i: (b, i, 0)),
        pl.BlockSpec((1, halo, qkv_w), lambda b, i: (b, jnp.maximum(i * hb - 1, 0), 0)),
        pl.BlockSpec((1, halo, qkv_w), lambda b, i: (b, jnp.minimum((i + 1) * hb, seq // halo - 1), 0)),
        pl.BlockSpec((1, tm, LANES), lambda b, i: (b, i, 0)),
        full(conv_w), full(aexp_row), full(dtb_row), full(gm384), full(place),
    ]
    args = [zdn, zdn, zdn, zab, conv_w, aexp_row, dtb_row, gm384, place]
    if rotary:
        in_specs += [pl.BlockSpec((tm, LANES), lambda b, i: (i, 0))] * 2
        args += list(rope_tabs)
    return pl.pallas_call(
        functools.partial(_dn_prep_kernel, tm=tm, n_tiles=n_tiles, rotary=rotary, halo=halo),
        grid=(bsz, n_tiles),
        in_specs=in_specs,
        out_specs=[pl.BlockSpec((1, tm, qkv_w), lambda b, i: (b, i, 0)),
                   pl.BlockSpec((1, tm, LANES), lambda b, i: (b, i, 0))],
        out_shape=[jax.ShapeDtypeStruct((bsz, seq, qkv_w), BF16),
                   jax.ShapeDtypeStruct((bsz, seq, LANES), BF16)],
        compiler_params=_cparams(("parallel", "parallel")),
        name="dn_prep",
    )(*args)


GATE_PIECES = 5
GATE_DIR_STRIDE = 32


def _dn_step(q, k, v, gcx, bx, st, masks):
    cs = DELTA_CHUNK
    causal, strict, eye, last_row, diag, diag_mul = masks
    grp = range(len(q))

    def bd(x):
        return jnp.concatenate([x.astype(BF16)] * 4, axis=0) * diag_mul

    qf = [q[g].astype(F32) for g in grp]
    kf = [k[g].astype(F32) for g in grp]
    vf = [v[g].astype(F32) for g in grp]
    gc_t = [jnp.sum(jnp.where(eye, gcx[g], 0.0), axis=0, keepdims=True) for g in grp]
    dm = [jnp.where(causal, jnp.exp(jnp.where(causal, gcx[g] - gc_t[g], 0.0)), 0.0) for g in grp]
    dms = [jnp.where(strict, dm[g], 0.0) for g in grp]
    gl = [jnp.sum(jnp.where(last_row, gcx[g], 0.0), axis=0, keepdims=True) for g in grp]
    egx = [jnp.exp(gcx[g]) for g in grp]
    kdx = [jnp.exp(gl[g] - gcx[g]) for g in grp]
    kb = [kf[g] * bx[g] for g in grp]
    vb = [vf[g] * bx[g] for g in grp]
    bk = [bd(k[g]) for g in grp]
    lhs = [jnp.concatenate([kb[g].astype(BF16), q[g]], axis=0) for g in grp]
    p1 = [_dot_nt(lhs[g], bk[g]) for g in grp]
    a_mat = [p1[g][:cs] * dms[g] for g in grp]
    attn = [p1[g][cs:] * dm[g] for g in grp]

    eye_f = jnp.where(eye, 1.0, 0.0)
    t_mat = [eye_f - a_mat[g] for g in grp]
    b_pow = [_dot(a_mat[g].astype(BF16), bd(a_mat[g])) for g in grp]
    for _ in range(int(math.log2(cs)) - 2):
        pp = [_dot(jnp.concatenate([t_mat[g], b_pow[g]], axis=0).astype(BF16), bd(b_pow[g])) for g in grp]
        t_mat = [t_mat[g] + pp[g][:cs] for g in grp]
        b_pow = [pp[g][cs:] for g in grp]
    t_mat = [t_mat[g] + _dot(t_mat[g].astype(BF16), bd(b_pow[g])) for g in grp]

    tb = [t_mat[g].astype(BF16) for g in grp]
    u = [_dot(tb[g], bd(vb[g])) for g in grp]
    wk = [_dot(tb[g], bd(kb[g] * egx[g])) for g in grp]
    lhs2 = [jnp.concatenate([wk[g], qf[g] * egx[g]], axis=0).astype(BF16) for g in grp]
    p2 = [_dot(lhs2[g], st[g].astype(BF16)) for g in grp]
    v_new = [u[g] - p2[g][:cs] for g in grp]
    o = [p2[g][cs:] + _dot(attn[g].astype(BF16), bd(v_new[g])) for g in grp]
    kd = [(kf[g] * kdx[g]).astype(BF16) for g in grp]
    st_new = [st[g] * jnp.exp(gl[g]) + jnp.where(diag, _dot_tn(kd[g], v_new[g].astype(BF16)), 0.0) for g in grp]
    return o, st_new


def _dn_masks(reverse):
    cs = DELTA_CHUNK
    wg = 4 * HEAD_DIM
    i3 = lax.broadcasted_iota(jnp.int32, (cs, wg), 0)
    j3 = lax.broadcasted_iota(jnp.int32, (cs, wg), 1) % HEAD_DIM
    if reverse:
        causal, strict, last_row = i3 <= j3, i3 < j3, i3 == 0
    else:
        causal, strict, last_row = i3 >= j3, i3 > j3, i3 == cs - 1
    ra = lax.broadcasted_iota(jnp.int32, (wg, wg), 0) // HEAD_DIM
    ca = lax.broadcasted_iota(jnp.int32, (wg, wg), 1) // HEAD_DIM
    diag = ra == ca
    diag_mul = jnp.where(diag, 1.0, 0.0).astype(BF16)
    return causal, strict, i3 == j3, last_row, diag, diag_mul


def _dn_scan_kernel(*refs, n_chunks, nb, reverse, finish):
    if finish:
        (qkv_ref, gp_ref, e_ref, s0_ref, of_ref, gate_ref, ng_ref, gm_ref, out_ref, s_out_ref, st_scr) = refs
    else:
        (qkv_ref, gp_ref, e_ref, s0_ref, out_ref, s_out_ref, st_scr) = refs
    w = DN_WIDTH
    wg = 4 * HEAD_DIM
    wr = w - wg
    n_grp = nb + nb // 2
    i = pl.program_id(1)

    @pl.when(i == 0)
    def _():
        st_scr[...] = s0_ref[...]

    masks = _dn_masks(reverse)
    e_comb = e_ref[...]
    qkv = [qkv_ref[bb] for bb in range(nb)]
    ex = [_dot(gp_ref[bb], e_comb) for bb in range(nb)]

    def groups(arrs, off):
        out = [a[:, off:off + wg] for a in arrs]
        for p in range(nb // 2):
            out.append(jnp.concatenate([arrs[2 * p][:, off + wg:off + w], arrs[2 * p + 1][:, off + wg:off + w]],
                                       axis=1))
        return out

    o_g, st_new = _dn_step(groups(qkv, 0), groups(qkv, w), groups(qkv, 2 * w), groups(ex, 0), groups(ex, w),
                           [st_scr[g] for g in range(n_grp)], masks)
    for g in range(n_grp):
        st_scr[g] = st_new[g]
    for bb in range(nb):
        rest = o_g[nb + bb // 2][:, (bb % 2) * wr:(bb % 2 + 1) * wr]
        o = jnp.concatenate([o_g[bb], rest], axis=1)
        if finish:
            ot = of_ref[bb] + o
            ms = _dot_x2(ot * ot, gm_ref[...])
            gate = gate_ref[bb].astype(F32)
            out_ref[bb] = (ot * lax.rsqrt(ms + EPS) * ng_ref[...] * _silu(gate)).astype(BF16)
        else:
            out_ref[bb] = o

    @pl.when(i == n_chunks - 1)
    def _():
        s_out_ref[...] = st_scr[...]


def _dn_scan_dir(qkv, gp, e_comb, s0, reverse, finish_args, nb):
    bsz, seq, _ = qkv.shape
    cs = DELTA_CHUNK
    n = seq // cs
    wg = 4 * HEAD_DIM
    n_grp = nb + nb // 2
    finish = finish_args is not None
    cidx = (lambda i: n - 1 - i) if reverse else (lambda i: i)
    full = lambda a: pl.BlockSpec(a.shape, lambda g, i: (0,) * a.ndim)
    tok = lambda wdt, col=0: pl.BlockSpec((nb, cs, wdt), lambda g, i: (g, cidx(i), col))
    st = pl.BlockSpec((None, n_grp, wg, wg), lambda g, i: (g, 0, 0, 0))
    in_specs = [tok(3 * DN_WIDTH), tok(LANES), full(e_comb), st]
    args = [qkv, gp, e_comb, s0]
    if finish:
        o_other, zdn, ng_row, gm_mean = finish_args
        in_specs += [tok(DN_WIDTH), tok(DN_WIDTH, 3), full(ng_row), full(gm_mean)]
        args += [o_other, zdn, ng_row, gm_mean]
    return pl.pallas_call(
        functools.partial(_dn_scan_kernel, n_chunks=n, nb=nb, reverse=reverse, finish=finish),
        grid=(bsz // nb, n),
        in_specs=in_specs,
        out_specs=[tok(DN_WIDTH), st],
        out_shape=[
            jax.ShapeDtypeStruct((bsz, seq, DN_WIDTH), BF16 if finish else F32),
            jax.ShapeDtypeStruct((bsz // nb, n_grp, wg, wg), F32),
        ],
        scratch_shapes=[pltpu.VMEM((n_grp, wg, wg), F32)],
        compiler_params=_cparams(("parallel", "arbitrary")),
        name="dn_scan_bwd" if reverse else "dn_scan_fwd",
    )(*args)


def _dn_scan(qkv, zdn, gp, e_fwd, e_bwd, ng_row, gm_mean, s0, nb):
    o_f, s_f = _dn_scan_dir(qkv, gp, e_fwd, s0[0], False, None, nb)
    y, s_b = _dn_scan_dir(qkv, gp, e_bwd, s0[1], True, (o_f, zdn, ng_row, gm_mean), nb)
    return y, (s_f, s_b)


def _na_kernel(q_ref, k_ref, v_ref, kc_ref, vc_ref, tbl_ref, o_ref, *, rows_per_step, n_rows, win_rows):
    i = pl.program_id(1)
    gw = GRID_W
    lane = lax.broadcasted_iota(jnp.int32, (1, LANES), 1)
    first_head = lane < HEAD_DIM
    scale = HEAD_DIM ** -0.5

    def row_body(rq, carry):
        r = i * rows_per_step + rq
        rs = jnp.clip(r - win_rows // 2, 0, n_rows - win_rows)
        var = r - rs
        k0 = pl.multiple_of(rs * gw, gw)
        q0 = pl.multiple_of(rq * gw, gw)
        for p in range(NA_WIDTH // LANES):
            ls = slice(p * LANES, (p + 1) * LANES)
            qp = q_ref[0, pl.ds(q0, gw), ls] * scale
            kw = k_ref[0, pl.ds(k0, win_rows * gw), ls]
            vw = v_ref[0, pl.ds(k0, win_rows * gw), ls]
            kc = kc_ref[0, :, ls]
            vc = vc_ref[0, :, ls]
            outs = []
            for hh in range(2):
                msk = first_head if hh == 0 else jnp.logical_not(first_head)
                qm = jnp.where(msk, qp, jnp.zeros_like(qp[:1, :1]))
                s_w = _dot_nt(qm, kw) + tbl_ref[2 * p + hh, var]
                s_c = _dot_nt(qm, kc)
                m = jnp.maximum(jnp.max(s_w, axis=-1, keepdims=True), jnp.max(s_c, axis=-1, keepdims=True))
                p_w = jnp.exp(s_w - m)
                p_c = jnp.exp(s_c - m)
                den = jnp.sum(p_w, axis=-1, keepdims=True) + jnp.sum(p_c, axis=-1, keepdims=True)
                o = _dot(p_w.astype(BF16), vw) + _dot(p_c.astype(BF16), vc)
                outs.append(o / den)
            o_ref[0, pl.ds(q0, gw), ls] = jnp.where(first_head, outs[0], outs[1]).astype(BF16)
        return carry

    lax.fori_loop(0, rows_per_step, row_body, 0)


def _na(zna, zna_ctx, tbl, rows_per_step):
    bsz, seq, _ = zna.shape
    n_ctx = zna_ctx.shape[1]
    n_rows = seq // GRID_W
    win_rows = min(WIN_ROWS, n_rows)
    tq = rows_per_step * GRID_W
    return pl.pallas_call(
        functools.partial(_na_kernel, rows_per_step=rows_per_step, n_rows=n_rows, win_rows=win_rows),
        grid=(bsz, n_rows // rows_per_step),
        in_specs=[
            pl.BlockSpec((1, tq, NA_WIDTH), lambda b, i: (b, i, 0)),
            pl.BlockSpec((1, seq, NA_WIDTH), lambda b, i: (b, 0, 1)),
            pl.BlockSpec((1, seq, NA_WIDTH), lambda b, i: (b, 0, 2)),
            pl.BlockSpec((1, n_ctx, NA_WIDTH), lambda b, i: (b, 0, 1)),
            pl.BlockSpec((1, n_ctx, NA_WIDTH), lambda b, i: (b, 0, 2)),
            pl.BlockSpec(tbl.shape, lambda b, i: (0, 0, 0, 0)),
        ],
        out_specs=pl.BlockSpec((1, tq, NA_WIDTH), lambda b, i: (b, i, 0)),
        out_shape=jax.ShapeDtypeStruct((bsz, seq, NA_WIDTH), BF16),
        compiler_params=_cparams(("parallel", "arbitrary")),
        name="nbr_attn",
    )(zna, zna, zna, zna_ctx, zna_ctx, tbl)


def _na_ctx_kernel(q_ref, k_ref, v_ref, o_ref):
    lane = lax.broadcasted_iota(jnp.int32, (1, LANES), 1)
    first_head = lane < HEAD_DIM
    scale = HEAD_DIM ** -0.5
    for p in range(NA_WIDTH // LANES):
        ls = slice(p * LANES, (p + 1) * LANES)
        qp = q_ref[0, :, ls] * scale
        kp = k_ref[0, :, ls]
        vp = v_ref[0, :, ls]
        outs = []
        for hh in range(2):
            msk = first_head if hh == 0 else jnp.logical_not(first_head)
            qm = jnp.where(msk, qp, jnp.zeros_like(qp[:1, :1]))
            s = _dot_nt(qm, kp)
            e = jnp.exp(s - jnp.max(s, axis=-1, keepdims=True))
            den = jnp.sum(e, axis=-1, keepdims=True)
            outs.append(_dot(e.astype(BF16), vp) / den)
        o_ref[0, :, ls] = jnp.where(first_head, outs[0], outs[1]).astype(BF16)


def _na_ctx(zna_ctx):
    bsz, n_ctx, _ = zna_ctx.shape
    col = lambda j: pl.BlockSpec((1, n_ctx, NA_WIDTH), lambda b: (b, 0, j))
    return pl.pallas_call(
        _na_ctx_kernel,
        grid=(bsz,),
        in_specs=[col(0), col(1), col(2)],
        out_specs=pl.BlockSpec((1, n_ctx, NA_WIDTH), lambda b: (b, 0, 0)),
        out_shape=jax.ShapeDtypeStruct((bsz, n_ctx, NA_WIDTH), BF16),
        compiler_params=_cparams(("parallel",)),
        name="ctx_attn",
    )(zna_ctx, zna_ctx, zna_ctx)


def _outproj_kernel(ya_ref, yb_ref, yn_ref, w_ref, x_ref, g_ref, lg_ref, lb_ref, o_ref, *, alpha):
    ycat = jnp.concatenate([ya_ref[0], yb_ref[0], yn_ref[0]], axis=1)
    y = _dot(ycat, w_ref[...])
    s = alpha * x_ref[0] + g_ref[0] * y
    o_ref[0] = _layer_norm_rows(s) * lg_ref[...] + lb_ref[...]


def _out_proj(ya, yb, yn, w_out_b, x, mod3, lg, lb, is_ctx, ctx_row, tm, alpha):
    bsz, seq, d = x.shape
    tok = lambda w: pl.BlockSpec((1, tm, w), lambda b, i: (b, i, 0))
    full = lambda a: pl.BlockSpec(a.shape, lambda b, i: (0,) * a.ndim)
    return pl.pallas_call(
        functools.partial(_outproj_kernel, alpha=alpha),
        grid=(bsz, seq // tm),
        in_specs=[tok(GMLP_WIDTH), tok(DN_WIDTH), tok(NA_WIDTH), full(w_out_b), tok(d),
                  _mod_spec(2, d, is_ctx, ctx_row), full(lg), full(lb)],
        out_specs=tok(d),
        out_shape=jax.ShapeDtypeStruct((bsz, seq, d), F32),
        compiler_params=_cparams(("parallel", "parallel")),
        name="out_proj",
    )(ya, yb, yn, w_out_b, x, mod3, lg, lb)


def _ffn_kernel(x_ref, xp_ref, xn_ref, sh_ref, sc_ref, g_ref, wa_ref, wv_ref, cw_ref, cb_ref, wd_ref,
                lg_ref, lb_ref, o_ref, acc_ref, *, tm, n_tiles, n_chunks, alpha):
    halo = 8
    i = pl.program_id(1)
    has_prev = (i > 0).astype(F32)
    has_next = (i < n_tiles - 1).astype(F32)
    x = x_ref[0]
    sc = 1.0 + sc_ref[0]
    sh = sh_ref[0]
    h_mid = _layer_norm_rows(x) * sc + sh
    h_prev = (_layer_norm_rows(xp_ref[0]) * sc + sh) * has_prev
    h_next = (_layer_norm_rows(xn_ref[0]) * sc + sh) * has_next
    h_ext = jnp.concatenate([h_prev, h_mid, h_next], axis=0).astype(BF16)
    h_b = h_mid.astype(BF16)
    n_ext = tm + 2 * halo
    acc_ref[...] = jnp.zeros_like(acc_ref)

    def chunk_body(j, carry):
        a = _dot(h_ext, wa_ref[j])
        vv = _dot(h_b, wv_ref[j])
        cw = cw_ref[j]
        a_m1 = pltpu.roll(a, 1, 0)[halo:halo + tm]
        a_p1 = pltpu.roll(a, n_ext - 1, 0)[halo:halo + tm]
        conv = a_m1 * cw[0:1] + a[halo:halo + tm] * cw[1:2] + a_p1 * cw[2:3] + cb_ref[j]
        gated = (_silu(conv) * vv).astype(BF16)
        acc_ref[...] += _dot(gated, wd_ref[j])
        return carry

    lax.fori_loop(0, n_chunks, chunk_body, 0)
    s = alpha * x + g_ref[0] * acc_ref[...]
    o_ref[0] = _layer_norm_rows(s) * lg_ref[...] + lb_ref[...]


def _ffn(x, mod3, wa, wv, cw, cb, wd, lg, lb, is_ctx, ctx_row, tm, alpha):
    bsz, seq, d = x.shape
    n_tiles = seq // tm
    n_chunks = wa.shape[0]
    hb = tm // 8
    tok = pl.BlockSpec((1, tm, d), lambda b, i: (b, i, 0))
    full = lambda a: pl.BlockSpec(a.shape, lambda b, i: (0,) * a.ndim)
    return pl.pallas_call(
        functools.partial(_ffn_kernel, tm=tm, n_tiles=n_tiles, n_chunks=n_chunks, alpha=alpha),
        grid=(bsz, n_tiles),
        in_specs=[
            tok,
            pl.BlockSpec((1, 8, d), lambda b, i: (b, jnp.maximum(i * hb - 1, 0), 0)),
            pl.BlockSpec((1, 8, d), lambda b, i: (b, jnp.minimum((i + 1) * hb, seq // 8 - 1), 0)),
            _mod_spec(3, d, is_ctx, ctx_row), _mod_spec(4, d, is_ctx, ctx_row), _mod_spec(5, d, is_ctx, ctx_row),
            full(wa), full(wv), full(cw), full(cb), full(wd), full(lg), full(lb),
        ],
        out_specs=tok,
        out_shape=jax.ShapeDtypeStruct((bsz, seq, d), F32),
        scratch_shapes=[pltpu.VMEM((tm, d), F32)],
        compiler_params=_cparams(("parallel", "parallel")),
        name="conv_glu",
    )(x, x, x, mod3, mod3, mod3, wa, wv, cw, cb, wd, lg, lb)


def _rope_tables(seq):
    half = HEAD_DIM // 2
    nf = half // 2
    pos = jnp.arange(seq)
    inv = ROPE_BASE ** (-jnp.arange(nf, dtype=F32) / nf)
    ang_r = (pos // GRID_W)[:, None].astype(F32) * inv
    ang_c = (pos % GRID_W)[:, None].astype(F32) * inv
    cos_h = jnp.concatenate([jnp.cos(ang_r)] * 2 + [jnp.cos(ang_c)] * 2, axis=1)
    sin_h = jnp.concatenate([-jnp.sin(ang_r), jnp.sin(ang_r), -jnp.sin(ang_c), jnp.sin(ang_c)], axis=1)
    reps = LANES // HEAD_DIM
    return jnp.tile(cos_h, (1, reps)), jnp.tile(sin_h, (1, reps))


def _bias_table(rpb, n_rows):
    wr = min(WIN_ROWS, n_rows)
    nh = rpb.shape[0]
    n_dr = 2 * WIN_ROWS - 1
    col = np.arange(GRID_W)
    cs = np.clip(col - WIN_COLS // 2, 0, GRID_W - WIN_COLS)
    valid = (col[None, :] >= cs[:, None]) & (col[None, :] < cs[:, None] + WIN_COLS)
    pad = GRID_W - WIN_COLS
    padded = jnp.full((nh, n_dr, 2 * GRID_W - 1), NEG_BIG, F32)
    padded = padded.at[:, :, pad:pad + 2 * WIN_COLS - 1].set(rpb.astype(F32))
    t = jnp.stack([padded[:, :, GRID_W - 1 - c:2 * GRID_W - 1 - c] for c in range(GRID_W)], axis=2)
    t = jnp.where(valid[None, None], t, NEG_BIG)
    per_var = []
    for var in range(wr):
        lo = WIN_ROWS - 1 - var
        per_var.append(t[:, lo:lo + wr].transpose(0, 2, 1, 3).reshape(nh, GRID_W, wr * GRID_W))
    return jnp.stack(per_var, axis=1)


def _gate_mats():
    place = np.zeros((GATE_PIECES, LANES, LANES), np.float32)
    expand = np.zeros((2, LANES, 2 * DN_WIDTH), np.float32)
    for dd in range(2):
        for h in range(DN_HEADS):
            cols = slice(h * HEAD_DIM, (h + 1) * HEAD_DIM)
            for kk in range(3):
                dst = dd * GATE_DIR_STRIDE + kk * DN_HEADS + h
                place[kk, dd * DN_HEADS + h, dst] = 1.0
                expand[dd, dst, cols] = 1.0
            for kk in range(2):
                dst = dd * GATE_DIR_STRIDE + (3 + kk) * DN_HEADS + h
                place[3 + kk, 2 * DN_HEADS + dd * DN_HEADS + h, dst] = 1.0
                expand[dd, dst, DN_WIDTH + h * HEAD_DIM:DN_WIDTH + (h + 1) * HEAD_DIM] = 1.0
    return jnp.asarray(place, BF16), jnp.asarray(expand[0], BF16), jnp.asarray(expand[1], BF16)


def _pad_lanes(v):
    return jnp.zeros((1, LANES), F32).at[0, :v.shape[0]].set(v.astype(F32))


def _token_tile(seq):
    return 512 if seq % 512 == 0 else 256


def kernel(x, c, ctx, c_ctx, w_ada, b_ada, w_in, gmlp_ln_g, gmlp_ln_b, gmlp_ws, gmlp_bs, dn_conv, dn_a_log,
           dn_dt_bias, dn_norm_g, na_rpb, w_out, ln1_g, ln1_b, ffn_up, ffn_conv, ffn_conv_b, ffn_down,
           ln2_g, ln2_b):
    depth = w_ada.shape[0]
    bsz, seq, d = x.shape
    n_ctx = ctx.shape[1]
    d_ff = ffn_down.shape[1]
    alpha = (2 * depth) ** 0.25
    ctx_row = bsz
    ffn_cw = 256
    n_ffn_chunks = d_ff // ffn_cw
    tm = _token_tile(seq)
    tm_c = _token_tile(n_ctx)

    mod = _mod_all(c, c_ctx, w_ada, b_ada)
    rope_tabs = _rope_tables(seq)
    place, e_fwd, e_bwd = _gate_mats()
    gm256_mean = jnp.asarray(_block_diag_np(GMLP_HEADS, HEAD_DIM, HEAD_DIM, 1.0 / HEAD_DIM), BF16)
    gm384_sum = jnp.asarray(_block_diag_np(DN_HEADS, HEAD_DIM, HEAD_DIM, 1.0), BF16)
    gm384_mean = jnp.asarray(_block_diag_np(DN_HEADS, HEAD_DIM, HEAD_DIM, 1.0 / HEAD_DIM), BF16)
    assert bsz % 2 == 0, "the DeltaNet scan pairs batch elements"
    nb = 4 if bsz % 4 == 0 else 2
    s_zero = jnp.zeros((bsz // nb, nb + nb // 2, 4 * HEAD_DIM, 4 * HEAD_DIM), F32)
    o1 = GMLP_COLS
    o2 = o1 + DN_MAIN_COLS
    o3 = o2 + DN_AB_COLS

    for l in range(depth):
        ctx_out = l < depth - 1
        mod3 = mod[l].reshape(8, 1, 6 * d)
        wl = w_in[l]
        w_in_p = jnp.concatenate(
            [wl[:, :o2], wl[:, o3:], wl[:, o2:o3], jnp.zeros((d, LANES - DN_AB_COLS), F32)], axis=1).astype(BF16)
        lng = gmlp_ln_g[l].reshape(1, GMLP_WIDTH)
        lnb = gmlp_ln_b[l].reshape(1, GMLP_WIDTH)
        wsp = gmlp_ws[l].transpose(1, 0, 2).reshape(GMLP_CHUNK, GMLP_HEADS * GMLP_CHUNK).astype(BF16)
        bsx = jnp.repeat(gmlp_bs[l].T, HEAD_DIM, axis=1)
        aexp_row = _pad_lanes(jnp.exp(dn_a_log[l].astype(F32)).reshape(-1))
        dtb_row = _pad_lanes(dn_dt_bias[l].reshape(-1))
        ng_row = jnp.tile(dn_norm_g[l].astype(F32), DN_HEADS).reshape(1, DN_WIDTH)
        tbl = _bias_table(na_rpb[l], seq // GRID_W)
        w_out_b = w_out[l].astype(BF16)
        lg1 = ln1_g[l].reshape(1, d)
        lb1 = ln1_b[l].reshape(1, d)
        lg2 = ln2_g[l].reshape(1, d)
        lb2 = ln2_b[l].reshape(1, d)
        up = ffn_up[l]
        wa = up[:, :d_ff].reshape(d, n_ffn_chunks, ffn_cw).transpose(1, 0, 2).astype(BF16)
        wv = up[:, d_ff:].reshape(d, n_ffn_chunks, ffn_cw).transpose(1, 0, 2).astype(BF16)
        wd = ffn_down[l].reshape(n_ffn_chunks, ffn_cw, d).astype(BF16)
        cw = ffn_conv[l].reshape(3, n_ffn_chunks, ffn_cw).transpose(1, 0, 2)
        cb = ffn_conv_b[l].reshape(n_ffn_chunks, 1, ffn_cw)

        zg_c, zdn_c, zna_c, zab_c = _in_proj(ctx, mod3, w_in_p, True, ctx_row, tm_c)
        qkv_c, gp_c = _dn_prep(zdn_c, zab_c, dn_conv[l], aexp_row, dtb_row, gm384_sum, place, None, tm_c)
        yb_c, s_ctx = _dn_scan(qkv_c, zdn_c, gp_c, e_fwd, e_bwd, ng_row, gm384_mean, (s_zero, s_zero), nb)

        zg, zdn, zna, zab = _in_proj(x, mod3, w_in_p, False, ctx_row, tm)
        ya = _gmlp(zg, gm256_mean, lng, lnb, wsp, bsx, tm)
        qkv, gp = _dn_prep(zdn, zab, dn_conv[l], aexp_row, dtb_row, gm384_sum, place, rope_tabs, tm)
        yb, _ = _dn_scan(qkv, zdn, gp, e_fwd, e_bwd, ng_row, gm384_mean, s_ctx, nb)
        yn = _na(zna, zna_c, tbl, 8)
        x1 = _out_proj(ya, yb, yn, w_out_b, x, mod3, lg1, lb1, False, ctx_row, tm, alpha)
        x_next = _ffn(x1, mod3, wa, wv, cw, cb, wd, lg2, lb2, False, ctx_row, tm, alpha)

        if ctx_out:
            ya_c = _gmlp(zg_c, gm256_mean, lng, lnb, wsp, bsx, tm_c)
            yn_c = _na_ctx(zna_c)
            c1 = _out_proj(ya_c, yb_c, yn_c, w_out_b, ctx, mod3, lg1, lb1, True, ctx_row, tm_c, alpha)
            ctx = _ffn(c1, mod3, wa, wv, cw, cb, wd, lg2, lb2, True, ctx_row, tm_c, alpha)
        x = x_next
    return x
```

```python
import functools
import math

import numpy as np
import jax
import jax.numpy as jnp
from jax import lax
from jax.experimental import pallas as pl
from jax.experimental.pallas import tpu as pltpu

F32 = jnp.float32
BF16 = jnp.bfloat16

HEAD_DIM = 64
GRID_W = 64
GMLP_HEADS = 4
GMLP_WIDTH = GMLP_HEADS * HEAD_DIM
GMLP_CHUNK = 128
DN_HEADS = 6
DN_WIDTH = DN_HEADS * HEAD_DIM
NA_HEADS = 6
NA_WIDTH = NA_HEADS * HEAD_DIM
DELTA_CHUNK = 64
WIN_ROWS = 8
WIN_COLS = 16
ROPE_BASE = 10000.0
EPS = 1e-6
NEG_BIG = -1e30
LANES = 128
VMEM_LIMIT = 56 * 1024 * 1024

GMLP_COLS = 2 * GMLP_WIDTH
DN_MAIN_COLS = 4 * DN_WIDTH
DN_AB_COLS = 4 * DN_HEADS
NA_COLS = 3 * NA_WIDTH


def _cparams(sem):
    return pltpu.CompilerParams(dimension_semantics=sem, vmem_limit_bytes=VMEM_LIMIT)


def _split2(x):
    hi = x.astype(BF16)
    lo = (x - hi.astype(F32)).astype(BF16)
    return hi, lo


def _split3(x):
    a = x.astype(BF16)
    r = x - a.astype(F32)
    b = r.astype(BF16)
    c = (r - b.astype(F32)).astype(BF16)
    return a, b, c


def _dot(a, b):
    return jnp.dot(a, b, preferred_element_type=F32)


def _dot_nt(a, b):
    return lax.dot_general(a, b, (((1,), (1,)), ((), ())), preferred_element_type=F32)


def _dot_tn(a, b):
    return lax.dot_general(a, b, (((0,), (0,)), ((), ())), preferred_element_type=F32)


def _dot_x3(x, m):
    a, b, c = _split3(x)
    return _dot(a, m) + _dot(b, m) + _dot(c, m)


def _dot_x2(x, m):
    a, b = _split2(x)
    return _dot(a, m) + _dot(b, m)


def _silu(x):
    return x * jax.nn.sigmoid(x)


def _layer_norm_rows(x):
    mu = jnp.mean(x, axis=-1, keepdims=True)
    xc = x - mu
    var = jnp.mean(xc * xc, axis=-1, keepdims=True)
    return xc * lax.rsqrt(var + EPS)


def _block_diag_np(n_blocks, rows, cols, value):
    m = np.zeros((n_blocks * rows, n_blocks * cols), np.float32)
    for g in range(n_blocks):
        m[g * rows:(g + 1) * rows, g * cols:(g + 1) * cols] = value
    return m


def _mod_kernel(c_ref, w_ref, b_ref, o_ref):
    a = _silu(c_ref[...])
    w = w_ref[0]
    a1, a2 = _split2(a)
    w1, w2 = _split2(w)
    o_ref[0] = _dot(a1, w1) + _dot(a1, w2) + _dot(a2, w1) + b_ref[0]


def _mod_all(c, c_ctx, w_ada, b_ada):
    n_layers, d, n6 = w_ada.shape
    bsz = c.shape[0]
    rows = jnp.concatenate([c, c_ctx[None, :], jnp.zeros((8 - bsz - 1, d), F32)], axis=0)
    tn = 1536
    return pl.pallas_call(
        _mod_kernel,
        grid=(n_layers, n6 // tn),
        in_specs=[
            pl.BlockSpec((8, d), lambda l, j: (0, 0)),
            pl.BlockSpec((1, d, tn), lambda l, j: (l, 0, j)),
            pl.BlockSpec((1, 1, tn), lambda l, j: (l, 0, j)),
        ],
        out_specs=pl.BlockSpec((1, 8, tn), lambda l, j: (l, 0, j)),
        out_shape=jax.ShapeDtypeStruct((n_layers, 8, n6), F32),
        compiler_params=_cparams(("parallel", "parallel")),
        name="adaln_mod",
    )(rows, w_ada, b_ada.reshape(n_layers, 1, n6))


def _mod_spec(chunk, d, is_ctx, ctx_row):
    if is_ctx:
        return pl.BlockSpec((1, 1, d), lambda b, i: (ctx_row, 0, chunk))
    return pl.BlockSpec((1, 1, d), lambda b, i: (b, 0, chunk))


def _inproj_kernel(x_ref, sh_ref, sc_ref, w_ref, zg_ref, zdn_ref, zna_ref, zab_ref):
    h = _layer_norm_rows(x_ref[0]) * (1.0 + sc_ref[0]) + sh_ref[0]
    h = h.astype(BF16)
    o1 = GMLP_COLS
    o2 = o1 + DN_MAIN_COLS
    o3 = o2 + NA_COLS
    zg_ref[0] = _dot(h, w_ref[:, 0:o1]).astype(BF16)
    zdn_ref[0] = _dot(h, w_ref[:, o1:o2]).astype(BF16)
    zna_ref[0] = _dot(h, w_ref[:, o2:o3]).astype(BF16)
    zab_ref[0] = _dot(h, w_ref[:, o3:o3 + LANES])


def _in_proj(x, mod3, w_in_p, is_ctx, ctx_row, tm):
    bsz, seq, d = x.shape
    ncols = w_in_p.shape[1]
    tok = lambda w: pl.BlockSpec((1, tm, w), lambda b, i: (b, i, 0))
    return pl.pallas_call(
        _inproj_kernel,
        grid=(bsz, seq // tm),
        in_specs=[
            tok(d),
            _mod_spec(0, d, is_ctx, ctx_row),
            _mod_spec(1, d, is_ctx, ctx_row),
            pl.BlockSpec((d, ncols), lambda b, i: (0, 0)),
        ],
        out_specs=[tok(GMLP_COLS), tok(DN_MAIN_COLS), tok(NA_COLS), tok(LANES)],
        out_shape=[
            jax.ShapeDtypeStruct((bsz, seq, GMLP_COLS), BF16),
            jax.ShapeDtypeStruct((bsz, seq, DN_MAIN_COLS), BF16),
            jax.ShapeDtypeStruct((bsz, seq, NA_COLS), BF16),
            jax.ShapeDtypeStruct((bsz, seq, LANES), F32),
        ],
        compiler_params=_cparams(("parallel", "parallel")),
        name="in_proj",
    )(x, mod3, mod3, w_in_p)


def _gmlp_kernel(z_ref, g_ref, lng_ref, lnb_ref, wsp_ref, bsx_ref, o_ref, *, tm):
    z = jax.nn.gelu(z_ref[0].astype(F32))
    u = z[:, :GMLP_WIDTH]
    v = z[:, GMLP_WIDTH:]
    gmat = g_ref[...]
    mu = _dot_x2(v, gmat)
    vc = v - mu
    var = _dot_x2(vc * vc, gmat)
    vn = vc * lax.rsqrt(var + EPS) * lng_ref[...] + lnb_ref[...]
    vn = vn.astype(BF16)
    nrow = GMLP_HEADS * GMLP_CHUNK
    ri = lax.broadcasted_iota(jnp.int32, (nrow, GMLP_WIDTH), 0) // GMLP_CHUNK
    ci = lax.broadcasted_iota(jnp.int32, (nrow, GMLP_WIDTH), 1) // HEAD_DIM
    diag = ri == ci
    wsp = wsp_ref[...]
    bsx = bsx_ref[...]
    for n in range(tm // GMLP_CHUNK):
        rows = slice(n * GMLP_CHUNK, (n + 1) * GMLP_CHUNK)
        vch = vn[rows]
        bd = jnp.where(diag, jnp.concatenate([vch] * GMLP_HEADS, axis=0), jnp.zeros_like(vch[:1, :1]))
        mixed = _dot(wsp, bd) + bsx
        o_ref[0, rows, :] = (u[rows] * mixed).astype(BF16)


def _gmlp(zg, gm256, lng, lnb, wsp, bsx, tm):
    bsz, seq, _ = zg.shape
    full = lambda a: pl.BlockSpec(a.shape, lambda b, i: (0,) * a.ndim)
    return pl.pallas_call(
        functools.partial(_gmlp_kernel, tm=tm),
        grid=(bsz, seq // tm),
        in_specs=[pl.BlockSpec((1, tm, GMLP_COLS), lambda b, i: (b, i, 0)),
                  full(gm256), full(lng), full(lnb), full(wsp), full(bsx)],
        out_specs=pl.BlockSpec((1, tm, GMLP_WIDTH), lambda b, i: (b, i, 0)),
        out_shape=jax.ShapeDtypeStruct((bsz, seq, GMLP_WIDTH), BF16),
        compiler_params=_cparams(("parallel", "parallel")),
        name="gmlp_mix",
    )(zg, gm256, lng, lnb, wsp, bsx)


def _dn_prep_kernel(*refs, tm, n_tiles, rotary, halo):
    if rotary:
        (z_ref, zp_ref, zn_ref, ab_ref, cw_ref, aexp_ref, dtb_ref, g_ref, pk_ref, cos_ref, sin_ref,
         qkv_ref, gp_ref) = refs
    else:
        (z_ref, zp_ref, zn_ref, ab_ref, cw_ref, aexp_ref, dtb_ref, g_ref, pk_ref, qkv_ref, gp_ref) = refs
    i = pl.program_id(1)
    has_prev = (i > 0).astype(F32)
    has_next = (i < n_tiles - 1).astype(F32)
    x_ext = jnp.concatenate([zp_ref[0].astype(F32) * has_prev, z_ref[0].astype(F32),
                             zn_ref[0].astype(F32) * has_next], axis=0)
    n_ext = tm + 2 * halo
    x_m1 = pltpu.roll(x_ext, 1, 0)[halo:halo + tm]
    x_p1 = pltpu.roll(x_ext, n_ext - 1, 0)[halo:halo + tm]
    x_0 = x_ext[halo:halo + tm]
    cw = cw_ref[...]
    y = _silu(x_m1 * cw[0:1] + x_0 * cw[1:2] + x_p1 * cw[2:3])
    q = y[:, 0:DN_WIDTH]
    k = y[:, DN_WIDTH:2 * DN_WIDTH]
    v = y[:, 2 * DN_WIDTH:3 * DN_WIDTH]
    gmat = g_ref[...]
    q = q * lax.rsqrt(_dot_x2(q * q, gmat) + EPS)
    k = k * lax.rsqrt(_dot_x2(k * k, gmat) + EPS)
    if rotary:
        cos2 = cos_ref[...]
        sin2 = sin_ref[...]
        lane = lax.broadcasted_iota(jnp.int32, (tm, LANES), 1)
        first = (lane % 32) < 16

        def rope(t):
            parts = []
            for p in range(DN_WIDTH // LANES):
                tp = t[:, p * LANES:(p + 1) * LANES]
                partner = jnp.where(first, pltpu.roll(tp, LANES - 16, 1), pltpu.roll(tp, 16, 1))
                parts.append(tp * cos2 + partner * sin2)
            return jnp.concatenate(parts, axis=1)

        q = rope(q)
        k = rope(k)
    q = q * (HEAD_DIM ** -0.5)
    qkv_ref[0, :, 0:DN_WIDTH] = q.astype(BF16)
    qkv_ref[0, :, DN_WIDTH:2 * DN_WIDTH] = k.astype(BF16)
    qkv_ref[0, :, 2 * DN_WIDTH:3 * DN_WIDTH] = v.astype(BF16)
    ab = ab_ref[0]
    lane_ab = lax.broadcasted_iota(jnp.int32, ab.shape, 1)
    xs = ab + dtb_ref[...]
    softplus = jnp.maximum(xs, 0.0) + jnp.log1p(jnp.exp(-jnp.abs(xs)))
    log_decay = -aexp_ref[...] * softplus
    beta = jax.nn.sigmoid(ab)
    ti = lax.broadcasted_iota(jnp.int32, (tm, tm), 0)
    tj = lax.broadcasted_iota(jnp.int32, (tm, tm), 1)
    same_chunk = (ti // DELTA_CHUNK) == (tj // DELTA_CHUNK)
    l_fwd = jnp.where(same_chunk, jnp.where(ti >= tj, 1.0, 0.0), 0.0).astype(BF16)
    l_bwd = jnp.where(same_chunk, jnp.where(ti <= tj, 1.0, 0.0), 0.0).astype(BF16)
    g1, g2, g3 = _split3(log_decay)
    gc_f = _dot(l_fwd, g1) + _dot(l_fwd, g2) + _dot(l_fwd, g3)
    gc_b = _dot(l_bwd, g1) + _dot(l_bwd, g2) + _dot(l_bwd, g3)
    gc = jnp.where(lane_ab < DN_HEADS, gc_f, gc_b)
    c1, c2, c3 = _split3(gc)
    b1, b2 = _split2(beta)
    packed = (_dot(c1, pk_ref[0]) + _dot(c2, pk_ref[1]) + _dot(c3, pk_ref[2])
              + _dot(b1, pk_ref[3]) + _dot(b2, pk_ref[4]))
    gp_ref[0] = packed.astype(BF16)


def _dn_prep(zdn, zab, conv_w, aexp_row, dtb_row, gm384, place, rope_tabs, tm):
    bsz, seq, _ = zdn.shape
    halo = 16
    n_tiles = seq // tm
    hb = tm // halo
    qkv_w = 3 * DN_WIDTH
    rotary = rope_tabs is not None
    full = lambda a: pl.BlockSpec(a.shape, lambda b, i: (0,) * a.ndim)
    in_specs = [
        pl.BlockSpec((1, tm, qkv_w), lambda b, i: (b, i, 0)),
        pl.BlockSpec((1, halo, qkv_w), lambda b, i: (b, jnp.maximum(i * hb - 1, 0), 0)),
        pl.BlockSpec((1, halo, qkv_w), lambda b, i: (b, jnp.minimum((i + 1) * hb, seq // halo - 1), 0)),
        pl.BlockSpec((1, tm, LANES), lambda b, i: (b, i, 0)),
        full(conv_w), full(aexp_row), full(dtb_row), full(gm384), full(place),
    ]
    args = [zdn, zdn, zdn, zab, conv_w, aexp_row, dtb_row, gm384, place]
    if rotary:
        in_specs += [pl.BlockSpec((tm, LANES), lambda b, i: (i, 0))] * 2
        args += list(rope_tabs)
    return pl.pallas_call(
        functools.partial(_dn_prep_kernel, tm=tm, n_tiles=n_tiles, rotary=rotary, halo=halo),
        grid=(bsz, n_tiles),
        in_specs=in_specs,
        out_specs=[pl.BlockSpec((1, tm, qkv_w), lambda b, i: (b, i, 0)),
                   pl.BlockSpec((1, tm, LANES), lambda b, i: (b, i, 0))],
        out_shape=[jax.ShapeDtypeStruct((bsz, seq, qkv_w), BF16),
                   jax.ShapeDtypeStruct((bsz, seq, LANES), BF16)],
        compiler_params=_cparams(("parallel", "parallel")),
        name="dn_prep",
    )(*args)


GATE_PIECES = 5
GATE_DIR_STRIDE = 32


def _dn_step(q, k, v, gcx, bx, st, masks):
    cs = DELTA_CHUNK
    causal, strict, eye, last_row, diag, diag_mul = masks
    grp = range(len(q))

    def bd(x):
        return jnp.concatenate([x.astype(BF16)] * 4, axis=0) * diag_mul

    qf = [q[g].astype(F32) for g in grp]
    kf = [k[g].astype(F32) for g in grp]
    vf = [v[g].astype(F32) for g in grp]
    gc_t = [jnp.sum(jnp.where(eye, gcx[g], 0.0), axis=0, keepdims=True) for g in grp]
    dm = [jnp.where(causal, jnp.exp(jnp.where(causal, gcx[g] - gc_t[g], 0.0)), 0.0) for g in grp]
    dms = [jnp.where(strict, dm[g], 0.0) for g in grp]
    gl = [jnp.sum(jnp.where(last_row, gcx[g], 0.0), axis=0, keepdims=True) for g in grp]
    egx = [jnp.exp(gcx[g]) for g in grp]
    kdx = [jnp.exp(gl[g] - gcx[g]) for g in grp]
    kb = [kf[g] * bx[g] for g in grp]
    vb = [vf[g] * bx[g] for g in grp]
    bk = [bd(k[g]) for g in grp]
    lhs = [jnp.concatenate([kb[g].astype(BF16), q[g]], axis=0) for g in grp]
    p1 = [_dot_nt(lhs[g], bk[g]) for g in grp]
    a_mat = [p1[g][:cs] * dms[g] for g in grp]
    attn = [p1[g][cs:] * dm[g] for g in grp]

    eye_f = jnp.where(eye, 1.0, 0.0)
    t_mat = [eye_f - a_mat[g] for g in grp]
    b_pow = [_dot(a_mat[g].astype(BF16), bd(a_mat[g])) for g in grp]
    for _ in range(int(math.log2(cs)) - 2):
        pp = [_dot(jnp.concatenate([t_mat[g], b_pow[g]], axis=0).astype(BF16), bd(b_pow[g])) for g in grp]
        t_mat = [t_mat[g] + pp[g][:cs] for g in grp]
        b_pow = [pp[g][cs:] for g in grp]
    t_mat = [t_mat[g] + _dot(t_mat[g].astype(BF16), bd(b_pow[g])) for g in grp]

    tb = [t_mat[g].astype(BF16) for g in grp]
    u = [_dot(tb[g], bd(vb[g])) for g in grp]
    wk = [_dot(tb[g], bd(kb[g] * egx[g])) for g in grp]
    lhs2 = [jnp.concatenate([wk[g], qf[g] * egx[g]], axis=0).astype(BF16) for g in grp]
    p2 = [_dot(lhs2[g], st[g].astype(BF16)) for g in grp]
    v_new = [u[g] - p2[g][:cs] for g in grp]
    o = [p2[g][cs:] + _dot(attn[g].astype(BF16), bd(v_new[g])) for g in grp]
    kd = [(kf[g] * kdx[g]).astype(BF16) for g in grp]
    st_new = [st[g] * jnp.exp(gl[g]) + jnp.where(diag, _dot_tn(kd[g], v_new[g].astype(BF16)), 0.0) for g in grp]
    return o, st_new


def _dn_masks(reverse):
    cs = DELTA_CHUNK
    wg = 4 * HEAD_DIM
    i3 = lax.broadcasted_iota(jnp.int32, (cs, wg), 0)
    j3 = lax.broadcasted_iota(jnp.int32, (cs, wg), 1) % HEAD_DIM
    if reverse:
        causal, strict, last_row = i3 <= j3, i3 < j3, i3 == 0
    else:
        causal, strict, last_row = i3 >= j3, i3 > j3, i3 == cs - 1
    ra = lax.broadcasted_iota(jnp.int32, (wg, wg), 0) // HEAD_DIM
    ca = lax.broadcasted_iota(jnp.int32, (wg, wg), 1) // HEAD_DIM
    diag = ra == ca
    diag_mul = jnp.where(diag, 1.0, 0.0).astype(BF16)
    return causal, strict, i3 == j3, last_row, diag, diag_mul


def _dn_scan_kernel(*refs, n_chunks, nb, reverse, finish):
    if finish:
        (qkv_ref, gp_ref, e_ref, s0_ref, of_ref, gate_ref, ng_ref, gm_ref, out_ref, s_out_ref, st_scr) = refs
    else:
        (qkv_ref, gp_ref, e_ref, s0_ref, out_ref, s_out_ref, st_scr) = refs
    w = DN_WIDTH
    wg = 4 * HEAD_DIM
    wr = w - wg
    n_grp = nb + nb // 2
    i = pl.program_id(1)

    @pl.when(i == 0)
    def _():
        st_scr[...] = s0_ref[...]

    masks = _dn_masks(reverse)
    e_comb = e_ref[...]
    qkv = [qkv_ref[bb] for bb in range(nb)]
    ex = [_dot(gp_ref[bb], e_comb) for bb in range(nb)]

    def groups(arrs, off):
        out = [a[:, off:off + wg] for a in arrs]
        for p in range(nb // 2):
            out.append(jnp.concatenate([arrs[2 * p][:, off + wg:off + w], arrs[2 * p + 1][:, off + wg:off + w]],
                                       axis=1))
        return out

    o_g, st_new = _dn_step(groups(qkv, 0), groups(qkv, w), groups(qkv, 2 * w), groups(ex, 0), groups(ex, w),
                           [st_scr[g] for g in range(n_grp)], masks)
    for g in range(n_grp):
        st_scr[g] = st_new[g]
    for bb in range(nb):
        rest = o_g[nb + bb // 2][:, (bb % 2) * wr:(bb % 2 + 1) * wr]
        o = jnp.concatenate([o_g[bb], rest], axis=1)
        if finish:
            ot = of_ref[bb] + o
            ms = _dot_x2(ot * ot, gm_ref[...])
            gate = gate_ref[bb].astype(F32)
            out_ref[bb] = (ot * lax.rsqrt(ms + EPS) * ng_ref[...] * _silu(gate)).astype(BF16)
        else:
            out_ref[bb] = o

    @pl.when(i == n_chunks - 1)
    def _():
        s_out_ref[...] = st_scr[...]


def _dn_scan_dir(qkv, gp, e_comb, s0, reverse, finish_args, nb):
    bsz, seq, _ = qkv.shape
    cs = DELTA_CHUNK
    n = seq // cs
    wg = 4 * HEAD_DIM
    n_grp = nb + nb // 2
    finish = finish_args is not None
    cidx = (lambda i: n - 1 - i) if reverse else (lambda i: i)
    full = lambda a: pl.BlockSpec(a.shape, lambda g, i: (0,) * a.ndim)
    tok = lambda wdt, col=0: pl.BlockSpec((nb, cs, wdt), lambda g, i: (g, cidx(i), col))
    st = pl.BlockSpec((None, n_grp, wg, wg), lambda g, i: (g, 0, 0, 0))
    in_specs = [tok(3 * DN_WIDTH), tok(LANES), full(e_comb), st]
    args = [qkv, gp, e_comb, s0]
    if finish:
        o_other, zdn, ng_row, gm_mean = finish_args
        in_specs += [tok(DN_WIDTH), tok(DN_WIDTH, 3), full(ng_row), full(gm_mean)]
        args += [o_other, zdn, ng_row, gm_mean]
    return pl.pallas_call(
        functools.partial(_dn_scan_kernel, n_chunks=n, nb=nb, reverse=reverse, finish=finish),
        grid=(bsz // nb, n),
        in_specs=in_specs,
        out_specs=[tok(DN_WIDTH), st],
        out_shape=[
            jax.ShapeDtypeStruct((bsz, seq, DN_WIDTH), BF16 if finish else F32),
            jax.ShapeDtypeStruct((bsz // nb, n_grp, wg, wg), F32),
        ],
        scratch_shapes=[pltpu.VMEM((n_grp, wg, wg), F32)],
        compiler_params=_cparams(("parallel", "arbitrary")),
        name="dn_scan_bwd" if reverse else "dn_scan_fwd",
    )(*args)


def _dn_scan(qkv, zdn, gp, e_fwd, e_bwd, ng_row, gm_mean, s0, nb):
    o_f, s_f = _dn_scan_dir(qkv, gp, e_fwd, s0[0], False, None, nb)
    y, s_b = _dn_scan_dir(qkv, gp, e_bwd, s0[1], True, (o_f, zdn, ng_row, gm_mean), nb)
    return y, (s_f, s_b)


def _na_kernel(q_ref, k_ref, v_ref, kc_ref, vc_ref, tbl_ref, o_ref, *, rows_per_step, n_rows, win_rows):
    i = pl.program_id(1)
    gw = GRID_W
    lane = lax.broadcasted_iota(jnp.int32, (1, LANES), 1)
    first_head = lane < HEAD_DIM
    scale = HEAD_DIM ** -0.5

    def row_body(rq, carry):
        r = i * rows_per_step + rq
        rs = jnp.clip(r - win_rows // 2, 0, n_rows - win_rows)
        var = r - rs
        k0 = pl.multiple_of(rs * gw, gw)
        q0 = pl.multiple_of(rq * gw, gw)
        pairs = range(NA_WIDTH // LANES)
        heads = [(p, hh) for p in pairs for hh in range(2)]
        ls = [slice(p * LANES, (p + 1) * LANES) for p in pairs]
        zero = jnp.zeros((1, 1), BF16)
        qp = [q_ref[0, pl.ds(q0, gw), ls[p]] * scale for p in pairs]
        kw = [k_ref[0, pl.ds(k0, win_rows * gw), ls[p]] for p in pairs]
        vw = [v_ref[0, pl.ds(k0, win_rows * gw), ls[p]] for p in pairs]
        kc = [kc_ref[0, :, ls[p]] for p in pairs]
        vc = [vc_ref[0, :, ls[p]] for p in pairs]
        qm = [jnp.where(first_head if hh == 0 else jnp.logical_not(first_head), qp[p], zero) for p, hh in heads]
        s_w = [_dot_nt(qm[u], kw[p]) + tbl_ref[2 * p + hh, var] for u, (p, hh) in enumerate(heads)]
        s_c = [_dot_nt(qm[u], kc[p]) for u, (p, hh) in enumerate(heads)]
        m = [jnp.maximum(jnp.max(s_w[u], axis=-1, keepdims=True), jnp.max(s_c[u], axis=-1, keepdims=True))
             for u in range(len(heads))]
        p_w = [jnp.exp(s_w[u] - m[u]) for u in range(len(heads))]
        p_c = [jnp.exp(s_c[u] - m[u]) for u in range(len(heads))]
        den = [jnp.sum(p_w[u], axis=-1, keepdims=True) + jnp.sum(p_c[u], axis=-1, keepdims=True)
               for u in range(len(heads))]
        o = [(_dot(p_w[u].astype(BF16), vw[p]) + _dot(p_c[u].astype(BF16), vc[p])) / den[u]
             for u, (p, hh) in enumerate(heads)]
        for p in pairs:
            o_ref[0, pl.ds(q0, gw), ls[p]] = jnp.where(first_head, o[2 * p], o[2 * p + 1]).astype(BF16)
        return carry

    lax.fori_loop(0, rows_per_step, row_body, 0)


def _na(zna, zna_ctx, tbl, rows_per_step):
    bsz, seq, _ = zna.shape
    n_ctx = zna_ctx.shape[1]
    n_rows = seq // GRID_W
    win_rows = min(WIN_ROWS, n_rows)
    tq = rows_per_step * GRID_W
    return pl.pallas_call(
        functools.partial(_na_kernel, rows_per_step=rows_per_step, n_rows=n_rows, win_rows=win_rows),
        grid=(bsz, n_rows // rows_per_step),
        in_specs=[
            pl.BlockSpec((1, tq, NA_WIDTH), lambda b, i: (b, i, 0)),
            pl.BlockSpec((1, seq, NA_WIDTH), lambda b, i: (b, 0, 1)),
            pl.BlockSpec((1, seq, NA_WIDTH), lambda b, i: (b, 0, 2)),
            pl.BlockSpec((1, n_ctx, NA_WIDTH), lambda b, i: (b, 0, 1)),
            pl.BlockSpec((1, n_ctx, NA_WIDTH), lambda b, i: (b, 0, 2)),
            pl.BlockSpec(tbl.shape, lambda b, i: (0, 0, 0, 0)),
        ],
        out_specs=pl.BlockSpec((1, tq, NA_WIDTH), lambda b, i: (b, i, 0)),
        out_shape=jax.ShapeDtypeStruct((bsz, seq, NA_WIDTH), BF16),
        compiler_params=_cparams(("parallel", "arbitrary")),
        name="nbr_attn",
    )(zna, zna, zna, zna_ctx, zna_ctx, tbl)


def _na_ctx_kernel(q_ref, k_ref, v_ref, o_ref):
    lane = lax.broadcasted_iota(jnp.int32, (1, LANES), 1)
    first_head = lane < HEAD_DIM
    scale = HEAD_DIM ** -0.5
    for p in range(NA_WIDTH // LANES):
        ls = slice(p * LANES, (p + 1) * LANES)
        qp = q_ref[0, :, ls] * scale
        kp = k_ref[0, :, ls]
        vp = v_ref[0, :, ls]
        outs = []
        for hh in range(2):
            msk = first_head if hh == 0 else jnp.logical_not(first_head)
            qm = jnp.where(msk, qp, jnp.zeros_like(qp[:1, :1]))
            s = _dot_nt(qm, kp)
            e = jnp.exp(s - jnp.max(s, axis=-1, keepdims=True))
            den = jnp.sum(e, axis=-1, keepdims=True)
            outs.append(_dot(e.astype(BF16), vp) / den)
        o_ref[0, :, ls] = jnp.where(first_head, outs[0], outs[1]).astype(BF16)


def _na_ctx(zna_ctx):
    bsz, n_ctx, _ = zna_ctx.shape
    col = lambda j: pl.BlockSpec((1, n_ctx, NA_WIDTH), lambda b: (b, 0, j))
    return pl.pallas_call(
        _na_ctx_kernel,
        grid=(bsz,),
        in_specs=[col(0), col(1), col(2)],
        out_specs=pl.BlockSpec((1, n_ctx, NA_WIDTH), lambda b: (b, 0, 0)),
        out_shape=jax.ShapeDtypeStruct((bsz, n_ctx, NA_WIDTH), BF16),
        compiler_params=_cparams(("parallel",)),
        name="ctx_attn",
    )(zna_ctx, zna_ctx, zna_ctx)


def _outproj_kernel(ya_ref, yb_ref, yn_ref, w_ref, x_ref, g_ref, lg_ref, lb_ref, o_ref, *, alpha):
    ycat = jnp.concatenate([ya_ref[0], yb_ref[0], yn_ref[0]], axis=1)
    y = _dot(ycat, w_ref[...])
    s = alpha * x_ref[0] + g_ref[0] * y
    o_ref[0] = _layer_norm_rows(s) * lg_ref[...] + lb_ref[...]


def _out_proj(ya, yb, yn, w_out_b, x, mod3, lg, lb, is_ctx, ctx_row, tm, alpha):
    bsz, seq, d = x.shape
    tok = lambda w: pl.BlockSpec((1, tm, w), lambda b, i: (b, i, 0))
    full = lambda a: pl.BlockSpec(a.shape, lambda b, i: (0,) * a.ndim)
    return pl.pallas_call(
        functools.partial(_outproj_kernel, alpha=alpha),
        grid=(bsz, seq // tm),
        in_specs=[tok(GMLP_WIDTH), tok(DN_WIDTH), tok(NA_WIDTH), full(w_out_b), tok(d),
                  _mod_spec(2, d, is_ctx, ctx_row), full(lg), full(lb)],
        out_specs=tok(d),
        out_shape=jax.ShapeDtypeStruct((bsz, seq, d), F32),
        compiler_params=_cparams(("parallel", "parallel")),
        name="out_proj",
    )(ya, yb, yn, w_out_b, x, mod3, lg, lb)


def _ffn_kernel(x_ref, xp_ref, xn_ref, sh_ref, sc_ref, g_ref, wa_ref, wv_ref, cw_ref, cb_ref, wd_ref,
                lg_ref, lb_ref, o_ref, acc_ref, *, tm, n_tiles, n_chunks, alpha):
    halo = 8
    i = pl.program_id(1)
    has_prev = (i > 0).astype(F32)
    has_next = (i < n_tiles - 1).astype(F32)
    x = x_ref[0]
    sc = 1.0 + sc_ref[0]
    sh = sh_ref[0]
    h_mid = _layer_norm_rows(x) * sc + sh
    h_prev = (_layer_norm_rows(xp_ref[0]) * sc + sh) * has_prev
    h_next = (_layer_norm_rows(xn_ref[0]) * sc + sh) * has_next
    h_ext = jnp.concatenate([h_prev, h_mid, h_next], axis=0).astype(BF16)
    h_b = h_mid.astype(BF16)
    n_ext = tm + 2 * halo
    acc_ref[...] = jnp.zeros_like(acc_ref)

    def chunk_body(j, carry):
        a = _dot(h_ext, wa_ref[j])
        vv = _dot(h_b, wv_ref[j])
        cw = cw_ref[j]
        a_m1 = pltpu.roll(a, 1, 0)[halo:halo + tm]
        a_p1 = pltpu.roll(a, n_ext - 1, 0)[halo:halo + tm]
        conv = a_m1 * cw[0:1] + a[halo:halo + tm] * cw[1:2] + a_p1 * cw[2:3] + cb_ref[j]
        gated = (_silu(conv) * vv).astype(BF16)
        acc_ref[...] += _dot(gated, wd_ref[j])
        return carry

    lax.fori_loop(0, n_chunks, chunk_body, 0, unroll=True)
    s = alpha * x + g_ref[0] * acc_ref[...]
    o_ref[0] = _layer_norm_rows(s) * lg_ref[...] + lb_ref[...]


def _ffn(x, mod3, wa, wv, cw, cb, wd, lg, lb, is_ctx, ctx_row, tm, alpha):
    bsz, seq, d = x.shape
    n_tiles = seq // tm
    n_chunks = wa.shape[0]
    hb = tm // 8
    tok = pl.BlockSpec((1, tm, d), lambda b, i: (b, i, 0))
    full = lambda a: pl.BlockSpec(a.shape, lambda b, i: (0,) * a.ndim)
    return pl.pallas_call(
        functools.partial(_ffn_kernel, tm=tm, n_tiles=n_tiles, n_chunks=n_chunks, alpha=alpha),
        grid=(bsz, n_tiles),
        in_specs=[
            tok,
            pl.BlockSpec((1, 8, d), lambda b, i: (b, jnp.maximum(i * hb - 1, 0), 0)),
            pl.BlockSpec((1, 8, d), lambda b, i: (b, jnp.minimum((i + 1) * hb, seq // 8 - 1), 0)),
            _mod_spec(3, d, is_ctx, ctx_row), _mod_spec(4, d, is_ctx, ctx_row), _mod_spec(5, d, is_ctx, ctx_row),
            full(wa), full(wv), full(cw), full(cb), full(wd), full(lg), full(lb),
        ],
        out_specs=tok,
        out_shape=jax.ShapeDtypeStruct((bsz, seq, d), F32),
        scratch_shapes=[pltpu.VMEM((tm, d), F32)],
        compiler_params=_cparams(("parallel", "parallel")),
        name="conv_glu",
    )(x, x, x, mod3, mod3, mod3, wa, wv, cw, cb, wd, lg, lb)


def _rope_tables(seq):
    half = HEAD_DIM // 2
    nf = half // 2
    pos = jnp.arange(seq)
    inv = ROPE_BASE ** (-jnp.arange(nf, dtype=F32) / nf)
    ang_r = (pos // GRID_W)[:, None].astype(F32) * inv
    ang_c = (pos % GRID_W)[:, None].astype(F32) * inv
    cos_h = jnp.concatenate([jnp.cos(ang_r)] * 2 + [jnp.cos(ang_c)] * 2, axis=1)
    sin_h = jnp.concatenate([-jnp.sin(ang_r), jnp.sin(ang_r), -jnp.sin(ang_c), jnp.sin(ang_c)], axis=1)
    reps = LANES // HEAD_DIM
    return jnp.tile(cos_h, (1, reps)), jnp.tile(sin_h, (1, reps))


def _bias_table(rpb, n_rows):
    wr = min(WIN_ROWS, n_rows)
    nh = rpb.shape[0]
    n_dr = 2 * WIN_ROWS - 1
    col = np.arange(GRID_W)
    cs = np.clip(col - WIN_COLS // 2, 0, GRID_W - WIN_COLS)
    valid = (col[None, :] >= cs[:, None]) & (col[None, :] < cs[:, None] + WIN_COLS)
    pad = GRID_W - WIN_COLS
    padded = jnp.full((nh, n_dr, 2 * GRID_W - 1), NEG_BIG, F32)
    padded = padded.at[:, :, pad:pad + 2 * WIN_COLS - 1].set(rpb.astype(F32))
    t = jnp.stack([padded[:, :, GRID_W - 1 - c:2 * GRID_W - 1 - c] for c in range(GRID_W)], axis=2)
    t = jnp.where(valid[None, None], t, NEG_BIG)
    per_var = []
    for var in range(wr):
        lo = WIN_ROWS - 1 - var
        per_var.append(t[:, lo:lo + wr].transpose(0, 2, 1, 3).reshape(nh, GRID_W, wr * GRID_W))
    return jnp.stack(per_var, axis=1)


def _gate_mats():
    place = np.zeros((GATE_PIECES, LANES, LANES), np.float32)
    expand = np.zeros((2, LANES, 2 * DN_WIDTH), np.float32)
    for dd in range(2):
        for h in range(DN_HEADS):
            cols = slice(h * HEAD_DIM, (h + 1) * HEAD_DIM)
            for kk in range(3):
                dst = dd * GATE_DIR_STRIDE + kk * DN_HEADS + h
                place[kk, dd * DN_HEADS + h, dst] = 1.0
                expand[dd, dst, cols] = 1.0
            for kk in range(2):
                dst = dd * GATE_DIR_STRIDE + (3 + kk) * DN_HEADS + h
                place[3 + kk, 2 * DN_HEADS + dd * DN_HEADS + h, dst] = 1.0
                expand[dd, dst, DN_WIDTH + h * HEAD_DIM:DN_WIDTH + (h + 1) * HEAD_DIM] = 1.0
    return jnp.asarray(place, BF16), jnp.asarray(expand[0], BF16), jnp.asarray(expand[1], BF16)


def _pad_lanes(v):
    return jnp.zeros((1, LANES), F32).at[0, :v.shape[0]].set(v.astype(F32))


def _token_tile(seq):
    return 512 if seq % 512 == 0 else 256


def kernel(x, c, ctx, c_ctx, w_ada, b_ada, w_in, gmlp_ln_g, gmlp_ln_b, gmlp_ws, gmlp_bs, dn_conv, dn_a_log,
           dn_dt_bias, dn_norm_g, na_rpb, w_out, ln1_g, ln1_b, ffn_up, ffn_conv, ffn_conv_b, ffn_down,
           ln2_g, ln2_b):
    depth = w_ada.shape[0]
    bsz, seq, d = x.shape
    n_ctx = ctx.shape[1]
    d_ff = ffn_down.shape[1]
    alpha = (2 * depth) ** 0.25
    ctx_row = bsz
    ffn_cw = 256
    n_ffn_chunks = d_ff // ffn_cw
    tm = _token_tile(seq)
    tm_c = _token_tile(n_ctx)

    mod = _mod_all(c, c_ctx, w_ada, b_ada)
    rope_tabs = _rope_tables(seq)
    place, e_fwd, e_bwd = _gate_mats()
    gm256_mean = jnp.asarray(_block_diag_np(GMLP_HEADS, HEAD_DIM, HEAD_DIM, 1.0 / HEAD_DIM), BF16)
    gm384_sum = jnp.asarray(_block_diag_np(DN_HEADS, HEAD_DIM, HEAD_DIM, 1.0), BF16)
    gm384_mean = jnp.asarray(_block_diag_np(DN_HEADS, HEAD_DIM, HEAD_DIM, 1.0 / HEAD_DIM), BF16)
    assert bsz % 2 == 0, "the DeltaNet scan pairs batch elements"
    nb = 4 if bsz % 4 == 0 else 2
    s_zero = jnp.zeros((bsz // nb, nb + nb // 2, 4 * HEAD_DIM, 4 * HEAD_DIM), F32)
    o1 = GMLP_COLS
    o2 = o1 + DN_MAIN_COLS
    o3 = o2 + DN_AB_COLS

    for l in range(depth):
        ctx_out = l < depth - 1
        mod3 = mod[l].reshape(8, 1, 6 * d)
        wl = w_in[l]
        w_in_p = jnp.concatenate(
            [wl[:, :o2], wl[:, o3:], wl[:, o2:o3], jnp.zeros((d, LANES - DN_AB_COLS), F32)], axis=1).astype(BF16)
        lng = gmlp_ln_g[l].reshape(1, GMLP_WIDTH)
        lnb = gmlp_ln_b[l].reshape(1, GMLP_WIDTH)
        wsp = gmlp_ws[l].transpose(1, 0, 2).reshape(GMLP_CHUNK, GMLP_HEADS * GMLP_CHUNK).astype(BF16)
        bsx = jnp.repeat(gmlp_bs[l].T, HEAD_DIM, axis=1)
        aexp_row = _pad_lanes(jnp.exp(dn_a_log[l].astype(F32)).reshape(-1))
        dtb_row = _pad_lanes(dn_dt_bias[l].reshape(-1))
        ng_row = jnp.tile(dn_norm_g[l].astype(F32), DN_HEADS).reshape(1, DN_WIDTH)
        tbl = _bias_table(na_rpb[l], seq // GRID_W)
        w_out_b = w_out[l].astype(BF16)
        lg1 = ln1_g[l].reshape(1, d)
        lb1 = ln1_b[l].reshape(1, d)
        lg2 = ln2_g[l].reshape(1, d)
        lb2 = ln2_b[l].reshape(1, d)
        up = ffn_up[l]
        wa = up[:, :d_ff].reshape(d, n_ffn_chunks, ffn_cw).transpose(1, 0, 2).astype(BF16)
        wv = up[:, d_ff:].reshape(d, n_ffn_chunks, ffn_cw).transpose(1, 0, 2).astype(BF16)
        wd = ffn_down[l].reshape(n_ffn_chunks, ffn_cw, d).astype(BF16)
        cw = ffn_conv[l].reshape(3, n_ffn_chunks, ffn_cw).transpose(1, 0, 2)
        cb = ffn_conv_b[l].reshape(n_ffn_chunks, 1, ffn_cw)

        zg_c, zdn_c, zna_c, zab_c = _in_proj(ctx, mod3, w_in_p, True, ctx_row, tm_c)
        qkv_c, gp_c = _dn_prep(zdn_c, zab_c, dn_conv[l], aexp_row, dtb_row, gm384_sum, place, None, tm_c)
        yb_c, s_ctx = _dn_scan(qkv_c, zdn_c, gp_c, e_fwd, e_bwd, ng_row, gm384_mean, (s_zero, s_zero), nb)

        zg, zdn, zna, zab = _in_proj(x, mod3, w_in_p, False, ctx_row, tm)
        ya = _gmlp(zg, gm256_mean, lng, lnb, wsp, bsx, tm)
        qkv, gp = _dn_prep(zdn, zab, dn_conv[l], aexp_row, dtb_row, gm384_sum, place, rope_tabs, tm)
        yb, _ = _dn_scan(qkv, zdn, gp, e_fwd, e_bwd, ng_row, gm384_mean, s_ctx, nb)
        yn = _na(zna, zna_c, tbl, 8)
        x1 = _out_proj(ya, yb, yn, w_out_b, x, mod3, lg1, lb1, False, ctx_row, tm, alpha)
        x_next = _ffn(x1, mod3, wa, wv, cw, cb, wd, lg2, lb2, False, ctx_row, tm, alpha)

        if ctx_out:
            ya_c = _gmlp(zg_c, gm256_mean, lng, lnb, wsp, bsx, tm_c)
            yn_c = _na_ctx(zna_c)
            c1 = _out_proj(ya_c, yb_c, yn_c, w_out_b, ctx, mod3, lg1, lb1, True, ctx_row, tm_c, alpha)
            ctx = _ffn(c1, mod3, wa, wv, cw, cb, wd, lg2, lb2, True, ctx_row, tm_c, alpha)
        x = x_next
    return x
```

```python
import functools
import math

import numpy as np
import jax
import jax.numpy as jnp
from jax import lax
from jax.experimental import pallas as pl
from jax.experimental.pallas import tpu as pltpu

F32 = jnp.float32
BF16 = jnp.bfloat16

HEAD_DIM = 64
GRID_W = 64
GMLP_HEADS = 4
GMLP_WIDTH = GMLP_HEADS * HEAD_DIM
GMLP_CHUNK = 128
DN_HEADS = 6
DN_WIDTH = DN_HEADS * HEAD_DIM
NA_HEADS = 6
NA_WIDTH = NA_HEADS * HEAD_DIM
DELTA_CHUNK = 64
WIN_ROWS = 8
WIN_COLS = 16
ROPE_BASE = 10000.0
EPS = 1e-6
NEG_BIG = -1e30
LANES = 128
VMEM_LIMIT = 56 * 1024 * 1024

GMLP_COLS = 2 * GMLP_WIDTH
DN_MAIN_COLS = 4 * DN_WIDTH
DN_AB_COLS = 4 * DN_HEADS
NA_COLS = 3 * NA_WIDTH


def _cparams(sem):
    return pltpu.CompilerParams(dimension_semantics=sem, vmem_limit_bytes=VMEM_LIMIT)


def _split2(x):
    hi = x.astype(BF16)
    lo = (x - hi.astype(F32)).astype(BF16)
    return hi, lo


def _split3(x):
    a = x.astype(BF16)
    r = x - a.astype(F32)
    b = r.astype(BF16)
    c = (r - b.astype(F32)).astype(BF16)
    return a, b, c


def _dot(a, b):
    return jnp.dot(a, b, preferred_element_type=F32)


def _dot_nt(a, b):
    return lax.dot_general(a, b, (((1,), (1,)), ((), ())), preferred_element_type=F32)


def _dot_tn(a, b):
    return lax.dot_general(a, b, (((0,), (0,)), ((), ())), preferred_element_type=F32)


def _dot_x3(x, m):
    a, b, c = _split3(x)
    return _dot(a, m) + _dot(b, m) + _dot(c, m)


def _dot_x2(x, m):
    a, b = _split2(x)
    return _dot(a, m) + _dot(b, m)


def _silu(x):
    return x * jax.nn.sigmoid(x)


def _layer_norm_rows(x):
    mu = jnp.mean(x, axis=-1, keepdims=True)
    xc = x - mu
    var = jnp.mean(xc * xc, axis=-1, keepdims=True)
    return xc * lax.rsqrt(var + EPS)


def _block_diag_np(n_blocks, rows, cols, value):
    m = np.zeros((n_blocks * rows, n_blocks * cols), np.float32)
    for g in range(n_blocks):
        m[g * rows:(g + 1) * rows, g * cols:(g + 1) * cols] = value
    return m


def _mod_kernel(c_ref, w_ref, b_ref, o_ref):
    a = _silu(c_ref[...])
    w = w_ref[0]
    a1, a2 = _split2(a)
    w1, w2 = _split2(w)
    o_ref[0] = _dot(a1, w1) + _dot(a1, w2) + _dot(a2, w1) + b_ref[0]


def _mod_all(c, c_ctx, w_ada, b_ada):
    n_layers, d, n6 = w_ada.shape
    bsz = c.shape[0]
    rows = jnp.concatenate([c, c_ctx[None, :], jnp.zeros((8 - bsz - 1, d), F32)], axis=0)
    tn = 1536
    return pl.pallas_call(
        _mod_kernel,
        grid=(n_layers, n6 // tn),
        in_specs=[
            pl.BlockSpec((8, d), lambda l, j: (0, 0)),
            pl.BlockSpec((1, d, tn), lambda l, j: (l, 0, j)),
            pl.BlockSpec((1, 1, tn), lambda l, j: (l, 0, j)),
        ],
        out_specs=pl.BlockSpec((1, 8, tn), lambda l, j: (l, 0, j)),
        out_shape=jax.ShapeDtypeStruct((n_layers, 8, n6), F32),
        compiler_params=_cparams(("parallel", "parallel")),
        name="adaln_mod",
    )(rows, w_ada, b_ada.reshape(n_layers, 1, n6))


def _mod_spec(chunk, d, is_ctx, ctx_row):
    if is_ctx:
        return pl.BlockSpec((1, 1, d), lambda b, i: (ctx_row, 0, chunk))
    return pl.BlockSpec((1, 1, d), lambda b, i: (b, 0, chunk))


def _inproj_kernel(x_ref, sh_ref, sc_ref, w_ref, zg_ref, zdn_ref, zna_ref, zab_ref):
    h = _layer_norm_rows(x_ref[0]) * (1.0 + sc_ref[0]) + sh_ref[0]
    h = h.astype(BF16)
    o1 = GMLP_COLS
    o2 = o1 + DN_MAIN_COLS
    o3 = o2 + NA_COLS
    zg_ref[0] = _dot(h, w_ref[:, 0:o1]).astype(BF16)
    zdn_ref[0] = _dot(h, w_ref[:, o1:o2]).astype(BF16)
    zna_ref[0] = _dot(h, w_ref[:, o2:o3]).astype(BF16)
    zab_ref[0] = _dot(h, w_ref[:, o3:o3 + LANES])


def _in_proj(x, mod3, w_in_p, is_ctx, ctx_row, tm):
    bsz, seq, d = x.shape
    ncols = w_in_p.shape[1]
    tok = lambda w: pl.BlockSpec((1, tm, w), lambda b, i: (b, i, 0))
    return pl.pallas_call(
        _inproj_kernel,
        grid=(bsz, seq // tm),
        in_specs=[
            tok(d),
            _mod_spec(0, d, is_ctx, ctx_row),
            _mod_spec(1, d, is_ctx, ctx_row),
            pl.BlockSpec((d, ncols), lambda b, i: (0, 0)),
        ],
        out_specs=[tok(GMLP_COLS), tok(DN_MAIN_COLS), tok(NA_COLS), tok(LANES)],
        out_shape=[
            jax.ShapeDtypeStruct((bsz, seq, GMLP_COLS), BF16),
            jax.ShapeDtypeStruct((bsz, seq, DN_MAIN_COLS), BF16),
            jax.ShapeDtypeStruct((bsz, seq, NA_COLS), BF16),
            jax.ShapeDtypeStruct((bsz, seq, LANES), F32),
        ],
        compiler_params=_cparams(("parallel", "parallel")),
        name="in_proj",
    )(x, mod3, mod3, w_in_p)


def _gmlp_kernel(z_ref, g_ref, lng_ref, lnb_ref, wsp_ref, bsx_ref, o_ref, *, tm):
    z = jax.nn.gelu(z_ref[0].astype(F32))
    u = z[:, :GMLP_WIDTH]
    v = z[:, GMLP_WIDTH:]
    gmat = g_ref[...]
    mu = _dot_x2(v, gmat)
    vc = v - mu
    var = _dot_x2(vc * vc, gmat)
    vn = vc * lax.rsqrt(var + EPS) * lng_ref[...] + lnb_ref[...]
    vn = vn.astype(BF16)
    nrow = GMLP_HEADS * GMLP_CHUNK
    ri = lax.broadcasted_iota(jnp.int32, (nrow, GMLP_WIDTH), 0) // GMLP_CHUNK
    ci = lax.broadcasted_iota(jnp.int32, (nrow, GMLP_WIDTH), 1) // HEAD_DIM
    diag = ri == ci
    wsp = wsp_ref[...]
    bsx = bsx_ref[...]
    for n in range(tm // GMLP_CHUNK):
        rows = slice(n * GMLP_CHUNK, (n + 1) * GMLP_CHUNK)
        vch = vn[rows]
        bd = jnp.where(diag, jnp.concatenate([vch] * GMLP_HEADS, axis=0), jnp.zeros_like(vch[:1, :1]))
        mixed = _dot(wsp, bd) + bsx
        o_ref[0, rows, :] = (u[rows] * mixed).astype(BF16)


def _gmlp(zg, gm256, lng, lnb, wsp, bsx, tm):
    bsz, seq, _ = zg.shape
    full = lambda a: pl.BlockSpec(a.shape, lambda b, i: (0,) * a.ndim)
    return pl.pallas_call(
        functools.partial(_gmlp_kernel, tm=tm),
        grid=(bsz, seq // tm),
        in_specs=[pl.BlockSpec((1, tm, GMLP_COLS), lambda b, i: (b, i, 0)),
                  full(gm256), full(lng), full(lnb), full(wsp), full(bsx)],
        out_specs=pl.BlockSpec((1, tm, GMLP_WIDTH), lambda b, i: (b, i, 0)),
        out_shape=jax.ShapeDtypeStruct((bsz, seq, GMLP_WIDTH), BF16),
        compiler_params=_cparams(("parallel", "parallel")),
        name="gmlp_mix",
    )(zg, gm256, lng, lnb, wsp, bsx)


def _dn_prep_kernel(*refs, tm, n_tiles, rotary, halo):
    if rotary:
        (z_ref, zp_ref, zn_ref, ab_ref, cw_ref, aexp_ref, dtb_ref, g_ref, pk_ref, cos_ref, sin_ref,
         qkv_ref, gp_ref) = refs
    else:
        (z_ref, zp_ref, zn_ref, ab_ref, cw_ref, aexp_ref, dtb_ref, g_ref, pk_ref, qkv_ref, gp_ref) = refs
    i = pl.program_id(1)
    has_prev = (i > 0).astype(F32)
    has_next = (i < n_tiles - 1).astype(F32)
    x_ext = jnp.concatenate([zp_ref[0].astype(F32) * has_prev, z_ref[0].astype(F32),
                             zn_ref[0].astype(F32) * has_next], axis=0)
    n_ext = tm + 2 * halo
    x_m1 = pltpu.roll(x_ext, 1, 0)[halo:halo + tm]
    x_p1 = pltpu.roll(x_ext, n_ext - 1, 0)[halo:halo + tm]
    x_0 = x_ext[halo:halo + tm]
    cw = cw_ref[...]
    y = _silu(x_m1 * cw[0:1] + x_0 * cw[1:2] + x_p1 * cw[2:3])
    q = y[:, 0:DN_WIDTH]
    k = y[:, DN_WIDTH:2 * DN_WIDTH]
    v = y[:, 2 * DN_WIDTH:3 * DN_WIDTH]
    gmat = g_ref[...]
    q = q * lax.rsqrt(_dot_x2(q * q, gmat) + EPS)
    k = k * lax.rsqrt(_dot_x2(k * k, gmat) + EPS)
    if rotary:
        cos2 = cos_ref[...]
        sin2 = sin_ref[...]
        lane = lax.broadcasted_iota(jnp.int32, (tm, LANES), 1)
        first = (lane % 32) < 16

        def rope(t):
            parts = []
            for p in range(DN_WIDTH // LANES):
                tp = t[:, p * LANES:(p + 1) * LANES]
                partner = jnp.where(first, pltpu.roll(tp, LANES - 16, 1), pltpu.roll(tp, 16, 1))
                parts.append(tp * cos2 + partner * sin2)
            return jnp.concatenate(parts, axis=1)

        q = rope(q)
        k = rope(k)
    q = q * (HEAD_DIM ** -0.5)
    qkv_ref[0, :, 0:DN_WIDTH] = q.astype(BF16)
    qkv_ref[0, :, DN_WIDTH:2 * DN_WIDTH] = k.astype(BF16)
    qkv_ref[0, :, 2 * DN_WIDTH:3 * DN_WIDTH] = v.astype(BF16)
    ab = ab_ref[0]
    lane_ab = lax.broadcasted_iota(jnp.int32, ab.shape, 1)
    xs = ab + dtb_ref[...]
    softplus = jnp.maximum(xs, 0.0) + jnp.log1p(jnp.exp(-jnp.abs(xs)))
    log_decay = -aexp_ref[...] * softplus
    beta = jax.nn.sigmoid(ab)
    ti = lax.broadcasted_iota(jnp.int32, (tm, tm), 0)
    tj = lax.broadcasted_iota(jnp.int32, (tm, tm), 1)
    same_chunk = (ti // DELTA_CHUNK) == (tj // DELTA_CHUNK)
    l_fwd = jnp.where(same_chunk, jnp.where(ti >= tj, 1.0, 0.0), 0.0).astype(BF16)
    l_bwd = jnp.where(same_chunk, jnp.where(ti <= tj, 1.0, 0.0), 0.0).astype(BF16)
    g1, g2, g3 = _split3(log_decay)
    gc_f = _dot(l_fwd, g1) + _dot(l_fwd, g2) + _dot(l_fwd, g3)
    gc_b = _dot(l_bwd, g1) + _dot(l_bwd, g2) + _dot(l_bwd, g3)
    gc = jnp.where(lane_ab < DN_HEADS, gc_f, gc_b)
    c1, c2, c3 = _split3(gc)
    b1, b2 = _split2(beta)
    packed = (_dot(c1, pk_ref[0]) + _dot(c2, pk_ref[1]) + _dot(c3, pk_ref[2])
              + _dot(b1, pk_ref[3]) + _dot(b2, pk_ref[4]))
    gp_ref[0] = packed.astype(BF16)


def _dn_prep(zdn, zab, conv_w, aexp_row, dtb_row, gm384, place, rope_tabs, tm):
    bsz, seq, _ = zdn.shape
    halo = 16
    n_tiles = seq // tm
    hb = tm // halo
    qkv_w = 3 * DN_WIDTH
    rotary = rope_tabs is not None
    full = lambda a: pl.BlockSpec(a.shape, lambda b, i: (0,) * a.ndim)
    in_specs = [
        pl.BlockSpec((1, tm, qkv_w), lambda b, i: (b, i, 0)),
        pl.BlockSpec((1, halo, qkv_w), lambda b, i: (b, jnp.maximum(i * hb - 1, 0), 0)),
        pl.BlockSpec((1, halo, qkv_w), lambda b, i: (b, jnp.minimum((i + 1) * hb, seq // halo - 1), 0)),
        pl.BlockSpec((1, tm, LANES), lambda b, i: (b, i, 0)),
        full(conv_w), full(aexp_row), full(dtb_row), full(gm384), full(place),
    ]
    args = [zdn, zdn, zdn, zab, conv_w, aexp_row, dtb_row, gm384, place]
    if rotary:
        in_specs += [pl.BlockSpec((tm, LANES), lambda b, i: (i, 0))] * 2
        args += list(rope_tabs)
    return pl.pallas_call(
        functools.partial(_dn_prep_kernel, tm=tm, n_tiles=n_tiles, rotary=rotary, halo=halo),
        grid=(bsz, n_tiles),
        in_specs=in_specs,
        out_specs=[pl.BlockSpec((1, tm, qkv_w), lambda b, i: (b, i, 0)),
                   pl.BlockSpec((1, tm, LANES), lambda b, i: (b, i, 0))],
        out_shape=[jax.ShapeDtypeStruct((bsz, seq, qkv_w), BF16),
                   jax.ShapeDtypeStruct((bsz, seq, LANES), BF16)],
        compiler_params=_cparams(("parallel", "parallel")),
        name="dn_prep",
    )(*args)


GATE_PIECES = 5
GATE_DIR_STRIDE = 32
INV_BLOCK = 8
DN_CHUNKS_PER_STEP = 4


def _dn_intra(q, k, v, gcx, bx, masks):
    cs = DELTA_CHUNK
    causal, strict, eye, last_row, diag, diag_mul, inv_diag, inv_levels = masks
    grp = range(len(q))

    def bd(x):
        return jnp.concatenate([x.astype(BF16)] * 4, axis=0) * diag_mul

    qf = [q[g].astype(F32) for g in grp]
    kf = [k[g].astype(F32) for g in grp]
    vf = [v[g].astype(F32) for g in grp]
    gc_t = [jnp.sum(jnp.where(eye, gcx[g], 0.0), axis=0, keepdims=True) for g in grp]
    dm = [jnp.where(causal, jnp.exp(jnp.where(causal, gcx[g] - gc_t[g], 0.0)), 0.0) for g in grp]
    dms = [jnp.where(strict, dm[g], 0.0) for g in grp]
    gl = [jnp.sum(jnp.where(last_row, gcx[g], 0.0), axis=0, keepdims=True) for g in grp]
    egx = [jnp.exp(gcx[g]) for g in grp]
    kdx = [jnp.exp(gl[g] - gcx[g]) for g in grp]
    kb = [kf[g] * bx[g] for g in grp]
    vb = [vf[g] * bx[g] for g in grp]
    bk = [bd(k[g]) for g in grp]
    lhs = [jnp.concatenate([kb[g].astype(BF16), q[g]], axis=0) for g in grp]
    p1 = [_dot_nt(lhs[g], bk[g]) for g in grp]
    a_mat = [p1[g][:cs] * dms[g] for g in grp]
    attn = [p1[g][cs:] * dm[g] for g in grp]

    eye_f = jnp.where(eye, 1.0, 0.0)
    a_d = [a_mat[g] * inv_diag for g in grp]
    t_mat = [eye_f - a_d[g] for g in grp]
    b_pow = [_dot(a_d[g].astype(BF16), bd(a_d[g])) for g in grp]
    pp = [_dot(jnp.concatenate([t_mat[g], b_pow[g]], axis=0).astype(BF16), bd(b_pow[g])) for g in grp]
    t_mat = [t_mat[g] + pp[g][:cs] for g in grp]
    t_mat = [t_mat[g] + _dot(t_mat[g].astype(BF16), bd(pp[g][cs:])) for g in grp]
    for coupling in inv_levels:
        x = [_dot(t_mat[g].astype(BF16), bd(a_mat[g] * coupling)) for g in grp]
        t_mat = [t_mat[g] - _dot(x[g].astype(BF16), bd(t_mat[g])) for g in grp]

    tb = [t_mat[g].astype(BF16) for g in grp]
    u = [_dot(tb[g], bd(vb[g])) for g in grp]
    wk = [_dot(tb[g], bd(kb[g] * egx[g])) for g in grp]
    lhs2 = [jnp.concatenate([wk[g], qf[g] * egx[g]], axis=0).astype(BF16) for g in grp]
    kd = [(kf[g] * kdx[g]).astype(BF16) for g in grp]
    egl = [jnp.exp(gl[g]) for g in grp]
    attn_b = [attn[g].astype(BF16) for g in grp]
    return u, lhs2, attn_b, kd, egl


def _dn_inter(intra, st, masks):
    cs = DELTA_CHUNK
    diag, diag_mul = masks[4], masks[5]
    u, lhs2, attn_b, kd, egl = intra
    grp = range(len(u))
    p2 = [_dot(lhs2[g], st[g].astype(BF16)) for g in grp]
    v_new = [u[g] - p2[g][:cs] for g in grp]
    vnb = [v_new[g].astype(BF16) for g in grp]
    bdv = [jnp.concatenate([vnb[g]] * 4, axis=0) * diag_mul for g in grp]
    o = [p2[g][cs:] + _dot(attn_b[g], bdv[g]) for g in grp]
    st_new = [st[g] * egl[g] + jnp.where(diag, _dot_tn(kd[g], vnb[g]), 0.0) for g in grp]
    return o, st_new


def _dn_masks(reverse):
    cs = DELTA_CHUNK
    wg = 4 * HEAD_DIM
    i3 = lax.broadcasted_iota(jnp.int32, (cs, wg), 0)
    j3 = lax.broadcasted_iota(jnp.int32, (cs, wg), 1) % HEAD_DIM
    if reverse:
        causal, strict, last_row = i3 <= j3, i3 < j3, i3 == 0
    else:
        causal, strict, last_row = i3 >= j3, i3 > j3, i3 == cs - 1
    ra = lax.broadcasted_iota(jnp.int32, (wg, wg), 0) // HEAD_DIM
    ca = lax.broadcasted_iota(jnp.int32, (wg, wg), 1) // HEAD_DIM
    diag = ra == ca
    diag_mul = jnp.where(diag, 1.0, 0.0).astype(BF16)
    inv_diag = jnp.where(i3 // INV_BLOCK == j3 // INV_BLOCK, 1.0, 0.0)
    inv_levels = []
    size = INV_BLOCK
    while size < cs:
        same_pair = i3 // (2 * size) == j3 // (2 * size)
        inv_levels.append(jnp.where(same_pair, jnp.where(i3 // size == j3 // size, 0.0, 1.0), 0.0))
        size *= 2
    return causal, strict, i3 == j3, last_row, diag, diag_mul, inv_diag, inv_levels


def _dn_scan_kernel(*refs, n_steps, nb, cps, reverse, finish):
    if finish:
        (qkv_ref, gp_ref, e_ref, s0_ref, of_ref, gate_ref, ng_ref, gm_ref, out_ref, s_out_ref, st_scr) = refs
    else:
        (qkv_ref, gp_ref, e_ref, s0_ref, out_ref, s_out_ref, st_scr) = refs
    w = DN_WIDTH
    wg = 4 * HEAD_DIM
    wr = w - wg
    n_grp = nb + nb // 2
    i = pl.program_id(1)

    @pl.when(i == 0)
    def _():
        st_scr[...] = s0_ref[...]

    masks = _dn_masks(reverse)
    e_comb = e_ref[...]
    cs = DELTA_CHUNK
    order = list(range(cps - 1, -1, -1)) if reverse else list(range(cps))

    def groups(arrs, off):
        out = [a[:, off:off + wg] for a in arrs]
        for p in range(nb // 2):
            out.append(jnp.concatenate([arrs[2 * p][:, off + wg:off + w], arrs[2 * p + 1][:, off + wg:off + w]],
                                       axis=1))
        return out

    qs, ks, vs, gs, bs = [], [], [], [], []
    for c in order:
        rows = slice(c * cs, (c + 1) * cs)
        qkv = [qkv_ref[bb, rows, :] for bb in range(nb)]
        ex = [_dot(gp_ref[bb, rows, :], e_comb) for bb in range(nb)]
        qs += groups(qkv, 0)
        ks += groups(qkv, w)
        vs += groups(qkv, 2 * w)
        gs += groups(ex, 0)
        bs += groups(ex, w)
    intra = _dn_intra(qs, ks, vs, gs, bs, masks)
    st = [st_scr[g] for g in range(n_grp)]
    for idx, c in enumerate(order):
        rows = slice(c * cs, (c + 1) * cs)
        part = [t[idx * n_grp:(idx + 1) * n_grp] for t in intra]
        o_g, st = _dn_inter(part, st, masks)
        for bb in range(nb):
            rest = o_g[nb + bb // 2][:, (bb % 2) * wr:(bb % 2 + 1) * wr]
            o = jnp.concatenate([o_g[bb], rest], axis=1)
            if finish:
                ot = of_ref[bb, rows, :] + o
                ms = _dot_x2(ot * ot, gm_ref[...])
                gate = gate_ref[bb, rows, :].astype(F32)
                out_ref[bb, rows, :] = (ot * lax.rsqrt(ms + EPS) * ng_ref[...] * _silu(gate)).astype(BF16)
            else:
                out_ref[bb, rows, :] = o
    for g in range(n_grp):
        st_scr[g] = st[g]

    @pl.when(i == n_steps - 1)
    def _():
        s_out_ref[...] = st_scr[...]


def _dn_scan_dir(qkv, gp, e_comb, s0, reverse, finish_args, nb):
    bsz, seq, _ = qkv.shape
    cps = DN_CHUNKS_PER_STEP
    cs = cps * DELTA_CHUNK
    n = seq // cs
    wg = 4 * HEAD_DIM
    n_grp = nb + nb // 2
    finish = finish_args is not None
    cidx = (lambda i: n - 1 - i) if reverse else (lambda i: i)
    full = lambda a: pl.BlockSpec(a.shape, lambda g, i: (0,) * a.ndim)
    tok = lambda wdt, col=0: pl.BlockSpec((nb, cs, wdt), lambda g, i: (g, cidx(i), col))
    st = pl.BlockSpec((None, n_grp, wg, wg), lambda g, i: (g, 0, 0, 0))
    in_specs = [tok(3 * DN_WIDTH), tok(LANES), full(e_comb), st]
    args = [qkv, gp, e_comb, s0]
    if finish:
        o_other, zdn, ng_row, gm_mean = finish_args
        in_specs += [tok(DN_WIDTH), tok(DN_WIDTH, 3), full(ng_row), full(gm_mean)]
        args += [o_other, zdn, ng_row, gm_mean]
    return pl.pallas_call(
        functools.partial(_dn_scan_kernel, n_steps=n, nb=nb, cps=cps, reverse=reverse, finish=finish),
        grid=(bsz // nb, n),
        in_specs=in_specs,
        out_specs=[tok(DN_WIDTH), st],
        out_shape=[
            jax.ShapeDtypeStruct((bsz, seq, DN_WIDTH), BF16 if finish else F32),
            jax.ShapeDtypeStruct((bsz // nb, n_grp, wg, wg), F32),
        ],
        scratch_shapes=[pltpu.VMEM((n_grp, wg, wg), F32)],
        compiler_params=_cparams(("parallel", "arbitrary")),
        name="dn_scan_bwd" if reverse else "dn_scan_fwd",
    )(*args)


def _dn_scan(qkv, zdn, gp, e_fwd, e_bwd, ng_row, gm_mean, s0, nb):
    o_f, s_f = _dn_scan_dir(qkv, gp, e_fwd, s0[0], False, None, nb)
    y, s_b = _dn_scan_dir(qkv, gp, e_bwd, s0[1], True, (o_f, zdn, ng_row, gm_mean), nb)
    return y, (s_f, s_b)


NA_ROWS_PER_ITER = 2


def _na_kernel(q_ref, k_ref, v_ref, kc_ref, vc_ref, tbl_ref, o_ref, sc_scr, pc_scr, ow_scr, *,
               rows_per_step, n_rows, win_rows):
    i = pl.program_id(1)
    gw = GRID_W
    lane = lax.broadcasted_iota(jnp.int32, (1, LANES), 1)
    first_head = lane < HEAD_DIM
    scale = HEAD_DIM ** -0.5
    pairs = range(NA_WIDTH // LANES)
    heads = [(p, hh) for p in pairs for hh in range(2)]
    ls = [slice(p * LANES, (p + 1) * LANES) for p in pairs]
    zero = jnp.zeros((1, 1), BF16)

    def one_head(x, hh):
        return jnp.where(first_head if hh == 0 else jnp.logical_not(first_head), x, zero)

    kc = [kc_ref[0, :, ls[p]] for p in pairs]
    q_all = [q_ref[0, :, ls[p]] * scale for p in pairs]
    for u, (p, hh) in enumerate(heads):
        sc_scr[u] = _dot_nt(one_head(q_all[p], hh), kc[p])

    def row_body(it, carry):
        units = []
        for rr in range(NA_ROWS_PER_ITER):
            rq = it * NA_ROWS_PER_ITER + rr
            r = i * rows_per_step + rq
            rs = jnp.clip(r - win_rows // 2, 0, n_rows - win_rows)
            k0 = pl.multiple_of(rs * gw, gw)
            q0 = pl.multiple_of(rq * gw, gw)
            for p in pairs:
                qp = q_ref[0, pl.ds(q0, gw), ls[p]] * scale
                kw = k_ref[0, pl.ds(k0, win_rows * gw), ls[p]]
                vw = v_ref[0, pl.ds(k0, win_rows * gw), ls[p]]
                for hh in range(2):
                    units.append((q0, r - rs, p, hh, qp, kw, vw))
        n_u = range(len(units))
        s_w = [_dot_nt(one_head(qp, hh), kw) + tbl_ref[2 * p + hh, var] for q0, var, p, hh, qp, kw, vw in units]
        s_c = [sc_scr[2 * p + hh, pl.ds(q0, gw), :] for q0, var, p, hh, qp, kw, vw in units]
        m = [jnp.maximum(jnp.max(s_w[u], axis=-1, keepdims=True), jnp.max(s_c[u], axis=-1, keepdims=True))
             for u in n_u]
        p_w = [jnp.exp(s_w[u] - m[u]) for u in n_u]
        p_c = [jnp.exp(s_c[u] - m[u]) for u in n_u]
        inv = [1.0 / (jnp.sum(p_w[u], axis=-1, keepdims=True) + jnp.sum(p_c[u], axis=-1, keepdims=True))
               for u in n_u]
        o_w = [_dot(p_w[u].astype(BF16), units[u][6]) * inv[u] for u in n_u]
        for u in n_u:
            q0, var, p, hh = units[u][:4]
            pc_scr[2 * p + hh, pl.ds(q0, gw), :] = (p_c[u] * inv[u]).astype(BF16)
            if hh == 1:
                ow_scr[p, pl.ds(q0, gw), :] = jnp.where(first_head, o_w[u - 1], o_w[u])
        return carry

    lax.fori_loop(0, rows_per_step // NA_ROWS_PER_ITER, row_body, 0)

    for p in pairs:
        vc = vc_ref[0, :, ls[p]]
        oc = jnp.where(first_head, _dot(pc_scr[2 * p], vc), _dot(pc_scr[2 * p + 1], vc))
        o_ref[0, :, ls[p]] = (ow_scr[p] + oc).astype(BF16)


def _na(zna, zna_ctx, tbl, rows_per_step):
    bsz, seq, _ = zna.shape
    n_ctx = zna_ctx.shape[1]
    n_rows = seq // GRID_W
    win_rows = min(WIN_ROWS, n_rows)
    tq = rows_per_step * GRID_W
    assert rows_per_step % NA_ROWS_PER_ITER == 0
    return pl.pallas_call(
        functools.partial(_na_kernel, rows_per_step=rows_per_step, n_rows=n_rows, win_rows=win_rows),
        grid=(bsz, n_rows // rows_per_step),
        in_specs=[
            pl.BlockSpec((1, tq, NA_WIDTH), lambda b, i: (b, i, 0)),
            pl.BlockSpec((1, seq, NA_WIDTH), lambda b, i: (b, 0, 1)),
            pl.BlockSpec((1, seq, NA_WIDTH), lambda b, i: (b, 0, 2)),
            pl.BlockSpec((1, n_ctx, NA_WIDTH), lambda b, i: (b, 0, 1)),
            pl.BlockSpec((1, n_ctx, NA_WIDTH), lambda b, i: (b, 0, 2)),
            pl.BlockSpec(tbl.shape, lambda b, i: (0, 0, 0, 0)),
        ],
        out_specs=pl.BlockSpec((1, tq, NA_WIDTH), lambda b, i: (b, i, 0)),
        out_shape=jax.ShapeDtypeStruct((bsz, seq, NA_WIDTH), BF16),
        scratch_shapes=[pltpu.VMEM((NA_HEADS, tq, n_ctx), F32), pltpu.VMEM((NA_HEADS, tq, n_ctx), BF16),
                        pltpu.VMEM((NA_WIDTH // LANES, tq, LANES), F32)],
        compiler_params=_cparams(("parallel", "arbitrary")),
        name="nbr_attn",
    )(zna, zna, zna, zna_ctx, zna_ctx, tbl)


def _na_ctx_kernel(q_ref, k_ref, v_ref, o_ref):
    lane = lax.broadcasted_iota(jnp.int32, (1, LANES), 1)
    first_head = lane < HEAD_DIM
    scale = HEAD_DIM ** -0.5
    for p in range(NA_WIDTH // LANES):
        ls = slice(p * LANES, (p + 1) * LANES)
        qp = q_ref[0, :, ls] * scale
        kp = k_ref[0, :, ls]
        vp = v_ref[0, :, ls]
        outs = []
        for hh in range(2):
            msk = first_head if hh == 0 else jnp.logical_not(first_head)
            qm = jnp.where(msk, qp, jnp.zeros_like(qp[:1, :1]))
            s = _dot_nt(qm, kp)
            e = jnp.exp(s - jnp.max(s, axis=-1, keepdims=True))
            den = jnp.sum(e, axis=-1, keepdims=True)
            outs.append(_dot(e.astype(BF16), vp) / den)
        o_ref[0, :, ls] = jnp.where(first_head, outs[0], outs[1]).astype(BF16)


def _na_ctx(zna_ctx):
    bsz, n_ctx, _ = zna_ctx.shape
    col = lambda j: pl.BlockSpec((1, n_ctx, NA_WIDTH), lambda b: (b, 0, j))
    return pl.pallas_call(
        _na_ctx_kernel,
        grid=(bsz,),
        in_specs=[col(0), col(1), col(2)],
        out_specs=pl.BlockSpec((1, n_ctx, NA_WIDTH), lambda b: (b, 0, 0)),
        out_shape=jax.ShapeDtypeStruct((bsz, n_ctx, NA_WIDTH), BF16),
        compiler_params=_cparams(("parallel",)),
        name="ctx_attn",
    )(zna_ctx, zna_ctx, zna_ctx)


def _outproj_kernel(ya_ref, yb_ref, yn_ref, w_ref, x_ref, g_ref, lg_ref, lb_ref, o_ref, *, alpha):
    ycat = jnp.concatenate([ya_ref[0], yb_ref[0], yn_ref[0]], axis=1)
    y = _dot(ycat, w_ref[...])
    s = alpha * x_ref[0] + g_ref[0] * y
    o_ref[0] = _layer_norm_rows(s) * lg_ref[...] + lb_ref[...]


def _out_proj(ya, yb, yn, w_out_b, x, mod3, lg, lb, is_ctx, ctx_row, tm, alpha):
    bsz, seq, d = x.shape
    tok = lambda w: pl.BlockSpec((1, tm, w), lambda b, i: (b, i, 0))
    full = lambda a: pl.BlockSpec(a.shape, lambda b, i: (0,) * a.ndim)
    return pl.pallas_call(
        functools.partial(_outproj_kernel, alpha=alpha),
        grid=(bsz, seq // tm),
        in_specs=[tok(GMLP_WIDTH), tok(DN_WIDTH), tok(NA_WIDTH), full(w_out_b), tok(d),
                  _mod_spec(2, d, is_ctx, ctx_row), full(lg), full(lb)],
        out_specs=tok(d),
        out_shape=jax.ShapeDtypeStruct((bsz, seq, d), F32),
        compiler_params=_cparams(("parallel", "parallel")),
        name="out_proj",
    )(ya, yb, yn, w_out_b, x, mod3, lg, lb)


FFN_CHUNK = 256


def _ffn_kernel(x_ref, xp_ref, xn_ref, sh_ref, sc_ref, g_ref, wup_ref, cw_ref, cb_ref, wd_ref,
                lg_ref, lb_ref, o_ref, gated_ref, *, tm, n_tiles, d_ff, alpha):
    halo = 8
    i = pl.program_id(1)
    has_prev = (i > 0).astype(F32)
    has_next = (i < n_tiles - 1).astype(F32)
    x = x_ref[0]
    sc = 1.0 + sc_ref[0]
    sh = sh_ref[0]
    h_mid = _layer_norm_rows(x) * sc + sh
    h_prev = (_layer_norm_rows(xp_ref[0]) * sc + sh) * has_prev
    h_next = (_layer_norm_rows(xn_ref[0]) * sc + sh) * has_next
    h_ext = jnp.concatenate([h_prev, h_mid, h_next], axis=0).astype(BF16)
    h_b = h_mid.astype(BF16)
    n_ext = tm + 2 * halo
    for j in range(d_ff // FFN_CHUNK):
        cols = slice(j * FFN_CHUNK, (j + 1) * FFN_CHUNK)
        a = _dot(h_ext, wup_ref[:, cols])
        vv = _dot(h_b, wup_ref[:, d_ff + j * FFN_CHUNK:d_ff + (j + 1) * FFN_CHUNK])
        cw = cw_ref[:, cols]
        a_m1 = pltpu.roll(a, 1, 0)[halo:halo + tm]
        a_p1 = pltpu.roll(a, n_ext - 1, 0)[halo:halo + tm]
        conv = a_m1 * cw[0:1] + a[halo:halo + tm] * cw[1:2] + a_p1 * cw[2:3] + cb_ref[:, cols]
        gated_ref[:, cols] = (_silu(conv) * vv).astype(BF16)
    y = _dot(gated_ref[...], wd_ref[...])
    s = alpha * x + g_ref[0] * y
    o_ref[0] = _layer_norm_rows(s) * lg_ref[...] + lb_ref[...]


def _ffn(x, mod3, wup, cw, cb, wd, lg, lb, is_ctx, ctx_row, tm, alpha):
    bsz, seq, d = x.shape
    n_tiles = seq // tm
    d_ff = wd.shape[0]
    assert d_ff % FFN_CHUNK == 0
    hb = tm // 8
    tok = pl.BlockSpec((1, tm, d), lambda b, i: (b, i, 0))
    full = lambda a: pl.BlockSpec(a.shape, lambda b, i: (0,) * a.ndim)
    return pl.pallas_call(
        functools.partial(_ffn_kernel, tm=tm, n_tiles=n_tiles, d_ff=d_ff, alpha=alpha),
        grid=(bsz, n_tiles),
        in_specs=[
            tok,
            pl.BlockSpec((1, 8, d), lambda b, i: (b, jnp.maximum(i * hb - 1, 0), 0)),
            pl.BlockSpec((1, 8, d), lambda b, i: (b, jnp.minimum((i + 1) * hb, seq // 8 - 1), 0)),
            _mod_spec(3, d, is_ctx, ctx_row), _mod_spec(4, d, is_ctx, ctx_row), _mod_spec(5, d, is_ctx, ctx_row),
            full(wup), full(cw), full(cb), full(wd), full(lg), full(lb),
        ],
        out_specs=tok,
        out_shape=jax.ShapeDtypeStruct((bsz, seq, d), F32),
        scratch_shapes=[pltpu.VMEM((tm, d_ff), BF16)],
        compiler_params=_cparams(("parallel", "parallel")),
        name="conv_glu",
    )(x, x, x, mod3, mod3, mod3, wup, cw, cb, wd, lg, lb)


def _rope_tables(seq):
    half = HEAD_DIM // 2
    nf = half // 2
    pos = jnp.arange(seq)
    inv = ROPE_BASE ** (-jnp.arange(nf, dtype=F32) / nf)
    ang_r = (pos // GRID_W)[:, None].astype(F32) * inv
    ang_c = (pos % GRID_W)[:, None].astype(F32) * inv
    cos_h = jnp.concatenate([jnp.cos(ang_r)] * 2 + [jnp.cos(ang_c)] * 2, axis=1)
    sin_h = jnp.concatenate([-jnp.sin(ang_r), jnp.sin(ang_r), -jnp.sin(ang_c), jnp.sin(ang_c)], axis=1)
    reps = LANES // HEAD_DIM
    return jnp.tile(cos_h, (1, reps)), jnp.tile(sin_h, (1, reps))


def _bias_table(rpb, n_rows):
    wr = min(WIN_ROWS, n_rows)
    nh = rpb.shape[0]
    n_dr = 2 * WIN_ROWS - 1
    col = np.arange(GRID_W)
    cs = np.clip(col - WIN_COLS // 2, 0, GRID_W - WIN_COLS)
    valid = (col[None, :] >= cs[:, None]) & (col[None, :] < cs[:, None] + WIN_COLS)
    dc = col[None, :] - col[:, None] + (WIN_COLS - 1)
    onehot = np.zeros((2 * WIN_COLS - 1, GRID_W, GRID_W), np.float32)
    cc, kk = np.nonzero(valid)
    onehot[dc[cc, kk], cc, kk] = 1.0
    t = jnp.einsum('hdm,mck->hdck', rpb.astype(F32), jnp.asarray(onehot), precision=lax.Precision.HIGHEST)
    t = jnp.where(valid[None, None], t, NEG_BIG)
    per_var = []
    for var in range(wr):
        lo = WIN_ROWS - 1 - var
        per_var.append(t[:, lo:lo + wr].transpose(0, 2, 1, 3).reshape(nh, GRID_W, wr * GRID_W))
    return jnp.stack(per_var, axis=1)


def _gate_mats():
    place = np.zeros((GATE_PIECES, LANES, LANES), np.float32)
    expand = np.zeros((2, LANES, 2 * DN_WIDTH), np.float32)
    for dd in range(2):
        for h in range(DN_HEADS):
            cols = slice(h * HEAD_DIM, (h + 1) * HEAD_DIM)
            for kk in range(3):
                dst = dd * GATE_DIR_STRIDE + kk * DN_HEADS + h
                place[kk, dd * DN_HEADS + h, dst] = 1.0
                expand[dd, dst, cols] = 1.0
            for kk in range(2):
                dst = dd * GATE_DIR_STRIDE + (3 + kk) * DN_HEADS + h
                place[3 + kk, 2 * DN_HEADS + dd * DN_HEADS + h, dst] = 1.0
                expand[dd, dst, DN_WIDTH + h * HEAD_DIM:DN_WIDTH + (h + 1) * HEAD_DIM] = 1.0
    return jnp.asarray(place, BF16), jnp.asarray(expand[0], BF16), jnp.asarray(expand[1], BF16)


def _pad_lanes(v):
    return jnp.zeros((1, LANES), F32).at[0, :v.shape[0]].set(v.astype(F32))


def _token_tile(seq):
    return 512 if seq % 512 == 0 else 256


def kernel(x, c, ctx, c_ctx, w_ada, b_ada, w_in, gmlp_ln_g, gmlp_ln_b, gmlp_ws, gmlp_bs, dn_conv, dn_a_log,
           dn_dt_bias, dn_norm_g, na_rpb, w_out, ln1_g, ln1_b, ffn_up, ffn_conv, ffn_conv_b, ffn_down,
           ln2_g, ln2_b):
    depth = w_ada.shape[0]
    bsz, seq, d = x.shape
    n_ctx = ctx.shape[1]
    d_ff = ffn_down.shape[1]
    alpha = (2 * depth) ** 0.25
    ctx_row = bsz
    tm = _token_tile(seq)
    tm_c = _token_tile(n_ctx)

    mod = _mod_all(c, c_ctx, w_ada, b_ada)
    rope_tabs = _rope_tables(seq)
    place, e_fwd, e_bwd = _gate_mats()
    gm256_mean = jnp.asarray(_block_diag_np(GMLP_HEADS, HEAD_DIM, HEAD_DIM, 1.0 / HEAD_DIM), BF16)
    gm384_sum = jnp.asarray(_block_diag_np(DN_HEADS, HEAD_DIM, HEAD_DIM, 1.0), BF16)
    gm384_mean = jnp.asarray(_block_diag_np(DN_HEADS, HEAD_DIM, HEAD_DIM, 1.0 / HEAD_DIM), BF16)
    assert bsz % 2 == 0, "the DeltaNet scan pairs batch elements"
    nb = 4 if bsz % 4 == 0 else 2
    s_zero = jnp.zeros((bsz // nb, nb + nb // 2, 4 * HEAD_DIM, 4 * HEAD_DIM), F32)
    o1 = GMLP_COLS
    o2 = o1 + DN_MAIN_COLS
    o3 = o2 + DN_AB_COLS

    for l in range(depth):
        ctx_out = l < depth - 1
        mod3 = mod[l].reshape(8, 1, 6 * d)
        wl = w_in[l]
        w_in_p = jnp.concatenate(
            [wl[:, :o2], wl[:, o3:], wl[:, o2:o3], jnp.zeros((d, LANES - DN_AB_COLS), F32)], axis=1).astype(BF16)
        lng = gmlp_ln_g[l].reshape(1, GMLP_WIDTH)
        lnb = gmlp_ln_b[l].reshape(1, GMLP_WIDTH)
        wsp = gmlp_ws[l].transpose(1, 0, 2).reshape(GMLP_CHUNK, GMLP_HEADS * GMLP_CHUNK).astype(BF16)
        bsx = jnp.repeat(gmlp_bs[l].T, HEAD_DIM, axis=1)
        aexp_row = _pad_lanes(jnp.exp(dn_a_log[l].astype(F32)).reshape(-1))
        dtb_row = _pad_lanes(dn_dt_bias[l].reshape(-1))
        ng_row = jnp.tile(dn_norm_g[l].astype(F32), DN_HEADS).reshape(1, DN_WIDTH)
        tbl = _bias_table(na_rpb[l], seq // GRID_W)
        w_out_b = w_out[l].astype(BF16)
        lg1 = ln1_g[l].reshape(1, d)
        lb1 = ln1_b[l].reshape(1, d)
        lg2 = ln2_g[l].reshape(1, d)
        lb2 = ln2_b[l].reshape(1, d)
        wup = ffn_up[l].astype(BF16)
        wd = ffn_down[l].astype(BF16)
        cw = ffn_conv[l]
        cb = ffn_conv_b[l].reshape(1, d_ff)

        zg_c, zdn_c, zna_c, zab_c = _in_proj(ctx, mod3, w_in_p, True, ctx_row, tm_c)
        qkv_c, gp_c = _dn_prep(zdn_c, zab_c, dn_conv[l], aexp_row, dtb_row, gm384_sum, place, None, tm_c)
        yb_c, s_ctx = _dn_scan(qkv_c, zdn_c, gp_c, e_fwd, e_bwd, ng_row, gm384_mean, (s_zero, s_zero), nb)

        zg, zdn, zna, zab = _in_proj(x, mod3, w_in_p, False, ctx_row, tm)
        ya = _gmlp(zg, gm256_mean, lng, lnb, wsp, bsx, tm)
        qkv, gp = _dn_prep(zdn, zab, dn_conv[l], aexp_row, dtb_row, gm384_sum, place, rope_tabs, tm)
        yb, _ = _dn_scan(qkv, zdn, gp, e_fwd, e_bwd, ng_row, gm384_mean, s_ctx, nb)
        yn = _na(zna, zna_c, tbl, 8)
        x1 = _out_proj(ya, yb, yn, w_out_b, x, mod3, lg1, lb1, False, ctx_row, tm, alpha)
        x_next = _ffn(x1, mod3, wup, cw, cb, wd, lg2, lb2, False, ctx_row, tm, alpha)

        if ctx_out:
            ya_c = _gmlp(zg_c, gm256_mean, lng, lnb, wsp, bsx, tm_c)
            yn_c = _na_ctx(zna_c)
            c1 = _out_proj(ya_c, yb_c, yn_c, w_out_b, ctx, mod3, lg1, lb1, True, ctx_row, tm_c, alpha)
            ctx = _ffn(c1, mod3, wup, cw, cb, wd, lg2, lb2, True, ctx_row, tm_c, alpha)
        x = x_next
    return x
```

```python
import functools
import math

import numpy as np
import jax
import jax.numpy as jnp
from jax import lax
from jax.experimental import pallas as pl
from jax.experimental.pallas import tpu as pltpu

F32 = jnp.float32
BF16 = jnp.bfloat16

HEAD_DIM = 64
GRID_W = 64
GMLP_HEADS = 4
GMLP_WIDTH = GMLP_HEADS * HEAD_DIM
GMLP_CHUNK = 128
DN_HEADS = 6
DN_WIDTH = DN_HEADS * HEAD_DIM
NA_HEADS = 6
NA_WIDTH = NA_HEADS * HEAD_DIM
DELTA_CHUNK = 64
WIN_ROWS = 8
WIN_COLS = 16
ROPE_BASE = 10000.0
EPS = 1e-6
NEG_BIG = -1e30
LANES = 128
VMEM_LIMIT = 56 * 1024 * 1024

GMLP_COLS = 2 * GMLP_WIDTH
DN_MAIN_COLS = 4 * DN_WIDTH
DN_AB_COLS = 4 * DN_HEADS
NA_COLS = 3 * NA_WIDTH


def _cparams(sem):
    return pltpu.CompilerParams(dimension_semantics=sem, vmem_limit_bytes=VMEM_LIMIT)


def _split2(x):
    hi = x.astype(BF16)
    lo = (x - hi.astype(F32)).astype(BF16)
    return hi, lo


def _split3(x):
    a = x.astype(BF16)
    r = x - a.astype(F32)
    b = r.astype(BF16)
    c = (r - b.astype(F32)).astype(BF16)
    return a, b, c


def _dot(a, b):
    return jnp.dot(a, b, preferred_element_type=F32)


def _dot_nt(a, b):
    return lax.dot_general(a, b, (((1,), (1,)), ((), ())), preferred_element_type=F32)


def _dot_tn(a, b):
    return lax.dot_general(a, b, (((0,), (0,)), ((), ())), preferred_element_type=F32)


def _dot_x3(x, m):
    a, b, c = _split3(x)
    return _dot(a, m) + _dot(b, m) + _dot(c, m)


def _dot_x2(x, m):
    a, b = _split2(x)
    return _dot(a, m) + _dot(b, m)


def _head_sums(x, pair_mat):
    parts = [_dot(x[:, p * LANES:(p + 1) * LANES].astype(BF16), pair_mat) for p in range(x.shape[1] // LANES)]
    return jnp.concatenate(parts, axis=1)


def _silu(x):
    return x * jax.nn.sigmoid(x)


def _layer_norm_rows(x):
    mu = jnp.mean(x, axis=-1, keepdims=True)
    xc = x - mu
    var = jnp.mean(xc * xc, axis=-1, keepdims=True)
    return xc * lax.rsqrt(var + EPS)


def _block_diag_np(n_blocks, rows, cols, value):
    m = np.zeros((n_blocks * rows, n_blocks * cols), np.float32)
    for g in range(n_blocks):
        m[g * rows:(g + 1) * rows, g * cols:(g + 1) * cols] = value
    return m


def _mod_kernel(c_ref, w_ref, b_ref, o_ref):
    a = _silu(c_ref[...])
    w = w_ref[0]
    a1, a2 = _split2(a)
    w1, w2 = _split2(w)
    o_ref[0] = _dot(a1, w1) + _dot(a1, w2) + _dot(a2, w1) + b_ref[0]


def _mod_all(c, c_ctx, w_ada, b_ada):
    n_layers, d, n6 = w_ada.shape
    bsz = c.shape[0]
    rows = jnp.concatenate([c, c_ctx[None, :], jnp.zeros((8 - bsz - 1, d), F32)], axis=0)
    tn = 1536
    return pl.pallas_call(
        _mod_kernel,
        grid=(n_layers, n6 // tn),
        in_specs=[
            pl.BlockSpec((8, d), lambda l, j: (0, 0)),
            pl.BlockSpec((1, d, tn), lambda l, j: (l, 0, j)),
            pl.BlockSpec((1, 1, tn), lambda l, j: (l, 0, j)),
        ],
        out_specs=pl.BlockSpec((1, 8, tn), lambda l, j: (l, 0, j)),
        out_shape=jax.ShapeDtypeStruct((n_layers, 8, n6), F32),
        compiler_params=_cparams(("parallel", "parallel")),
        name="adaln_mod",
    )(rows, w_ada, b_ada.reshape(n_layers, 1, n6))


def _mod_spec(chunk, d, is_ctx, ctx_row):
    if is_ctx:
        return pl.BlockSpec((1, 1, d), lambda b, i: (ctx_row, 0, chunk))
    return pl.BlockSpec((1, 1, d), lambda b, i: (b, 0, chunk))


def _inproj_kernel(*refs, tm, n_tiles, rotary):
    if rotary:
        (x_ref, xp_ref, xn_ref, sh_ref, sc_ref, w_ref, cw_ref, aexp_ref, dtb_ref, g_ref, cos_ref, sin_ref,
         zg_ref, qkv_ref, gate_ref, gp_ref, zna_ref) = refs
    else:
        (x_ref, xp_ref, xn_ref, sh_ref, sc_ref, w_ref, cw_ref, aexp_ref, dtb_ref, g_ref,
         zg_ref, qkv_ref, gate_ref, gp_ref, zna_ref) = refs
    halo = 8
    i = pl.program_id(1)
    has_prev = (i > 0).astype(F32)
    has_next = (i < n_tiles - 1).astype(F32)
    sc = 1.0 + sc_ref[0]
    sh = sh_ref[0]
    h_mid = _layer_norm_rows(x_ref[0]) * sc + sh
    h_prev = (_layer_norm_rows(xp_ref[0]) * sc + sh) * has_prev
    h_next = (_layer_norm_rows(xn_ref[0]) * sc + sh) * has_next
    h_ext = jnp.concatenate([h_prev, h_mid, h_next], axis=0).astype(BF16)
    h = h_mid.astype(BF16)
    o_q = GMLP_COLS
    o_g = o_q + 3 * DN_WIDTH
    o_n = o_g + DN_WIDTH
    o_ab = o_n + NA_COLS

    x_ext = _dot(h_ext, w_ref[:, o_q:o_g])
    ab = _dot(h, w_ref[:, o_ab:o_ab + LANES])
    n_ext = tm + 2 * halo
    x_m1 = pltpu.roll(x_ext, 1, 0)[halo:halo + tm]
    x_p1 = pltpu.roll(x_ext, n_ext - 1, 0)[halo:halo + tm]
    x_0 = x_ext[halo:halo + tm]
    cw = cw_ref[...]
    y = _silu(x_m1 * cw[0:1] + x_0 * cw[1:2] + x_p1 * cw[2:3])
    zg_ref[0] = _dot(h, w_ref[:, 0:o_q]).astype(BF16)
    q = y[:, 0:DN_WIDTH]
    k = y[:, DN_WIDTH:2 * DN_WIDTH]
    v = y[:, 2 * DN_WIDTH:3 * DN_WIDTH]
    gmat = g_ref[...]
    q = q * lax.rsqrt(_head_sums(q * q, gmat) + EPS)
    k = k * lax.rsqrt(_head_sums(k * k, gmat) + EPS)
    gate_ref[0] = _dot(h, w_ref[:, o_g:o_n]).astype(BF16)
    if rotary:
        cos2 = cos_ref[...]
        sin2 = sin_ref[...]
        lane = lax.broadcasted_iota(jnp.int32, (tm, LANES), 1)
        first = (lane % 32) < 16

        def rope(t):
            parts = []
            for p in range(DN_WIDTH // LANES):
                tp = t[:, p * LANES:(p + 1) * LANES]
                partner = jnp.where(first, pltpu.roll(tp, LANES - 16, 1), pltpu.roll(tp, 16, 1))
                parts.append(tp * cos2 + partner * sin2)
            return jnp.concatenate(parts, axis=1)

        q = rope(q)
        k = rope(k)
    q = q * (HEAD_DIM ** -0.5)
    qkv_ref[0, :, 0:DN_WIDTH] = q.astype(BF16)
    qkv_ref[0, :, DN_WIDTH:2 * DN_WIDTH] = k.astype(BF16)
    qkv_ref[0, :, 2 * DN_WIDTH:3 * DN_WIDTH] = v.astype(BF16)
    zna_ref[0] = _dot(h, w_ref[:, o_n:o_ab]).astype(BF16)

    lane_ab = lax.broadcasted_iota(jnp.int32, ab.shape, 1)
    in_copy = lane_ab % GATE_COPY_STRIDE
    copy = lane_ab // GATE_COPY_STRIDE
    xs = ab + dtb_ref[...]
    softplus = jnp.maximum(xs, 0.0) + jnp.log1p(jnp.exp(-jnp.abs(xs)))
    log_decay = -aexp_ref[...] * softplus
    beta = jax.nn.sigmoid(ab)
    blk = min(tm, 4 * DELTA_CHUNK)
    ti = lax.broadcasted_iota(jnp.int32, (blk, blk), 0)
    tj = lax.broadcasted_iota(jnp.int32, (blk, blk), 1)
    same_chunk = (ti // DELTA_CHUNK) == (tj // DELTA_CHUNK)
    l_fwd = jnp.where(same_chunk, jnp.where(ti >= tj, 1.0, 0.0), 0.0).astype(BF16)
    l_bwd = jnp.where(same_chunk, jnp.where(ti <= tj, 1.0, 0.0), 0.0).astype(BF16)
    g1, g2 = _split2(log_decay)
    gc_f, gc_b = [], []
    for rb in range(tm // blk):
        rows = slice(rb * blk, (rb + 1) * blk)
        gc_f.append(_dot(l_fwd, g1[rows]) + _dot(l_fwd, g2[rows]))
        gc_b.append(_dot(l_bwd, g1[rows]) + _dot(l_bwd, g2[rows]))
    gc = jnp.where(in_copy < DN_HEADS, jnp.concatenate(gc_f, axis=0), jnp.concatenate(gc_b, axis=0))
    c1, c2, c3 = _split3(gc)
    b1, b2 = _split2(beta)
    dec_piece = jnp.where(copy == 0, c1, jnp.where(copy == 1, c2, c3)).astype(F32)
    beta_piece = jnp.where(copy == 0, b1, jnp.where(copy == 1, b2, jnp.zeros_like(b2))).astype(F32)
    packed = jnp.where(in_copy < 2 * DN_HEADS, dec_piece, beta_piece)
    gp_ref[0] = packed.astype(BF16)


def _in_proj(x, mod3, w_in_p, conv_w, aexp_row, dtb_row, gm384, rope_tabs, is_ctx, ctx_row, tm):
    bsz, seq, d = x.shape
    n_tiles = seq // tm
    hb = tm // 8
    rotary = rope_tabs is not None
    tok = lambda w: pl.BlockSpec((1, tm, w), lambda b, i: (b, i, 0))
    full = lambda a: pl.BlockSpec(a.shape, lambda b, i: (0,) * a.ndim)
    in_specs = [
        tok(d),
        pl.BlockSpec((1, 8, d), lambda b, i: (b, jnp.maximum(i * hb - 1, 0), 0)),
        pl.BlockSpec((1, 8, d), lambda b, i: (b, jnp.minimum((i + 1) * hb, seq // 8 - 1), 0)),
        _mod_spec(0, d, is_ctx, ctx_row),
        _mod_spec(1, d, is_ctx, ctx_row),
        full(w_in_p), full(conv_w), full(aexp_row), full(dtb_row), full(gm384),
    ]
    args = [x, x, x, mod3, mod3, w_in_p, conv_w, aexp_row, dtb_row, gm384]
    if rotary:
        in_specs += [pl.BlockSpec((tm, LANES), lambda b, i: (i, 0))] * 2
        args += list(rope_tabs)
    widths = [GMLP_COLS, 3 * DN_WIDTH, DN_WIDTH, LANES, NA_COLS]
    return pl.pallas_call(
        functools.partial(_inproj_kernel, tm=tm, n_tiles=n_tiles, rotary=rotary),
        grid=(bsz, n_tiles),
        in_specs=in_specs,
        out_specs=[tok(w) for w in widths],
        out_shape=[jax.ShapeDtypeStruct((bsz, seq, w), BF16) for w in widths],
        compiler_params=_cparams(("parallel", "parallel")),
        name="in_proj",
    )(*args)


def _gmlp_kernel(z_ref, g_ref, lng_ref, lnb_ref, wsp_ref, bsx_ref, o_ref, *, tm):
    z = jax.nn.gelu(z_ref[0].astype(F32))
    u = z[:, :GMLP_WIDTH]
    v = z[:, GMLP_WIDTH:]
    gmat = g_ref[...]
    mu = _dot_x2(v, gmat)
    vc = v - mu
    var = _dot_x2(vc * vc, gmat)
    vn = vc * lax.rsqrt(var + EPS) * lng_ref[...] + lnb_ref[...]
    vn = vn.astype(BF16)
    nrow = GMLP_HEADS * GMLP_CHUNK
    ri = lax.broadcasted_iota(jnp.int32, (nrow, GMLP_WIDTH), 0) // GMLP_CHUNK
    ci = lax.broadcasted_iota(jnp.int32, (nrow, GMLP_WIDTH), 1) // HEAD_DIM
    diag = ri == ci
    wsp = wsp_ref[...]
    bsx = bsx_ref[...]
    for n in range(tm // GMLP_CHUNK):
        rows = slice(n * GMLP_CHUNK, (n + 1) * GMLP_CHUNK)
        vch = vn[rows]
        bd = jnp.where(diag, jnp.concatenate([vch] * GMLP_HEADS, axis=0), jnp.zeros_like(vch[:1, :1]))
        mixed = _dot(wsp, bd) + bsx
        o_ref[0, rows, :] = (u[rows] * mixed).astype(BF16)


def _gmlp(zg, gm256, lng, lnb, wsp, bsx, tm):
    bsz, seq, _ = zg.shape
    full = lambda a: pl.BlockSpec(a.shape, lambda b, i: (0,) * a.ndim)
    return pl.pallas_call(
        functools.partial(_gmlp_kernel, tm=tm),
        grid=(bsz, seq // tm),
        in_specs=[pl.BlockSpec((1, tm, GMLP_COLS), lambda b, i: (b, i, 0)),
                  full(gm256), full(lng), full(lnb), full(wsp), full(bsx)],
        out_specs=pl.BlockSpec((1, tm, GMLP_WIDTH), lambda b, i: (b, i, 0)),
        out_shape=jax.ShapeDtypeStruct((bsz, seq, GMLP_WIDTH), BF16),
        compiler_params=_cparams(("parallel", "parallel")),
        name="gmlp_mix",
    )(zg, gm256, lng, lnb, wsp, bsx)


GATE_COPIES = 3
GATE_COPY_STRIDE = 32
INV_BLOCK = 8
DN_CHUNKS_PER_STEP = 4


def _dn_intra(q, k, v, gcx, bx, masks):
    cs = DELTA_CHUNK
    causal, strict, eye, last_row, diag, diag_mul, inv_diag, inv_levels = masks
    grp = range(len(q))

    def bd(x):
        return jnp.concatenate([x.astype(BF16)] * 4, axis=0) * diag_mul

    qf = [q[g].astype(F32) for g in grp]
    kf = [k[g].astype(F32) for g in grp]
    vf = [v[g].astype(F32) for g in grp]
    gc_t = [jnp.sum(jnp.where(eye, gcx[g], 0.0), axis=0, keepdims=True) for g in grp]
    dm = [jnp.where(causal, jnp.exp(jnp.where(causal, gcx[g] - gc_t[g], 0.0)), 0.0) for g in grp]
    dms = [jnp.where(strict, dm[g], 0.0) for g in grp]
    gl = [jnp.sum(jnp.where(last_row, gcx[g], 0.0), axis=0, keepdims=True) for g in grp]
    egx = [jnp.exp(gcx[g]) for g in grp]
    kdx = [jnp.exp(gl[g] - gcx[g]) for g in grp]
    kb = [kf[g] * bx[g] for g in grp]
    vb = [vf[g] * bx[g] for g in grp]
    bk = [bd(k[g]) for g in grp]
    lhs = [jnp.concatenate([kb[g].astype(BF16), q[g]], axis=0) for g in grp]
    p1 = [_dot_nt(lhs[g], bk[g]) for g in grp]
    a_mat = [p1[g][:cs] * dms[g] for g in grp]
    attn = [p1[g][cs:] * dm[g] for g in grp]

    eye_f = jnp.where(eye, 1.0, 0.0)
    a_d = [a_mat[g] * inv_diag for g in grp]
    t_mat = [eye_f - a_d[g] for g in grp]
    b_pow = [_dot(a_d[g].astype(BF16), bd(a_d[g])) for g in grp]
    pp = [_dot(jnp.concatenate([t_mat[g], b_pow[g]], axis=0).astype(BF16), bd(b_pow[g])) for g in grp]
    t_mat = [t_mat[g] + pp[g][:cs] for g in grp]
    t_mat = [t_mat[g] + _dot(t_mat[g].astype(BF16), bd(pp[g][cs:])) for g in grp]
    for coupling in inv_levels:
        x = [_dot(t_mat[g].astype(BF16), bd(a_mat[g] * coupling)) for g in grp]
        t_mat = [t_mat[g] - _dot(x[g].astype(BF16), bd(t_mat[g])) for g in grp]

    tb = [t_mat[g].astype(BF16) for g in grp]
    u = [_dot(tb[g], bd(vb[g])) for g in grp]
    wk = [_dot(tb[g], bd(kb[g] * egx[g])) for g in grp]
    lhs2 = [jnp.concatenate([wk[g], qf[g] * egx[g]], axis=0).astype(BF16) for g in grp]
    kd = [(kf[g] * kdx[g]).astype(BF16) for g in grp]
    egl = [jnp.exp(gl[g]) for g in grp]
    attn_b = [attn[g].astype(BF16) for g in grp]
    return u, lhs2, attn_b, kd, egl


def _dn_inter(intra, st, masks):
    cs = DELTA_CHUNK
    diag, diag_mul = masks[4], masks[5]
    u, lhs2, attn_b, kd, egl = intra
    grp = range(len(u))
    p2 = [_dot(lhs2[g], st[g].astype(BF16)) for g in grp]
    v_new = [u[g] - p2[g][:cs] for g in grp]
    vnb = [v_new[g].astype(BF16) for g in grp]
    bdv = [jnp.concatenate([vnb[g]] * 4, axis=0) * diag_mul for g in grp]
    o = [p2[g][cs:] + _dot(attn_b[g], bdv[g]) for g in grp]
    st_new = [st[g] * egl[g] + jnp.where(diag, _dot_tn(kd[g], vnb[g]), 0.0) for g in grp]
    return o, st_new


def _dn_masks(reverse):
    cs = DELTA_CHUNK
    wg = 4 * HEAD_DIM
    i3 = lax.broadcasted_iota(jnp.int32, (cs, wg), 0)
    j3 = lax.broadcasted_iota(jnp.int32, (cs, wg), 1) % HEAD_DIM
    if reverse:
        causal, strict, last_row = i3 <= j3, i3 < j3, i3 == 0
    else:
        causal, strict, last_row = i3 >= j3, i3 > j3, i3 == cs - 1
    ra = lax.broadcasted_iota(jnp.int32, (wg, wg), 0) // HEAD_DIM
    ca = lax.broadcasted_iota(jnp.int32, (wg, wg), 1) // HEAD_DIM
    diag = ra == ca
    diag_mul = jnp.where(diag, 1.0, 0.0).astype(BF16)
    inv_diag = jnp.where(i3 // INV_BLOCK == j3 // INV_BLOCK, 1.0, 0.0)
    inv_levels = []
    size = INV_BLOCK
    while size < cs:
        same_pair = i3 // (2 * size) == j3 // (2 * size)
        inv_levels.append(jnp.where(same_pair, jnp.where(i3 // size == j3 // size, 0.0, 1.0), 0.0))
        size *= 2
    return causal, strict, i3 == j3, last_row, diag, diag_mul, inv_diag, inv_levels


def _dn_scan_kernel(*refs, n_steps, nb, cps, reverse, finish):
    if finish:
        (qkv_ref, gp_ref, e_ref, s0_ref, of_ref, gate_ref, ng_ref, gm_ref, out_ref, s_out_ref, st_scr) = refs
    else:
        (qkv_ref, gp_ref, e_ref, s0_ref, out_ref, s_out_ref, st_scr) = refs
    w = DN_WIDTH
    wg = 4 * HEAD_DIM
    wr = w - wg
    n_grp = nb + nb // 2
    i = pl.program_id(1)

    @pl.when(i == 0)
    def _():
        st_scr[...] = s0_ref[...]

    masks = _dn_masks(reverse)
    e_comb = e_ref[...]
    cs = DELTA_CHUNK
    order = list(range(cps - 1, -1, -1)) if reverse else list(range(cps))

    def groups(arrs, off):
        out = [a[:, off:off + wg] for a in arrs]
        for p in range(nb // 2):
            out.append(jnp.concatenate([arrs[2 * p][:, off + wg:off + w], arrs[2 * p + 1][:, off + wg:off + w]],
                                       axis=1))
        return out

    qs, ks, vs, gs, bs = [], [], [], [], []
    for c in order:
        rows = slice(c * cs, (c + 1) * cs)
        qkv = [qkv_ref[bb, rows, :] for bb in range(nb)]
        ex = [_dot(gp_ref[bb, rows, :], e_comb) for bb in range(nb)]
        qs += groups(qkv, 0)
        ks += groups(qkv, w)
        vs += groups(qkv, 2 * w)
        gs += groups(ex, 0)
        bs += groups(ex, w)
    intra = _dn_intra(qs, ks, vs, gs, bs, masks)
    st = [st_scr[g] for g in range(n_grp)]
    for idx, c in enumerate(order):
        rows = slice(c * cs, (c + 1) * cs)
        part = [t[idx * n_grp:(idx + 1) * n_grp] for t in intra]
        o_g, st = _dn_inter(part, st, masks)
        for bb in range(nb):
            rest = o_g[nb + bb // 2][:, (bb % 2) * wr:(bb % 2 + 1) * wr]
            o = jnp.concatenate([o_g[bb], rest], axis=1)
            if finish:
                ot = of_ref[bb, rows, :] + o
                ms = _head_sums(ot * ot, gm_ref[...])
                gate = gate_ref[bb, rows, :].astype(F32)
                out_ref[bb, rows, :] = (ot * lax.rsqrt(ms + EPS) * ng_ref[...] * _silu(gate)).astype(BF16)
            else:
                out_ref[bb, rows, :] = o
    for g in range(n_grp):
        st_scr[g] = st[g]

    @pl.when(i == n_steps - 1)
    def _():
        s_out_ref[...] = st_scr[...]


def _dn_scan_dir(qkv, gp, e_comb, s0, reverse, finish_args, nb):
    bsz, seq, _ = qkv.shape
    cps = DN_CHUNKS_PER_STEP
    cs = cps * DELTA_CHUNK
    n = seq // cs
    wg = 4 * HEAD_DIM
    n_grp = nb + nb // 2
    finish = finish_args is not None
    cidx = (lambda i: n - 1 - i) if reverse else (lambda i: i)
    full = lambda a: pl.BlockSpec(a.shape, lambda g, i: (0,) * a.ndim)
    tok = lambda wdt, col=0: pl.BlockSpec((nb, cs, wdt), lambda g, i: (g, cidx(i), col))
    st = pl.BlockSpec((None, n_grp, wg, wg), lambda g, i: (g, 0, 0, 0))
    in_specs = [tok(3 * DN_WIDTH), tok(LANES), full(e_comb), st]
    args = [qkv, gp, e_comb, s0]
    if finish:
        o_other, gate, ng_row, gm_mean = finish_args
        in_specs += [tok(DN_WIDTH), tok(DN_WIDTH), full(ng_row), full(gm_mean)]
        args += [o_other, gate, ng_row, gm_mean]
    return pl.pallas_call(
        functools.partial(_dn_scan_kernel, n_steps=n, nb=nb, cps=cps, reverse=reverse, finish=finish),
        grid=(bsz // nb, n),
        in_specs=in_specs,
        out_specs=[tok(DN_WIDTH), st],
        out_shape=[
            jax.ShapeDtypeStruct((bsz, seq, DN_WIDTH), BF16 if finish else F32),
            jax.ShapeDtypeStruct((bsz // nb, n_grp, wg, wg), F32),
        ],
        scratch_shapes=[pltpu.VMEM((n_grp, wg, wg), F32)],
        compiler_params=_cparams(("parallel", "arbitrary")),
        name="dn_scan_bwd" if reverse else "dn_scan_fwd",
    )(*args)


def _dn_scan(qkv, gate, gp, e_fwd, e_bwd, ng_row, gm_mean, s0, nb):
    o_f, s_f = _dn_scan_dir(qkv, gp, e_fwd, s0[0], False, None, nb)
    y, s_b = _dn_scan_dir(qkv, gp, e_bwd, s0[1], True, (o_f, gate, ng_row, gm_mean), nb)
    return y, (s_f, s_b)


NA_ROWS_PER_ITER = 2


def _na_kernel(q_ref, k_ref, v_ref, kc_ref, vc_ref, tbl_ref, o_ref, sc_scr, pc_scr, ow_scr, *,
               rows_per_step, n_rows, win_rows):
    i = pl.program_id(1)
    gw = GRID_W
    lane = lax.broadcasted_iota(jnp.int32, (1, LANES), 1)
    first_head = lane < HEAD_DIM
    scale = HEAD_DIM ** -0.5
    pairs = range(NA_WIDTH // LANES)
    heads = [(p, hh) for p in pairs for hh in range(2)]
    ls = [slice(p * LANES, (p + 1) * LANES) for p in pairs]
    zero = jnp.zeros((1, 1), BF16)

    def one_head(x, hh):
        return jnp.where(first_head if hh == 0 else jnp.logical_not(first_head), x, zero)

    kc = [kc_ref[0, :, ls[p]] for p in pairs]
    q_all = [q_ref[0, :, ls[p]] * scale for p in pairs]
    for u, (p, hh) in enumerate(heads):
        sc_scr[u] = _dot_nt(one_head(q_all[p], hh), kc[p])

    def row_body(it, carry):
        units = []
        for rr in range(NA_ROWS_PER_ITER):
            rq = it * NA_ROWS_PER_ITER + rr
            r = i * rows_per_step + rq
            rs = jnp.clip(r - win_rows // 2, 0, n_rows - win_rows)
            k0 = pl.multiple_of(rs * gw, gw)
            q0 = pl.multiple_of(rq * gw, gw)
            for p in pairs:
                qp = q_ref[0, pl.ds(q0, gw), ls[p]] * scale
                kw = k_ref[0, pl.ds(k0, win_rows * gw), ls[p]]
                vw = v_ref[0, pl.ds(k0, win_rows * gw), ls[p]]
                for hh in range(2):
                    units.append((q0, r - rs, p, hh, qp, kw, vw))
        n_u = range(len(units))
        s_w = [_dot_nt(one_head(qp, hh), kw) + tbl_ref[2 * p + hh, var] for q0, var, p, hh, qp, kw, vw in units]
        s_c = [sc_scr[2 * p + hh, pl.ds(q0, gw), :] for q0, var, p, hh, qp, kw, vw in units]
        m = [jnp.maximum(jnp.max(s_w[u], axis=-1, keepdims=True), jnp.max(s_c[u], axis=-1, keepdims=True))
             for u in n_u]
        p_w = [jnp.exp(s_w[u] - m[u]) for u in n_u]
        p_c = [jnp.exp(s_c[u] - m[u]) for u in n_u]
        inv = [1.0 / (jnp.sum(p_w[u], axis=-1, keepdims=True) + jnp.sum(p_c[u], axis=-1, keepdims=True))
               for u in n_u]
        o_w = [_dot(p_w[u].astype(BF16), units[u][6]) * inv[u] for u in n_u]
        for u in n_u:
            q0, var, p, hh = units[u][:4]
            pc_scr[2 * p + hh, pl.ds(q0, gw), :] = (p_c[u] * inv[u]).astype(BF16)
            if hh == 1:
                ow_scr[p, pl.ds(q0, gw), :] = jnp.where(first_head, o_w[u - 1], o_w[u])
        return carry

    lax.fori_loop(0, rows_per_step // NA_ROWS_PER_ITER, row_body, 0)

    for p in pairs:
        vc = vc_ref[0, :, ls[p]]
        oc = jnp.where(first_head, _dot(pc_scr[2 * p], vc), _dot(pc_scr[2 * p + 1], vc))
        o_ref[0, :, ls[p]] = (ow_scr[p] + oc).astype(BF16)


def _na(zna, zna_ctx, tbl, rows_per_step):
    bsz, seq, _ = zna.shape
    n_ctx = zna_ctx.shape[1]
    n_rows = seq // GRID_W
    win_rows = min(WIN_ROWS, n_rows)
    tq = rows_per_step * GRID_W
    assert rows_per_step % NA_ROWS_PER_ITER == 0
    return pl.pallas_call(
        functools.partial(_na_kernel, rows_per_step=rows_per_step, n_rows=n_rows, win_rows=win_rows),
        grid=(bsz, n_rows // rows_per_step),
        in_specs=[
            pl.BlockSpec((1, tq, NA_WIDTH), lambda b, i: (b, i, 0)),
            pl.BlockSpec((1, seq, NA_WIDTH), lambda b, i: (b, 0, 1)),
            pl.BlockSpec((1, seq, NA_WIDTH), lambda b, i: (b, 0, 2)),
            pl.BlockSpec((1, n_ctx, NA_WIDTH), lambda b, i: (b, 0, 1)),
            pl.BlockSpec((1, n_ctx, NA_WIDTH), lambda b, i: (b, 0, 2)),
            pl.BlockSpec(tbl.shape, lambda b, i: (0, 0, 0, 0)),
        ],
        out_specs=pl.BlockSpec((1, tq, NA_WIDTH), lambda b, i: (b, i, 0)),
        out_shape=jax.ShapeDtypeStruct((bsz, seq, NA_WIDTH), BF16),
        scratch_shapes=[pltpu.VMEM((NA_HEADS, tq, n_ctx), F32), pltpu.VMEM((NA_HEADS, tq, n_ctx), BF16),
                        pltpu.VMEM((NA_WIDTH // LANES, tq, LANES), F32)],
        compiler_params=_cparams(("parallel", "arbitrary")),
        name="nbr_attn",
    )(zna, zna, zna, zna_ctx, zna_ctx, tbl)


def _na_ctx_kernel(q_ref, k_ref, v_ref, o_ref):
    lane = lax.broadcasted_iota(jnp.int32, (1, LANES), 1)
    first_head = lane < HEAD_DIM
    scale = HEAD_DIM ** -0.5
    for p in range(NA_WIDTH // LANES):
        ls = slice(p * LANES, (p + 1) * LANES)
        qp = q_ref[0, :, ls] * scale
        kp = k_ref[0, :, ls]
        vp = v_ref[0, :, ls]
        outs = []
        for hh in range(2):
            msk = first_head if hh == 0 else jnp.logical_not(first_head)
            qm = jnp.where(msk, qp, jnp.zeros_like(qp[:1, :1]))
            s = _dot_nt(qm, kp)
            e = jnp.exp(s - jnp.max(s, axis=-1, keepdims=True))
            den = jnp.sum(e, axis=-1, keepdims=True)
            outs.append(_dot(e.astype(BF16), vp) / den)
        o_ref[0, :, ls] = jnp.where(first_head, outs[0], outs[1]).astype(BF16)


def _na_ctx(zna_ctx):
    bsz, n_ctx, _ = zna_ctx.shape
    col = lambda j: pl.BlockSpec((1, n_ctx, NA_WIDTH), lambda b: (b, 0, j))
    return pl.pallas_call(
        _na_ctx_kernel,
        grid=(bsz,),
        in_specs=[col(0), col(1), col(2)],
        out_specs=pl.BlockSpec((1, n_ctx, NA_WIDTH), lambda b: (b, 0, 0)),
        out_shape=jax.ShapeDtypeStruct((bsz, n_ctx, NA_WIDTH), BF16),
        compiler_params=_cparams(("parallel",)),
        name="ctx_attn",
    )(zna_ctx, zna_ctx, zna_ctx)


def _outproj_kernel(ya_ref, yb_ref, yn_ref, w_ref, x_ref, g_ref, lg_ref, lb_ref, o_ref, *, alpha):
    ycat = jnp.concatenate([ya_ref[0], yb_ref[0], yn_ref[0]], axis=1)
    y = _dot(ycat, w_ref[...])
    s = alpha * x_ref[0] + g_ref[0] * y
    o_ref[0] = _layer_norm_rows(s) * lg_ref[...] + lb_ref[...]


def _out_proj(ya, yb, yn, w_out_b, x, mod3, lg, lb, is_ctx, ctx_row, tm, alpha):
    bsz, seq, d = x.shape
    tok = lambda w: pl.BlockSpec((1, tm, w), lambda b, i: (b, i, 0))
    full = lambda a: pl.BlockSpec(a.shape, lambda b, i: (0,) * a.ndim)
    return pl.pallas_call(
        functools.partial(_outproj_kernel, alpha=alpha),
        grid=(bsz, seq // tm),
        in_specs=[tok(GMLP_WIDTH), tok(DN_WIDTH), tok(NA_WIDTH), full(w_out_b), tok(d),
                  _mod_spec(2, d, is_ctx, ctx_row), full(lg), full(lb)],
        out_specs=tok(d),
        out_shape=jax.ShapeDtypeStruct((bsz, seq, d), F32),
        compiler_params=_cparams(("parallel", "parallel")),
        name="out_proj",
    )(ya, yb, yn, w_out_b, x, mod3, lg, lb)


FFN_CHUNK = 256


def _ffn_kernel(x_ref, xp_ref, xn_ref, sh_ref, sc_ref, g_ref, wup_ref, cw_ref, cb_ref, wd_ref,
                lg_ref, lb_ref, o_ref, gated_ref, *, tm, n_tiles, d_ff, alpha):
    halo = 8
    i = pl.program_id(1)
    has_prev = (i > 0).astype(F32)
    has_next = (i < n_tiles - 1).astype(F32)
    x = x_ref[0]
    sc = 1.0 + sc_ref[0]
    sh = sh_ref[0]
    h_mid = _layer_norm_rows(x) * sc + sh
    h_prev = (_layer_norm_rows(xp_ref[0]) * sc + sh) * has_prev
    h_next = (_layer_norm_rows(xn_ref[0]) * sc + sh) * has_next
    h_ext = jnp.concatenate([h_prev, h_mid, h_next], axis=0).astype(BF16)
    h_b = h_mid.astype(BF16)
    n_ext = tm + 2 * halo
    for j in range(d_ff // FFN_CHUNK):
        cols = slice(j * FFN_CHUNK, (j + 1) * FFN_CHUNK)
        a = _dot(h_ext, wup_ref[:, cols])
        vv = _dot(h_b, wup_ref[:, d_ff + j * FFN_CHUNK:d_ff + (j + 1) * FFN_CHUNK])
        cw = cw_ref[:, cols]
        a_m1 = pltpu.roll(a, 1, 0)[halo:halo + tm]
        a_p1 = pltpu.roll(a, n_ext - 1, 0)[halo:halo + tm]
        conv = a_m1 * cw[0:1] + a[halo:halo + tm] * cw[1:2] + a_p1 * cw[2:3] + cb_ref[:, cols]
        gated_ref[:, cols] = (_silu(conv) * vv).astype(BF16)
    y = _dot(gated_ref[...], wd_ref[...])
    s = alpha * x + g_ref[0] * y
    o_ref[0] = _layer_norm_rows(s) * lg_ref[...] + lb_ref[...]


def _ffn(x, mod3, wup, cw, cb, wd, lg, lb, is_ctx, ctx_row, tm, alpha):
    bsz, seq, d = x.shape
    n_tiles = seq // tm
    d_ff = wd.shape[0]
    assert d_ff % FFN_CHUNK == 0
    hb = tm // 8
    tok = pl.BlockSpec((1, tm, d), lambda b, i: (b, i, 0))
    full = lambda a: pl.BlockSpec(a.shape, lambda b, i: (0,) * a.ndim)
    return pl.pallas_call(
        functools.partial(_ffn_kernel, tm=tm, n_tiles=n_tiles, d_ff=d_ff, alpha=alpha),
        grid=(bsz, n_tiles),
        in_specs=[
            tok,
            pl.BlockSpec((1, 8, d), lambda b, i: (b, jnp.maximum(i * hb - 1, 0), 0)),
            pl.BlockSpec((1, 8, d), lambda b, i: (b, jnp.minimum((i + 1) * hb, seq // 8 - 1), 0)),
            _mod_spec(3, d, is_ctx, ctx_row), _mod_spec(4, d, is_ctx, ctx_row), _mod_spec(5, d, is_ctx, ctx_row),
            full(wup), full(cw), full(cb), full(wd), full(lg), full(lb),
        ],
        out_specs=tok,
        out_shape=jax.ShapeDtypeStruct((bsz, seq, d), F32),
        scratch_shapes=[pltpu.VMEM((tm, d_ff), BF16)],
        compiler_params=_cparams(("parallel", "parallel")),
        name="conv_glu",
    )(x, x, x, mod3, mod3, mod3, wup, cw, cb, wd, lg, lb)


def _rope_tables(seq):
    half = HEAD_DIM // 2
    nf = half // 2
    pos = jnp.arange(seq)
    inv = ROPE_BASE ** (-jnp.arange(nf, dtype=F32) / nf)
    ang_r = (pos // GRID_W)[:, None].astype(F32) * inv
    ang_c = (pos % GRID_W)[:, None].astype(F32) * inv
    cos_h = jnp.concatenate([jnp.cos(ang_r)] * 2 + [jnp.cos(ang_c)] * 2, axis=1)
    sin_h = jnp.concatenate([-jnp.sin(ang_r), jnp.sin(ang_r), -jnp.sin(ang_c), jnp.sin(ang_c)], axis=1)
    reps = LANES // HEAD_DIM
    return jnp.tile(cos_h, (1, reps)), jnp.tile(sin_h, (1, reps))


def _bias_table(rpb, n_rows):
    wr = min(WIN_ROWS, n_rows)
    nh = rpb.shape[0]
    n_dr = 2 * WIN_ROWS - 1
    col = np.arange(GRID_W)
    cs = np.clip(col - WIN_COLS // 2, 0, GRID_W - WIN_COLS)
    valid = (col[None, :] >= cs[:, None]) & (col[None, :] < cs[:, None] + WIN_COLS)
    dc = col[None, :] - col[:, None] + (WIN_COLS - 1)
    onehot = np.zeros((2 * WIN_COLS - 1, GRID_W, GRID_W), np.float32)
    cc, kk = np.nonzero(valid)
    onehot[dc[cc, kk], cc, kk] = 1.0
    t = jnp.einsum('hdm,mck->hdck', rpb.astype(F32), jnp.asarray(onehot), precision=lax.Precision.HIGHEST)
    t = jnp.where(valid[None, None], t, NEG_BIG)
    per_var = []
    for var in range(wr):
        lo = WIN_ROWS - 1 - var
        per_var.append(t[:, lo:lo + wr].transpose(0, 2, 1, 3).reshape(nh, GRID_W, wr * GRID_W))
    return jnp.stack(per_var, axis=1)


def _gate_mats():
    expand = np.zeros((2, LANES, 2 * DN_WIDTH), np.float32)
    for dd in range(2):
        for h in range(DN_HEADS):
            for c in range(GATE_COPIES):
                expand[dd, c * GATE_COPY_STRIDE + dd * DN_HEADS + h, h * HEAD_DIM:(h + 1) * HEAD_DIM] = 1.0
            for c in range(2):
                src = c * GATE_COPY_STRIDE + 2 * DN_HEADS + dd * DN_HEADS + h
                expand[dd, src, DN_WIDTH + h * HEAD_DIM:DN_WIDTH + (h + 1) * HEAD_DIM] = 1.0
    return jnp.asarray(expand[0], BF16), jnp.asarray(expand[1], BF16)


def _gate_row(v):
    row = jnp.zeros((1, LANES), F32)
    for c in range(GATE_COPIES):
        row = row.at[0, c * GATE_COPY_STRIDE:c * GATE_COPY_STRIDE + v.shape[0]].set(v.astype(F32))
    return row


def _token_tile(seq):
    return 512 if seq % 512 == 0 else 256


def kernel(x, c, ctx, c_ctx, w_ada, b_ada, w_in, gmlp_ln_g, gmlp_ln_b, gmlp_ws, gmlp_bs, dn_conv, dn_a_log,
           dn_dt_bias, dn_norm_g, na_rpb, w_out, ln1_g, ln1_b, ffn_up, ffn_conv, ffn_conv_b, ffn_down,
           ln2_g, ln2_b):
    depth = w_ada.shape[0]
    bsz, seq, d = x.shape
    n_ctx = ctx.shape[1]
    d_ff = ffn_down.shape[1]
    alpha = (2 * depth) ** 0.25
    ctx_row = bsz
    tm = _token_tile(seq)
    tm_c = _token_tile(n_ctx)

    mod = _mod_all(c, c_ctx, w_ada, b_ada)
    rope_tabs = _rope_tables(seq)
    e_fwd, e_bwd = _gate_mats()
    gm256_mean = jnp.asarray(_block_diag_np(GMLP_HEADS, HEAD_DIM, HEAD_DIM, 1.0 / HEAD_DIM), BF16)
    gm384_sum = jnp.asarray(_block_diag_np(LANES // HEAD_DIM, HEAD_DIM, HEAD_DIM, 1.0), BF16)
    gm384_mean = jnp.asarray(_block_diag_np(LANES // HEAD_DIM, HEAD_DIM, HEAD_DIM, 1.0 / HEAD_DIM), BF16)
    assert bsz % 2 == 0, "the DeltaNet scan pairs batch elements"
    nb = 4 if bsz % 4 == 0 else 2
    s_zero = jnp.zeros((bsz // nb, nb + nb // 2, 4 * HEAD_DIM, 4 * HEAD_DIM), F32)
    o1 = GMLP_COLS
    o2 = o1 + DN_MAIN_COLS
    o3 = o2 + DN_AB_COLS

    for l in range(depth):
        ctx_out = l < depth - 1
        mod3 = mod[l].reshape(8, 1, 6 * d)
        wl = w_in[l]
        ab_cols = [wl[:, o2:o3], jnp.zeros((d, GATE_COPY_STRIDE - DN_AB_COLS), F32)] * GATE_COPIES
        ab_cols.append(jnp.zeros((d, LANES - GATE_COPIES * GATE_COPY_STRIDE), F32))
        w_in_p = jnp.concatenate([wl[:, :o2], wl[:, o3:]] + ab_cols, axis=1).astype(BF16)
        lng = gmlp_ln_g[l].reshape(1, GMLP_WIDTH)
        lnb = gmlp_ln_b[l].reshape(1, GMLP_WIDTH)
        wsp = gmlp_ws[l].transpose(1, 0, 2).reshape(GMLP_CHUNK, GMLP_HEADS * GMLP_CHUNK).astype(BF16)
        bsx = jnp.repeat(gmlp_bs[l].T, HEAD_DIM, axis=1)
        aexp_row = _gate_row(jnp.exp(dn_a_log[l].astype(F32)).reshape(-1))
        dtb_row = _gate_row(dn_dt_bias[l].reshape(-1))
        ng_row = jnp.tile(dn_norm_g[l].astype(F32), DN_HEADS).reshape(1, DN_WIDTH)
        tbl = _bias_table(na_rpb[l], seq // GRID_W)
        w_out_b = w_out[l].astype(BF16)
        lg1 = ln1_g[l].reshape(1, d)
        lb1 = ln1_b[l].reshape(1, d)
        lg2 = ln2_g[l].reshape(1, d)
        lb2 = ln2_b[l].reshape(1, d)
        wup = ffn_up[l].astype(BF16)
        wd = ffn_down[l].astype(BF16)
        cw = ffn_conv[l]
        cb = ffn_conv_b[l].reshape(1, d_ff)

        prep = (dn_conv[l], aexp_row, dtb_row, gm384_sum)
        zg_c, qkv_c, gate_c, gp_c, zna_c = _in_proj(ctx, mod3, w_in_p, *prep, None, True, ctx_row, tm_c)
        yb_c, s_ctx = _dn_scan(qkv_c, gate_c, gp_c, e_fwd, e_bwd, ng_row, gm384_mean, (s_zero, s_zero), nb)

        zg, qkv, gate, gp, zna = _in_proj(x, mod3, w_in_p, *prep, rope_tabs, False, ctx_row, tm)
        ya = _gmlp(zg, gm256_mean, lng, lnb, wsp, bsx, tm)
        yb, _ = _dn_scan(qkv, gate, gp, e_fwd, e_bwd, ng_row, gm384_mean, s_ctx, nb)
        yn = _na(zna, zna_c, tbl, 8)
        x1 = _out_proj(ya, yb, yn, w_out_b, x, mod3, lg1, lb1, False, ctx_row, tm, alpha)
        x_next = _ffn(x1, mod3, wup, cw, cb, wd, lg2, lb2, False, ctx_row, tm, alpha)

        if ctx_out:
            ya_c = _gmlp(zg_c, gm256_mean, lng, lnb, wsp, bsx, tm_c)
            yn_c = _na_ctx(zna_c)
            c1 = _out_proj(ya_c, yb_c, yn_c, w_out_b, ctx, mod3, lg1, lb1, True, ctx_row, tm_c, alpha)
            ctx = _ffn(c1, mod3, wup, cw, cb, wd, lg2, lb2, True, ctx_row, tm_c, alpha)
        x = x_next
    return x
```

```python
import functools
import math

import numpy as np
import jax
import jax.numpy as jnp
from jax import lax
from jax.experimental import pallas as pl
from jax.experimental.pallas import tpu as pltpu

F32 = jnp.float32
BF16 = jnp.bfloat16

HEAD_DIM = 64
GRID_W = 64
GMLP_HEADS = 4
GMLP_WIDTH = GMLP_HEADS * HEAD_DIM
GMLP_CHUNK = 128
DN_HEADS = 6
DN_WIDTH = DN_HEADS * HEAD_DIM
NA_HEADS = 6
NA_WIDTH = NA_HEADS * HEAD_DIM
DELTA_CHUNK = 64
WIN_ROWS = 8
WIN_COLS = 16
ROPE_BASE = 10000.0
EPS = 1e-6
NEG_BIG = -1e30
LANES = 128
VMEM_LIMIT = 56 * 1024 * 1024

GMLP_COLS = 2 * GMLP_WIDTH
DN_MAIN_COLS = 4 * DN_WIDTH
DN_AB_COLS = 4 * DN_HEADS
NA_COLS = 3 * NA_WIDTH


def _cparams(sem):
    return pltpu.CompilerParams(dimension_semantics=sem, vmem_limit_bytes=VMEM_LIMIT)


def _split2(x):
    hi = x.astype(BF16)
    lo = (x - hi.astype(F32)).astype(BF16)
    return hi, lo


def _split3(x):
    a = x.astype(BF16)
    r = x - a.astype(F32)
    b = r.astype(BF16)
    c = (r - b.astype(F32)).astype(BF16)
    return a, b, c


def _dot(a, b):
    return jnp.dot(a, b, preferred_element_type=F32)


def _dot_nt(a, b):
    return lax.dot_general(a, b, (((1,), (1,)), ((), ())), preferred_element_type=F32)


def _dot_tn(a, b):
    return lax.dot_general(a, b, (((0,), (0,)), ((), ())), preferred_element_type=F32)


def _dot_x3(x, m):
    a, b, c = _split3(x)
    return _dot(a, m) + _dot(b, m) + _dot(c, m)


def _dot_x2(x, m):
    a, b = _split2(x)
    return _dot(a, m) + _dot(b, m)


def _head_sums(x, pair_mat):
    parts = [_dot(x[:, p * LANES:(p + 1) * LANES].astype(BF16), pair_mat) for p in range(x.shape[1] // LANES)]
    return jnp.concatenate(parts, axis=1)


def _silu(x):
    return x * jax.nn.sigmoid(x)


def _layer_norm_rows(x):
    mu = jnp.mean(x, axis=-1, keepdims=True)
    xc = x - mu
    var = jnp.mean(xc * xc, axis=-1, keepdims=True)
    return xc * lax.rsqrt(var + EPS)


def _block_diag_np(n_blocks, rows, cols, value):
    m = np.zeros((n_blocks * rows, n_blocks * cols), np.float32)
    for g in range(n_blocks):
        m[g * rows:(g + 1) * rows, g * cols:(g + 1) * cols] = value
    return m


def _mod_kernel(c_ref, w_ref, b_ref, o_ref):
    a = _silu(c_ref[...])
    w = w_ref[0]
    a1, a2 = _split2(a)
    w1, w2 = _split2(w)
    o_ref[0] = _dot(a1, w1) + _dot(a1, w2) + _dot(a2, w1) + b_ref[0]


def _mod_all(c, c_ctx, w_ada, b_ada):
    n_layers, d, n6 = w_ada.shape
    bsz = c.shape[0]
    rows = jnp.concatenate([c, c_ctx[None, :], jnp.zeros((8 - bsz - 1, d), F32)], axis=0)
    tn = 1536
    return pl.pallas_call(
        _mod_kernel,
        grid=(n_layers, n6 // tn),
        in_specs=[
            pl.BlockSpec((8, d), lambda l, j: (0, 0)),
            pl.BlockSpec((1, d, tn), lambda l, j: (l, 0, j)),
            pl.BlockSpec((1, 1, tn), lambda l, j: (l, 0, j)),
        ],
        out_specs=pl.BlockSpec((1, 8, tn), lambda l, j: (l, 0, j)),
        out_shape=jax.ShapeDtypeStruct((n_layers, 8, n6), F32),
        compiler_params=_cparams(("parallel", "parallel")),
        name="adaln_mod",
    )(rows, w_ada, b_ada.reshape(n_layers, 1, n6))


def _mod_spec(chunk, d, is_ctx, ctx_row):
    if is_ctx:
        return pl.BlockSpec((1, 1, d), lambda b, i: (ctx_row, 0, chunk))
    return pl.BlockSpec((1, 1, d), lambda b, i: (b, 0, chunk))


def _inproj_kernel(*refs, tm, n_tiles, rotary):
    if rotary:
        (x_ref, xp_ref, xn_ref, sh_ref, sc_ref, w_ref, cw_ref, aexp_ref, dtb_ref, g_ref, gmm_ref, lng_ref, lnb_ref,
         wsp_ref, bsx_ref, cos_ref, sin_ref, ya_ref, qkv_ref, gate_ref, gp_ref, zna_ref) = refs
    else:
        (x_ref, xp_ref, xn_ref, sh_ref, sc_ref, w_ref, cw_ref, aexp_ref, dtb_ref, g_ref, gmm_ref, lng_ref, lnb_ref,
         wsp_ref, bsx_ref, ya_ref, qkv_ref, gate_ref, gp_ref, zna_ref) = refs
    halo = 8
    i = pl.program_id(1)
    has_prev = (i > 0).astype(F32)
    has_next = (i < n_tiles - 1).astype(F32)
    sc = 1.0 + sc_ref[0]
    sh = sh_ref[0]
    h_mid = _layer_norm_rows(x_ref[0]) * sc + sh
    h_prev = (_layer_norm_rows(xp_ref[0]) * sc + sh) * has_prev
    h_next = (_layer_norm_rows(xn_ref[0]) * sc + sh) * has_next
    h_ext = jnp.concatenate([h_prev, h_mid, h_next], axis=0).astype(BF16)
    h = h_mid.astype(BF16)
    o_q = GMLP_COLS
    o_g = o_q + 3 * DN_WIDTH
    o_n = o_g + DN_WIDTH
    o_ab = o_n + NA_COLS

    x_ext = _dot(h_ext, w_ref[:, o_q:o_g])
    ab = _dot(h, w_ref[:, o_ab:o_ab + LANES])
    n_ext = tm + 2 * halo
    x_m1 = pltpu.roll(x_ext, 1, 0)[halo:halo + tm]
    x_p1 = pltpu.roll(x_ext, n_ext - 1, 0)[halo:halo + tm]
    x_0 = x_ext[halo:halo + tm]
    cw = cw_ref[...]
    y = _silu(x_m1 * cw[0:1] + x_0 * cw[1:2] + x_p1 * cw[2:3])
    ya_ref[0] = _gmlp_mix(_dot(h, w_ref[:, 0:o_q]), gmm_ref[...], lng_ref[...], lnb_ref[...], wsp_ref[...],
                          bsx_ref[...]).astype(BF16)
    q = y[:, 0:DN_WIDTH]
    k = y[:, DN_WIDTH:2 * DN_WIDTH]
    v = y[:, 2 * DN_WIDTH:3 * DN_WIDTH]
    gmat = g_ref[...]
    q = q * lax.rsqrt(_head_sums(q * q, gmat) + EPS)
    k = k * lax.rsqrt(_head_sums(k * k, gmat) + EPS)
    gate_ref[0] = _dot(h, w_ref[:, o_g:o_n]).astype(BF16)
    if rotary:
        cos2 = cos_ref[...]
        sin2 = sin_ref[...]
        lane = lax.broadcasted_iota(jnp.int32, (tm, LANES), 1)
        first = (lane % 32) < 16

        def rope(t):
            parts = []
            for p in range(DN_WIDTH // LANES):
                tp = t[:, p * LANES:(p + 1) * LANES]
                partner = jnp.where(first, pltpu.roll(tp, LANES - 16, 1), pltpu.roll(tp, 16, 1))
                parts.append(tp * cos2 + partner * sin2)
            return jnp.concatenate(parts, axis=1)

        q = rope(q)
        k = rope(k)
    q = q * (HEAD_DIM ** -0.5)
    qkv_ref[0, :, 0:DN_WIDTH] = q.astype(BF16)
    qkv_ref[0, :, DN_WIDTH:2 * DN_WIDTH] = k.astype(BF16)
    qkv_ref[0, :, 2 * DN_WIDTH:3 * DN_WIDTH] = v.astype(BF16)
    zna_ref[0] = _dot(h, w_ref[:, o_n:o_ab]).astype(BF16)

    lane_ab = lax.broadcasted_iota(jnp.int32, ab.shape, 1)
    in_copy = lane_ab % GATE_COPY_STRIDE
    copy = lane_ab // GATE_COPY_STRIDE
    xs = ab + dtb_ref[...]
    softplus = jnp.maximum(xs, 0.0) + jnp.log1p(jnp.exp(-jnp.abs(xs)))
    log_decay = -aexp_ref[...] * softplus
    beta = jax.nn.sigmoid(ab)
    blk = min(tm, 4 * DELTA_CHUNK)
    ti = lax.broadcasted_iota(jnp.int32, (blk, blk), 0)
    tj = lax.broadcasted_iota(jnp.int32, (blk, blk), 1)
    same_chunk = (ti // DELTA_CHUNK) == (tj // DELTA_CHUNK)
    l_fwd = jnp.where(same_chunk, jnp.where(ti >= tj, 1.0, 0.0), 0.0).astype(BF16)
    l_bwd = jnp.where(same_chunk, jnp.where(ti <= tj, 1.0, 0.0), 0.0).astype(BF16)
    g1, g2 = _split2(log_decay)
    gc_f, gc_b = [], []
    for rb in range(tm // blk):
        rows = slice(rb * blk, (rb + 1) * blk)
        gc_f.append(_dot(l_fwd, g1[rows]) + _dot(l_fwd, g2[rows]))
        gc_b.append(_dot(l_bwd, g1[rows]) + _dot(l_bwd, g2[rows]))
    gc = jnp.where(in_copy < DN_HEADS, jnp.concatenate(gc_f, axis=0), jnp.concatenate(gc_b, axis=0))
    c1, c2, c3 = _split3(gc)
    b1, b2 = _split2(beta)
    dec_piece = jnp.where(copy == 0, c1, jnp.where(copy == 1, c2, c3)).astype(F32)
    beta_piece = jnp.where(copy == 0, b1, jnp.where(copy == 1, b2, jnp.zeros_like(b2))).astype(F32)
    packed = jnp.where(in_copy < 2 * DN_HEADS, dec_piece, beta_piece)
    gp_ref[0] = packed.astype(BF16)


def _in_proj(x, mod3, w_in_p, conv_w, aexp_row, dtb_row, gm384, gmlp_args, rope_tabs, is_ctx, ctx_row, tm):
    bsz, seq, d = x.shape
    n_tiles = seq // tm
    hb = tm // 8
    rotary = rope_tabs is not None
    tok = lambda w: pl.BlockSpec((1, tm, w), lambda b, i: (b, i, 0))
    full = lambda a: pl.BlockSpec(a.shape, lambda b, i: (0,) * a.ndim)
    in_specs = [
        tok(d),
        pl.BlockSpec((1, 8, d), lambda b, i: (b, jnp.maximum(i * hb - 1, 0), 0)),
        pl.BlockSpec((1, 8, d), lambda b, i: (b, jnp.minimum((i + 1) * hb, seq // 8 - 1), 0)),
        _mod_spec(0, d, is_ctx, ctx_row),
        _mod_spec(1, d, is_ctx, ctx_row),
        full(w_in_p), full(conv_w), full(aexp_row), full(dtb_row), full(gm384),
    ] + [full(a) for a in gmlp_args]
    args = [x, x, x, mod3, mod3, w_in_p, conv_w, aexp_row, dtb_row, gm384] + list(gmlp_args)
    if rotary:
        in_specs += [pl.BlockSpec((tm, LANES), lambda b, i: (i, 0))] * 2
        args += list(rope_tabs)
    widths = [GMLP_WIDTH, 3 * DN_WIDTH, DN_WIDTH, LANES, NA_COLS]
    return pl.pallas_call(
        functools.partial(_inproj_kernel, tm=tm, n_tiles=n_tiles, rotary=rotary),
        grid=(bsz, n_tiles),
        in_specs=in_specs,
        out_specs=[tok(w) for w in widths],
        out_shape=[jax.ShapeDtypeStruct((bsz, seq, w), BF16) for w in widths],
        compiler_params=_cparams(("parallel", "parallel")),
        name="in_proj",
    )(*args)


def _gmlp_mix(z, gmat, lng, lnb, wsp, bsx):
    tm = z.shape[0]
    z = jax.nn.gelu(z)
    u = z[:, :GMLP_WIDTH]
    v = z[:, GMLP_WIDTH:]
    vc = v - _head_sums(v, gmat)
    var = _head_sums(vc * vc, gmat)
    vn = (vc * lax.rsqrt(var + EPS) * lng + lnb).astype(BF16)
    nrow = GMLP_HEADS * GMLP_CHUNK
    ri = lax.broadcasted_iota(jnp.int32, (nrow, GMLP_WIDTH), 0) // GMLP_CHUNK
    ci = lax.broadcasted_iota(jnp.int32, (nrow, GMLP_WIDTH), 1) // HEAD_DIM
    diag = ri == ci
    outs = []
    for n in range(tm // GMLP_CHUNK):
        rows = slice(n * GMLP_CHUNK, (n + 1) * GMLP_CHUNK)
        vch = vn[rows]
        bd = jnp.where(diag, jnp.concatenate([vch] * GMLP_HEADS, axis=0), jnp.zeros_like(vch[:1, :1]))
        outs.append(u[rows] * (_dot(wsp, bd) + bsx))
    return jnp.concatenate(outs, axis=0)


GATE_COPIES = 3
GATE_COPY_STRIDE = 32
INV_BLOCK = 8
DN_CHUNKS_PER_STEP = 4


def _dn_intra(q, k, v, gcx, bx, masks):
    cs = DELTA_CHUNK
    causal, strict, eye, last_row, diag, diag_mul, inv_diag, inv_levels = masks
    grp = range(len(q))

    def bd(x):
        return jnp.concatenate([x.astype(BF16)] * 4, axis=0) * diag_mul

    qf = [q[g].astype(F32) for g in grp]
    kf = [k[g].astype(F32) for g in grp]
    vf = [v[g].astype(F32) for g in grp]
    gc_t = [jnp.sum(jnp.where(eye, gcx[g], 0.0), axis=0, keepdims=True) for g in grp]
    dm = [jnp.where(causal, jnp.exp(jnp.where(causal, gcx[g] - gc_t[g], 0.0)), 0.0) for g in grp]
    dms = [jnp.where(strict, dm[g], 0.0) for g in grp]
    gl = [jnp.sum(jnp.where(last_row, gcx[g], 0.0), axis=0, keepdims=True) for g in grp]
    egx = [jnp.exp(gcx[g]) for g in grp]
    kdx = [jnp.exp(gl[g] - gcx[g]) for g in grp]
    kb = [kf[g] * bx[g] for g in grp]
    vb = [vf[g] * bx[g] for g in grp]
    bk = [bd(k[g]) for g in grp]
    lhs = [jnp.concatenate([kb[g].astype(BF16), q[g]], axis=0) for g in grp]
    p1 = [_dot_nt(lhs[g], bk[g]) for g in grp]
    a_mat = [p1[g][:cs] * dms[g] for g in grp]
    attn = [p1[g][cs:] * dm[g] for g in grp]

    eye_f = jnp.where(eye, 1.0, 0.0)
    a_d = [a_mat[g] * inv_diag for g in grp]
    t_mat = [eye_f - a_d[g] for g in grp]
    b_pow = [_dot(a_d[g].astype(BF16), bd(a_d[g])) for g in grp]
    pp = [_dot(jnp.concatenate([t_mat[g], b_pow[g]], axis=0).astype(BF16), bd(b_pow[g])) for g in grp]
    t_mat = [t_mat[g] + pp[g][:cs] for g in grp]
    t_mat = [t_mat[g] + _dot(t_mat[g].astype(BF16), bd(pp[g][cs:])) for g in grp]
    for coupling in inv_levels:
        x = [_dot(t_mat[g].astype(BF16), bd(a_mat[g] * coupling)) for g in grp]
        t_mat = [t_mat[g] - _dot(x[g].astype(BF16), bd(t_mat[g])) for g in grp]

    tb = [t_mat[g].astype(BF16) for g in grp]
    u = [_dot(tb[g], bd(vb[g])) for g in grp]
    wk = [_dot(tb[g], bd(kb[g] * egx[g])) for g in grp]
    lhs2 = [jnp.concatenate([wk[g], qf[g] * egx[g]], axis=0).astype(BF16) for g in grp]
    kd = [(kf[g] * kdx[g]).astype(BF16) for g in grp]
    egl = [jnp.exp(gl[g]) for g in grp]
    attn_b = [attn[g].astype(BF16) for g in grp]
    return u, lhs2, attn_b, kd, egl


def _dn_inter(intra, st, masks):
    cs = DELTA_CHUNK
    diag, diag_mul = masks[4], masks[5]
    u, lhs2, attn_b, kd, egl = intra
    grp = range(len(u))
    p2 = [_dot(lhs2[g], st[g].astype(BF16)) for g in grp]
    v_new = [u[g] - p2[g][:cs] for g in grp]
    vnb = [v_new[g].astype(BF16) for g in grp]
    bdv = [jnp.concatenate([vnb[g]] * 4, axis=0) * diag_mul for g in grp]
    o = [p2[g][cs:] + _dot(attn_b[g], bdv[g]) for g in grp]
    st_new = [st[g] * egl[g] + jnp.where(diag, _dot_tn(kd[g], vnb[g]), 0.0) for g in grp]
    return o, st_new


def _dn_masks(reverse):
    cs = DELTA_CHUNK
    wg = 4 * HEAD_DIM
    i3 = lax.broadcasted_iota(jnp.int32, (cs, wg), 0)
    j3 = lax.broadcasted_iota(jnp.int32, (cs, wg), 1) % HEAD_DIM
    if reverse:
        causal, strict, last_row = i3 <= j3, i3 < j3, i3 == 0
    else:
        causal, strict, last_row = i3 >= j3, i3 > j3, i3 == cs - 1
    ra = lax.broadcasted_iota(jnp.int32, (wg, wg), 0) // HEAD_DIM
    ca = lax.broadcasted_iota(jnp.int32, (wg, wg), 1) // HEAD_DIM
    diag = ra == ca
    diag_mul = jnp.where(diag, 1.0, 0.0).astype(BF16)
    inv_diag = jnp.where(i3 // INV_BLOCK == j3 // INV_BLOCK, 1.0, 0.0)
    inv_levels = []
    size = INV_BLOCK
    while size < cs:
        same_pair = i3 // (2 * size) == j3 // (2 * size)
        inv_levels.append(jnp.where(same_pair, jnp.where(i3 // size == j3 // size, 0.0, 1.0), 0.0))
        size *= 2
    return causal, strict, i3 == j3, last_row, diag, diag_mul, inv_diag, inv_levels


def _dn_scan_kernel(*refs, n_steps, nb, cps, reverse, finish):
    if finish:
        (qkv_ref, gp_ref, e_ref, s0_ref, of_ref, gate_ref, ng_ref, gm_ref, out_ref, s_out_ref, st_scr) = refs
    else:
        (qkv_ref, gp_ref, e_ref, s0_ref, out_ref, s_out_ref, st_scr) = refs
    w = DN_WIDTH
    wg = 4 * HEAD_DIM
    wr = w - wg
    n_grp = nb + nb // 2
    i = pl.program_id(1)

    @pl.when(i == 0)
    def _():
        st_scr[...] = s0_ref[...]

    masks = _dn_masks(reverse)
    e_comb = e_ref[...]
    cs = DELTA_CHUNK
    order = list(range(cps - 1, -1, -1)) if reverse else list(range(cps))

    def groups(arrs, off):
        out = [a[:, off:off + wg] for a in arrs]
        for p in range(nb // 2):
            out.append(jnp.concatenate([arrs[2 * p][:, off + wg:off + w], arrs[2 * p + 1][:, off + wg:off + w]],
                                       axis=1))
        return out

    qs, ks, vs, gs, bs = [], [], [], [], []
    step_rows = cps * cs
    ex_all = _dot(gp_ref[...].reshape(nb * step_rows, LANES), e_comb)
    for c in order:
        rows = slice(c * cs, (c + 1) * cs)
        qkv = [qkv_ref[bb, rows, :] for bb in range(nb)]
        ex = [ex_all[bb * step_rows + c * cs:bb * step_rows + (c + 1) * cs] for bb in range(nb)]
        qs += groups(qkv, 0)
        ks += groups(qkv, w)
        vs += groups(qkv, 2 * w)
        gs += groups(ex, 0)
        bs += groups(ex, w)
    intra = _dn_intra(qs, ks, vs, gs, bs, masks)
    st = [st_scr[g] for g in range(n_grp)]
    for idx, c in enumerate(order):
        rows = slice(c * cs, (c + 1) * cs)
        part = [t[idx * n_grp:(idx + 1) * n_grp] for t in intra]
        o_g, st = _dn_inter(part, st, masks)
        for bb in range(nb):
            rest = o_g[nb + bb // 2][:, (bb % 2) * wr:(bb % 2 + 1) * wr]
            o = jnp.concatenate([o_g[bb], rest], axis=1)
            if finish:
                ot = of_ref[bb, rows, :] + o
                ms = _head_sums(ot * ot, gm_ref[...])
                gate = gate_ref[bb, rows, :].astype(F32)
                out_ref[bb, rows, :] = (ot * lax.rsqrt(ms + EPS) * ng_ref[...] * _silu(gate)).astype(BF16)
            else:
                out_ref[bb, rows, :] = o
    for g in range(n_grp):
        st_scr[g] = st[g]

    @pl.when(i == n_steps - 1)
    def _():
        s_out_ref[...] = st_scr[...]


def _dn_scan_dir(qkv, gp, e_comb, s0, reverse, finish_args, nb):
    bsz, seq, _ = qkv.shape
    cps = DN_CHUNKS_PER_STEP
    cs = cps * DELTA_CHUNK
    n = seq // cs
    wg = 4 * HEAD_DIM
    n_grp = nb + nb // 2
    finish = finish_args is not None
    cidx = (lambda i: n - 1 - i) if reverse else (lambda i: i)
    full = lambda a: pl.BlockSpec(a.shape, lambda g, i: (0,) * a.ndim)
    tok = lambda wdt, col=0: pl.BlockSpec((nb, cs, wdt), lambda g, i: (g, cidx(i), col))
    st = pl.BlockSpec((None, n_grp, wg, wg), lambda g, i: (g, 0, 0, 0))
    in_specs = [tok(3 * DN_WIDTH), tok(LANES), full(e_comb), st]
    args = [qkv, gp, e_comb, s0]
    if finish:
        o_other, gate, ng_row, gm_mean = finish_args
        in_specs += [tok(DN_WIDTH), tok(DN_WIDTH), full(ng_row), full(gm_mean)]
        args += [o_other, gate, ng_row, gm_mean]
    return pl.pallas_call(
        functools.partial(_dn_scan_kernel, n_steps=n, nb=nb, cps=cps, reverse=reverse, finish=finish),
        grid=(bsz // nb, n),
        in_specs=in_specs,
        out_specs=[tok(DN_WIDTH), st],
        out_shape=[
            jax.ShapeDtypeStruct((bsz, seq, DN_WIDTH), BF16 if finish else F32),
            jax.ShapeDtypeStruct((bsz // nb, n_grp, wg, wg), F32),
        ],
        scratch_shapes=[pltpu.VMEM((n_grp, wg, wg), F32)],
        compiler_params=_cparams(("parallel", "arbitrary")),
        name="dn_scan_bwd" if reverse else "dn_scan_fwd",
    )(*args)


def _dn_scan(qkv, gate, gp, e_fwd, e_bwd, ng_row, gm_mean, s0, nb):
    o_f, s_f = _dn_scan_dir(qkv, gp, e_fwd, s0[0], False, None, nb)
    y, s_b = _dn_scan_dir(qkv, gp, e_bwd, s0[1], True, (o_f, gate, ng_row, gm_mean), nb)
    return y, (s_f, s_b)


NA_ROWS_PER_ITER = 2


def _na_kernel(q_ref, k_ref, v_ref, kc_ref, vc_ref, tbl_ref, o_ref, sc_scr, pc_scr, ow_scr, *,
               rows_per_step, n_rows, win_rows):
    i = pl.program_id(1)
    gw = GRID_W
    lane = lax.broadcasted_iota(jnp.int32, (1, LANES), 1)
    first_head = lane < HEAD_DIM
    scale = HEAD_DIM ** -0.5
    pairs = range(NA_WIDTH // LANES)
    heads = [(p, hh) for p in pairs for hh in range(2)]
    ls = [slice(p * LANES, (p + 1) * LANES) for p in pairs]
    zero = jnp.zeros((1, 1), BF16)

    def one_head(x, hh):
        return jnp.where(first_head if hh == 0 else jnp.logical_not(first_head), x, zero)

    kc = [kc_ref[0, :, ls[p]] for p in pairs]
    q_all = [q_ref[0, :, ls[p]] * scale for p in pairs]
    for u, (p, hh) in enumerate(heads):
        sc_scr[u] = _dot_nt(one_head(q_all[p], hh), kc[p])

    def row_body(it, carry):
        units = []
        for rr in range(NA_ROWS_PER_ITER):
            rq = it * NA_ROWS_PER_ITER + rr
            r = i * rows_per_step + rq
            rs = jnp.clip(r - win_rows // 2, 0, n_rows - win_rows)
            k0 = pl.multiple_of(rs * gw, gw)
            q0 = pl.multiple_of(rq * gw, gw)
            for p in pairs:
                qp = q_ref[0, pl.ds(q0, gw), ls[p]] * scale
                kw = k_ref[0, pl.ds(k0, win_rows * gw), ls[p]]
                vw = v_ref[0, pl.ds(k0, win_rows * gw), ls[p]]
                for hh in range(2):
                    units.append((q0, r - rs, p, hh, qp, kw, vw))
        n_u = range(len(units))
        s_w = [_dot_nt(one_head(qp, hh), kw) + tbl_ref[2 * p + hh, var] for q0, var, p, hh, qp, kw, vw in units]
        s_c = [sc_scr[2 * p + hh, pl.ds(q0, gw), :] for q0, var, p, hh, qp, kw, vw in units]
        m = [jnp.maximum(jnp.max(s_w[u], axis=-1, keepdims=True), jnp.max(s_c[u], axis=-1, keepdims=True))
             for u in n_u]
        p_w = [jnp.exp(s_w[u] - m[u]) for u in n_u]
        p_c = [jnp.exp(s_c[u] - m[u]) for u in n_u]
        inv = [1.0 / (jnp.sum(p_w[u], axis=-1, keepdims=True) + jnp.sum(p_c[u], axis=-1, keepdims=True))
               for u in n_u]
        o_w = [_dot(p_w[u].astype(BF16), units[u][6]) * inv[u] for u in n_u]
        for u in n_u:
            q0, var, p, hh = units[u][:4]
            pc_scr[2 * p + hh, pl.ds(q0, gw), :] = (p_c[u] * inv[u]).astype(BF16)
            if hh == 1:
                ow_scr[p, pl.ds(q0, gw), :] = jnp.where(first_head, o_w[u - 1], o_w[u])
        return carry

    lax.fori_loop(0, rows_per_step // NA_ROWS_PER_ITER, row_body, 0)

    for p in pairs:
        vc = vc_ref[0, :, ls[p]]
        oc = jnp.where(first_head, _dot(pc_scr[2 * p], vc), _dot(pc_scr[2 * p + 1], vc))
        o_ref[0, :, ls[p]] = (ow_scr[p] + oc).astype(BF16)


def _na(zna, zna_ctx, tbl, rows_per_step):
    bsz, seq, _ = zna.shape
    n_ctx = zna_ctx.shape[1]
    n_rows = seq // GRID_W
    win_rows = min(WIN_ROWS, n_rows)
    tq = rows_per_step * GRID_W
    assert rows_per_step % NA_ROWS_PER_ITER == 0
    return pl.pallas_call(
        functools.partial(_na_kernel, rows_per_step=rows_per_step, n_rows=n_rows, win_rows=win_rows),
        grid=(bsz, n_rows // rows_per_step),
        in_specs=[
            pl.BlockSpec((1, tq, NA_WIDTH), lambda b, i: (b, i, 0)),
            pl.BlockSpec((1, seq, NA_WIDTH), lambda b, i: (b, 0, 1)),
            pl.BlockSpec((1, seq, NA_WIDTH), lambda b, i: (b, 0, 2)),
            pl.BlockSpec((1, n_ctx, NA_WIDTH), lambda b, i: (b, 0, 1)),
            pl.BlockSpec((1, n_ctx, NA_WIDTH), lambda b, i: (b, 0, 2)),
            pl.BlockSpec(tbl.shape, lambda b, i: (0, 0, 0, 0)),
        ],
        out_specs=pl.BlockSpec((1, tq, NA_WIDTH), lambda b, i: (b, i, 0)),
        out_shape=jax.ShapeDtypeStruct((bsz, seq, NA_WIDTH), BF16),
        scratch_shapes=[pltpu.VMEM((NA_HEADS, tq, n_ctx), F32), pltpu.VMEM((NA_HEADS, tq, n_ctx), BF16),
                        pltpu.VMEM((NA_WIDTH // LANES, tq, LANES), F32)],
        compiler_params=_cparams(("parallel", "arbitrary")),
        name="nbr_attn",
    )(zna, zna, zna, zna_ctx, zna_ctx, tbl)


def _na_ctx_kernel(q_ref, k_ref, v_ref, o_ref):
    lane = lax.broadcasted_iota(jnp.int32, (1, LANES), 1)
    first_head = lane < HEAD_DIM
    scale = HEAD_DIM ** -0.5
    for p in range(NA_WIDTH // LANES):
        ls = slice(p * LANES, (p + 1) * LANES)
        qp = q_ref[0, :, ls] * scale
        kp = k_ref[0, :, ls]
        vp = v_ref[0, :, ls]
        outs = []
        for hh in range(2):
            msk = first_head if hh == 0 else jnp.logical_not(first_head)
            qm = jnp.where(msk, qp, jnp.zeros_like(qp[:1, :1]))
            s = _dot_nt(qm, kp)
            e = jnp.exp(s - jnp.max(s, axis=-1, keepdims=True))
            den = jnp.sum(e, axis=-1, keepdims=True)
            outs.append(_dot(e.astype(BF16), vp) / den)
        o_ref[0, :, ls] = jnp.where(first_head, outs[0], outs[1]).astype(BF16)


def _na_ctx(zna_ctx):
    bsz, n_ctx, _ = zna_ctx.shape
    col = lambda j: pl.BlockSpec((1, n_ctx, NA_WIDTH), lambda b: (b, 0, j))
    return pl.pallas_call(
        _na_ctx_kernel,
        grid=(bsz,),
        in_specs=[col(0), col(1), col(2)],
        out_specs=pl.BlockSpec((1, n_ctx, NA_WIDTH), lambda b: (b, 0, 0)),
        out_shape=jax.ShapeDtypeStruct((bsz, n_ctx, NA_WIDTH), BF16),
        compiler_params=_cparams(("parallel",)),
        name="ctx_attn",
    )(zna_ctx, zna_ctx, zna_ctx)


def _outproj_kernel(ya_ref, yb_ref, yn_ref, w_ref, x_ref, g_ref, lg_ref, lb_ref, o_ref, *, alpha):
    ycat = jnp.concatenate([ya_ref[0], yb_ref[0], yn_ref[0]], axis=1)
    y = _dot(ycat, w_ref[...])
    s = alpha * x_ref[0] + g_ref[0] * y
    o_ref[0] = _layer_norm_rows(s) * lg_ref[...] + lb_ref[...]


def _out_proj(ya, yb, yn, w_out_b, x, mod3, lg, lb, is_ctx, ctx_row, tm, alpha):
    bsz, seq, d = x.shape
    tok = lambda w: pl.BlockSpec((1, tm, w), lambda b, i: (b, i, 0))
    full = lambda a: pl.BlockSpec(a.shape, lambda b, i: (0,) * a.ndim)
    return pl.pallas_call(
        functools.partial(_outproj_kernel, alpha=alpha),
        grid=(bsz, seq // tm),
        in_specs=[tok(GMLP_WIDTH), tok(DN_WIDTH), tok(NA_WIDTH), full(w_out_b), tok(d),
                  _mod_spec(2, d, is_ctx, ctx_row), full(lg), full(lb)],
        out_specs=tok(d),
        out_shape=jax.ShapeDtypeStruct((bsz, seq, d), F32),
        compiler_params=_cparams(("parallel", "parallel")),
        name="out_proj",
    )(ya, yb, yn, w_out_b, x, mod3, lg, lb)


FFN_CHUNK = 256


def _ffn_kernel(x_ref, xp_ref, xn_ref, sh_ref, sc_ref, g_ref, wup_ref, cw_ref, cb_ref, wd_ref,
                lg_ref, lb_ref, o_ref, gated_ref, *, tm, n_tiles, d_ff, alpha):
    halo = 8
    i = pl.program_id(1)
    has_prev = (i > 0).astype(F32)
    has_next = (i < n_tiles - 1).astype(F32)
    x = x_ref[0]
    sc = 1.0 + sc_ref[0]
    sh = sh_ref[0]
    h_mid = _layer_norm_rows(x) * sc + sh
    h_prev = (_layer_norm_rows(xp_ref[0]) * sc + sh) * has_prev
    h_next = (_layer_norm_rows(xn_ref[0]) * sc + sh) * has_next
    h_ext = jnp.concatenate([h_prev, h_mid, h_next], axis=0).astype(BF16)
    h_b = h_mid.astype(BF16)
    n_ext = tm + 2 * halo
    for j in range(d_ff // FFN_CHUNK):
        cols = slice(j * FFN_CHUNK, (j + 1) * FFN_CHUNK)
        a = _dot(h_ext, wup_ref[:, cols])
        vv = _dot(h_b, wup_ref[:, d_ff + j * FFN_CHUNK:d_ff + (j + 1) * FFN_CHUNK])
        cw = cw_ref[:, cols]
        a_m1 = pltpu.roll(a, 1, 0)[halo:halo + tm]
        a_p1 = pltpu.roll(a, n_ext - 1, 0)[halo:halo + tm]
        conv = a_m1 * cw[0:1] + a[halo:halo + tm] * cw[1:2] + a_p1 * cw[2:3] + cb_ref[:, cols]
        gated_ref[:, cols] = (_silu(conv) * vv).astype(BF16)
    y = _dot(gated_ref[...], wd_ref[...])
    s = alpha * x + g_ref[0] * y
    o_ref[0] = _layer_norm_rows(s) * lg_ref[...] + lb_ref[...]


def _ffn(x, mod3, wup, cw, cb, wd, lg, lb, is_ctx, ctx_row, tm, alpha):
    bsz, seq, d = x.shape
    n_tiles = seq // tm
    d_ff = wd.shape[0]
    assert d_ff % FFN_CHUNK == 0
    hb = tm // 8
    tok = pl.BlockSpec((1, tm, d), lambda b, i: (b, i, 0))
    full = lambda a: pl.BlockSpec(a.shape, lambda b, i: (0,) * a.ndim)
    return pl.pallas_call(
        functools.partial(_ffn_kernel, tm=tm, n_tiles=n_tiles, d_ff=d_ff, alpha=alpha),
        grid=(bsz, n_tiles),
        in_specs=[
            tok,
            pl.BlockSpec((1, 8, d), lambda b, i: (b, jnp.maximum(i * hb - 1, 0), 0)),
            pl.BlockSpec((1, 8, d), lambda b, i: (b, jnp.minimum((i + 1) * hb, seq // 8 - 1), 0)),
            _mod_spec(3, d, is_ctx, ctx_row), _mod_spec(4, d, is_ctx, ctx_row), _mod_spec(5, d, is_ctx, ctx_row),
            full(wup), full(cw), full(cb), full(wd), full(lg), full(lb),
        ],
        out_specs=tok,
        out_shape=jax.ShapeDtypeStruct((bsz, seq, d), F32),
        scratch_shapes=[pltpu.VMEM((tm, d_ff), BF16)],
        compiler_params=_cparams(("parallel", "parallel")),
        name="conv_glu",
    )(x, x, x, mod3, mod3, mod3, wup, cw, cb, wd, lg, lb)


def _rope_tables(seq):
    half = HEAD_DIM // 2
    nf = half // 2
    pos = jnp.arange(seq)
    inv = ROPE_BASE ** (-jnp.arange(nf, dtype=F32) / nf)
    ang_r = (pos // GRID_W)[:, None].astype(F32) * inv
    ang_c = (pos % GRID_W)[:, None].astype(F32) * inv
    cos_h = jnp.concatenate([jnp.cos(ang_r)] * 2 + [jnp.cos(ang_c)] * 2, axis=1)
    sin_h = jnp.concatenate([-jnp.sin(ang_r), jnp.sin(ang_r), -jnp.sin(ang_c), jnp.sin(ang_c)], axis=1)
    reps = LANES // HEAD_DIM
    return jnp.tile(cos_h, (1, reps)), jnp.tile(sin_h, (1, reps))


def _bias_table(rpb, n_rows):
    wr = min(WIN_ROWS, n_rows)
    nh = rpb.shape[0]
    n_dr = 2 * WIN_ROWS - 1
    col = np.arange(GRID_W)
    cs = np.clip(col - WIN_COLS // 2, 0, GRID_W - WIN_COLS)
    valid = (col[None, :] >= cs[:, None]) & (col[None, :] < cs[:, None] + WIN_COLS)
    dc = col[None, :] - col[:, None] + (WIN_COLS - 1)
    onehot = np.zeros((2 * WIN_COLS - 1, GRID_W, GRID_W), np.float32)
    cc, kk = np.nonzero(valid)
    onehot[dc[cc, kk], cc, kk] = 1.0
    t = jnp.einsum('hdm,mck->hdck', rpb.astype(F32), jnp.asarray(onehot), precision=lax.Precision.HIGHEST)
    t = jnp.where(valid[None, None], t, NEG_BIG)
    per_var = []
    for var in range(wr):
        lo = WIN_ROWS - 1 - var
        per_var.append(t[:, lo:lo + wr].transpose(0, 2, 1, 3).reshape(nh, GRID_W, wr * GRID_W))
    return jnp.stack(per_var, axis=1)


def _gate_mats():
    expand = np.zeros((2, LANES, 2 * DN_WIDTH), np.float32)
    for dd in range(2):
        for h in range(DN_HEADS):
            for c in range(GATE_COPIES):
                expand[dd, c * GATE_COPY_STRIDE + dd * DN_HEADS + h, h * HEAD_DIM:(h + 1) * HEAD_DIM] = 1.0
            for c in range(2):
                src = c * GATE_COPY_STRIDE + 2 * DN_HEADS + dd * DN_HEADS + h
                expand[dd, src, DN_WIDTH + h * HEAD_DIM:DN_WIDTH + (h + 1) * HEAD_DIM] = 1.0
    return jnp.asarray(expand[0], BF16), jnp.asarray(expand[1], BF16)


def _gate_row(v):
    row = jnp.zeros((1, LANES), F32)
    for c in range(GATE_COPIES):
        row = row.at[0, c * GATE_COPY_STRIDE:c * GATE_COPY_STRIDE + v.shape[0]].set(v.astype(F32))
    return row


def _token_tile(seq):
    return 512 if seq % 512 == 0 else 256


def kernel(x, c, ctx, c_ctx, w_ada, b_ada, w_in, gmlp_ln_g, gmlp_ln_b, gmlp_ws, gmlp_bs, dn_conv, dn_a_log,
           dn_dt_bias, dn_norm_g, na_rpb, w_out, ln1_g, ln1_b, ffn_up, ffn_conv, ffn_conv_b, ffn_down,
           ln2_g, ln2_b):
    depth = w_ada.shape[0]
    bsz, seq, d = x.shape
    n_ctx = ctx.shape[1]
    d_ff = ffn_down.shape[1]
    alpha = (2 * depth) ** 0.25
    ctx_row = bsz
    tm = _token_tile(seq)
    tm_c = _token_tile(n_ctx)

    mod = _mod_all(c, c_ctx, w_ada, b_ada)
    rope_tabs = _rope_tables(seq)
    e_fwd, e_bwd = _gate_mats()
    gm384_sum = jnp.asarray(_block_diag_np(LANES // HEAD_DIM, HEAD_DIM, HEAD_DIM, 1.0), BF16)
    gm384_mean = jnp.asarray(_block_diag_np(LANES // HEAD_DIM, HEAD_DIM, HEAD_DIM, 1.0 / HEAD_DIM), BF16)
    assert bsz % 2 == 0, "the DeltaNet scan pairs batch elements"
    nb = 4 if bsz % 4 == 0 else 2
    s_zero = jnp.zeros((bsz // nb, nb + nb // 2, 4 * HEAD_DIM, 4 * HEAD_DIM), F32)
    o1 = GMLP_COLS
    o2 = o1 + DN_MAIN_COLS
    o3 = o2 + DN_AB_COLS

    for l in range(depth):
        ctx_out = l < depth - 1
        mod3 = mod[l].reshape(8, 1, 6 * d)
        wl = w_in[l]
        ab_cols = [wl[:, o2:o3], jnp.zeros((d, GATE_COPY_STRIDE - DN_AB_COLS), F32)] * GATE_COPIES
        ab_cols.append(jnp.zeros((d, LANES - GATE_COPIES * GATE_COPY_STRIDE), F32))
        w_in_p = jnp.concatenate([wl[:, :o2], wl[:, o3:]] + ab_cols, axis=1).astype(BF16)
        lng = gmlp_ln_g[l].reshape(1, GMLP_WIDTH)
        lnb = gmlp_ln_b[l].reshape(1, GMLP_WIDTH)
        wsp = gmlp_ws[l].transpose(1, 0, 2).reshape(GMLP_CHUNK, GMLP_HEADS * GMLP_CHUNK).astype(BF16)
        bsx = jnp.repeat(gmlp_bs[l].T, HEAD_DIM, axis=1)
        aexp_row = _gate_row(jnp.exp(dn_a_log[l].astype(F32)).reshape(-1))
        dtb_row = _gate_row(dn_dt_bias[l].reshape(-1))
        ng_row = jnp.tile(dn_norm_g[l].astype(F32), DN_HEADS).reshape(1, DN_WIDTH)
        tbl = _bias_table(na_rpb[l], seq // GRID_W)
        w_out_b = w_out[l].astype(BF16)
        lg1 = ln1_g[l].reshape(1, d)
        lb1 = ln1_b[l].reshape(1, d)
        lg2 = ln2_g[l].reshape(1, d)
        lb2 = ln2_b[l].reshape(1, d)
        wup = ffn_up[l].astype(BF16)
        wd = ffn_down[l].astype(BF16)
        cw = ffn_conv[l]
        cb = ffn_conv_b[l].reshape(1, d_ff)

        prep = (dn_conv[l], aexp_row, dtb_row, gm384_sum, (gm384_mean, lng, lnb, wsp, bsx))
        ya_c, qkv_c, gate_c, gp_c, zna_c = _in_proj(ctx, mod3, w_in_p, *prep, None, True, ctx_row, tm_c)
        yb_c, s_ctx = _dn_scan(qkv_c, gate_c, gp_c, e_fwd, e_bwd, ng_row, gm384_mean, (s_zero, s_zero), nb)

        ya, qkv, gate, gp, zna = _in_proj(x, mod3, w_in_p, *prep, rope_tabs, False, ctx_row, tm)
        yb, _ = _dn_scan(qkv, gate, gp, e_fwd, e_bwd, ng_row, gm384_mean, s_ctx, nb)
        yn = _na(zna, zna_c, tbl, 8)
        x1 = _out_proj(ya, yb, yn, w_out_b, x, mod3, lg1, lb1, False, ctx_row, tm, alpha)
        x_next = _ffn(x1, mod3, wup, cw, cb, wd, lg2, lb2, False, ctx_row, tm, alpha)

        if ctx_out:
            yn_c = _na_ctx(zna_c)
            c1 = _out_proj(ya_c, yb_c, yn_c, w_out_b, ctx, mod3, lg1, lb1, True, ctx_row, tm_c, alpha)
            ctx = _ffn(c1, mod3, wup, cw, cb, wd, lg2, lb2, True, ctx_row, tm_c, alpha)
        x = x_next
    return x
```

```python
import functools
import math

import numpy as np
import jax
import jax.numpy as jnp
from jax import lax
from jax.experimental import pallas as pl
from jax.experimental.pallas import tpu as pltpu

F32 = jnp.float32
BF16 = jnp.bfloat16

HEAD_DIM = 64
GRID_W = 64
GMLP_HEADS = 4
GMLP_WIDTH = GMLP_HEADS * HEAD_DIM
GMLP_CHUNK = 128
DN_HEADS = 6
DN_WIDTH = DN_HEADS * HEAD_DIM
NA_HEADS = 6
NA_WIDTH = NA_HEADS * HEAD_DIM
DELTA_CHUNK = 64
WIN_ROWS = 8
WIN_COLS = 16
ROPE_BASE = 10000.0
EPS = 1e-6
NEG_BIG = -1e30
LANES = 128
VMEM_LIMIT = 56 * 1024 * 1024

GMLP_COLS = 2 * GMLP_WIDTH
DN_MAIN_COLS = 4 * DN_WIDTH
DN_AB_COLS = 4 * DN_HEADS
NA_COLS = 3 * NA_WIDTH


def _cparams(sem):
    return pltpu.CompilerParams(dimension_semantics=sem, vmem_limit_bytes=VMEM_LIMIT)


def _split2(x):
    hi = x.astype(BF16)
    lo = (x - hi.astype(F32)).astype(BF16)
    return hi, lo


def _split3(x):
    a = x.astype(BF16)
    r = x - a.astype(F32)
    b = r.astype(BF16)
    c = (r - b.astype(F32)).astype(BF16)
    return a, b, c


def _dot(a, b):
    return jnp.dot(a, b, preferred_element_type=F32)


def _dot_nt(a, b):
    return lax.dot_general(a, b, (((1,), (1,)), ((), ())), preferred_element_type=F32)


def _dot_tn(a, b):
    return lax.dot_general(a, b, (((0,), (0,)), ((), ())), preferred_element_type=F32)


def _dot_x3(x, m):
    a, b, c = _split3(x)
    return _dot(a, m) + _dot(b, m) + _dot(c, m)


def _dot_x2(x, m):
    a, b = _split2(x)
    return _dot(a, m) + _dot(b, m)


def _head_sums(x, pair_mat):
    parts = [_dot(x[:, p * LANES:(p + 1) * LANES].astype(BF16), pair_mat) for p in range(x.shape[1] // LANES)]
    return jnp.concatenate(parts, axis=1)


def _silu(x):
    return x * jax.nn.sigmoid(x)


def _layer_norm_rows(x):
    mu = jnp.mean(x, axis=-1, keepdims=True)
    xc = x - mu
    var = jnp.mean(xc * xc, axis=-1, keepdims=True)
    return xc * lax.rsqrt(var + EPS)


def _block_diag_np(n_blocks, rows, cols, value):
    m = np.zeros((n_blocks * rows, n_blocks * cols), np.float32)
    for g in range(n_blocks):
        m[g * rows:(g + 1) * rows, g * cols:(g + 1) * cols] = value
    return m


def _mod_kernel(c_ref, w_ref, b_ref, o_ref):
    a = _silu(c_ref[...])
    w = w_ref[0]
    a1, a2 = _split2(a)
    w1, w2 = _split2(w)
    o_ref[0] = _dot(a1, w1) + _dot(a1, w2) + _dot(a2, w1) + b_ref[0]


def _mod_all(c, c_ctx, w_ada, b_ada):
    n_layers, d, n6 = w_ada.shape
    bsz = c.shape[0]
    rows = jnp.concatenate([c, c_ctx[None, :], jnp.zeros((8 - bsz - 1, d), F32)], axis=0)
    tn = 1536
    return pl.pallas_call(
        _mod_kernel,
        grid=(n_layers, n6 // tn),
        in_specs=[
            pl.BlockSpec((8, d), lambda l, j: (0, 0)),
            pl.BlockSpec((1, d, tn), lambda l, j: (l, 0, j)),
            pl.BlockSpec((1, 1, tn), lambda l, j: (l, 0, j)),
        ],
        out_specs=pl.BlockSpec((1, 8, tn), lambda l, j: (l, 0, j)),
        out_shape=jax.ShapeDtypeStruct((n_layers, 8, n6), F32),
        compiler_params=_cparams(("parallel", "parallel")),
        name="adaln_mod",
    )(rows, w_ada, b_ada.reshape(n_layers, 1, n6))


def _mod_spec(chunk, d, is_ctx, ctx_row):
    if is_ctx:
        return pl.BlockSpec((1, 1, d), lambda b, i: (ctx_row, 0, chunk))
    return pl.BlockSpec((1, 1, d), lambda b, i: (b, 0, chunk))


def _inproj_kernel(*refs, tm, n_tiles, rotary):
    if rotary:
        (x_ref, xp_ref, xn_ref, sh_ref, sc_ref, w_ref, cw_ref, aexp_ref, dtb_ref, g_ref, gmm_ref, lng_ref, lnb_ref,
         wsp_ref, bsx_ref, cos_ref, sin_ref, ya_ref, qkv_ref, gate_ref, gp_ref, zna_ref) = refs
    else:
        (x_ref, xp_ref, xn_ref, sh_ref, sc_ref, w_ref, cw_ref, aexp_ref, dtb_ref, g_ref, gmm_ref, lng_ref, lnb_ref,
         wsp_ref, bsx_ref, ya_ref, qkv_ref, gate_ref, gp_ref, zna_ref) = refs
    halo = 8
    i = pl.program_id(1)
    has_prev = (i > 0).astype(F32)
    has_next = (i < n_tiles - 1).astype(F32)
    sc = 1.0 + sc_ref[0]
    sh = sh_ref[0]
    h_mid = _layer_norm_rows(x_ref[0]) * sc + sh
    h_prev = (_layer_norm_rows(xp_ref[0]) * sc + sh) * has_prev
    h_next = (_layer_norm_rows(xn_ref[0]) * sc + sh) * has_next
    h_ext = jnp.concatenate([h_prev, h_mid, h_next], axis=0).astype(BF16)
    h = h_mid.astype(BF16)
    o_q = GMLP_COLS
    o_g = o_q + 3 * DN_WIDTH
    o_n = o_g + DN_WIDTH
    o_ab = o_n + NA_COLS

    x_ext = _dot(h_ext, w_ref[:, o_q:o_g])
    ab = _dot(h, w_ref[:, o_ab:o_ab + LANES])
    n_ext = tm + 2 * halo
    x_m1 = pltpu.roll(x_ext, 1, 0)[halo:halo + tm]
    x_p1 = pltpu.roll(x_ext, n_ext - 1, 0)[halo:halo + tm]
    x_0 = x_ext[halo:halo + tm]
    cw = cw_ref[...]
    y = _silu(x_m1 * cw[0:1] + x_0 * cw[1:2] + x_p1 * cw[2:3])
    ya_ref[0] = _gmlp_mix(_dot(h, w_ref[:, 0:o_q]), gmm_ref[...], lng_ref[...], lnb_ref[...], wsp_ref[...],
                          bsx_ref[...]).astype(BF16)
    q = y[:, 0:DN_WIDTH]
    k = y[:, DN_WIDTH:2 * DN_WIDTH]
    v = y[:, 2 * DN_WIDTH:3 * DN_WIDTH]
    gmat = g_ref[...]
    q = q * lax.rsqrt(_head_sums(q * q, gmat) + EPS)
    k = k * lax.rsqrt(_head_sums(k * k, gmat) + EPS)
    gate_ref[0] = _dot(h, w_ref[:, o_g:o_n]).astype(BF16)
    if rotary:
        cos2 = cos_ref[...]
        sin2 = sin_ref[...]
        lane = lax.broadcasted_iota(jnp.int32, (tm, LANES), 1)
        first = (lane % 32) < 16

        def rope(t):
            parts = []
            for p in range(DN_WIDTH // LANES):
                tp = t[:, p * LANES:(p + 1) * LANES]
                partner = jnp.where(first, pltpu.roll(tp, LANES - 16, 1), pltpu.roll(tp, 16, 1))
                parts.append(tp * cos2 + partner * sin2)
            return jnp.concatenate(parts, axis=1)

        q = rope(q)
        k = rope(k)
    q = q * (HEAD_DIM ** -0.5)
    qkv_ref[0, :, 0:DN_WIDTH] = q.astype(BF16)
    qkv_ref[0, :, DN_WIDTH:2 * DN_WIDTH] = k.astype(BF16)
    qkv_ref[0, :, 2 * DN_WIDTH:3 * DN_WIDTH] = v.astype(BF16)
    zna_ref[0] = _dot(h, w_ref[:, o_n:o_ab]).astype(BF16)

    lane_ab = lax.broadcasted_iota(jnp.int32, ab.shape, 1)
    in_copy = lane_ab % GATE_COPY_STRIDE
    copy = lane_ab // GATE_COPY_STRIDE
    xs = ab + dtb_ref[...]
    softplus = jnp.maximum(xs, 0.0) + jnp.log1p(jnp.exp(-jnp.abs(xs)))
    log_decay = -aexp_ref[...] * softplus
    beta = jax.nn.sigmoid(ab)
    blk = min(tm, 4 * DELTA_CHUNK)
    ti = lax.broadcasted_iota(jnp.int32, (blk, blk), 0)
    tj = lax.broadcasted_iota(jnp.int32, (blk, blk), 1)
    same_chunk = (ti // DELTA_CHUNK) == (tj // DELTA_CHUNK)
    l_fwd = jnp.where(same_chunk, jnp.where(ti >= tj, 1.0, 0.0), 0.0).astype(BF16)
    l_bwd = jnp.where(same_chunk, jnp.where(ti <= tj, 1.0, 0.0), 0.0).astype(BF16)
    g1, g2 = _split2(log_decay)
    gc_f, gc_b = [], []
    for rb in range(tm // blk):
        rows = slice(rb * blk, (rb + 1) * blk)
        gc_f.append(_dot(l_fwd, g1[rows]) + _dot(l_fwd, g2[rows]))
        gc_b.append(_dot(l_bwd, g1[rows]) + _dot(l_bwd, g2[rows]))
    gc = jnp.where(in_copy < DN_HEADS, jnp.concatenate(gc_f, axis=0), jnp.concatenate(gc_b, axis=0))
    c1, c2, c3 = _split3(gc)
    b1, b2 = _split2(beta)
    dec_piece = jnp.where(copy == 0, c1, jnp.where(copy == 1, c2, c3)).astype(F32)
    beta_piece = jnp.where(copy == 0, b1, jnp.where(copy == 1, b2, jnp.zeros_like(b2))).astype(F32)
    packed = jnp.where(in_copy < 2 * DN_HEADS, dec_piece, beta_piece)
    gp_ref[0] = packed.astype(BF16)


def _in_proj(x, mod3, w_in_p, conv_w, aexp_row, dtb_row, gm384, gmlp_args, rope_tabs, is_ctx, ctx_row, tm):
    bsz, seq, d = x.shape
    n_tiles = seq // tm
    hb = tm // 8
    rotary = rope_tabs is not None
    tok = lambda w: pl.BlockSpec((1, tm, w), lambda b, i: (b, i, 0))
    full = lambda a: pl.BlockSpec(a.shape, lambda b, i: (0,) * a.ndim)
    in_specs = [
        tok(d),
        pl.BlockSpec((1, 8, d), lambda b, i: (b, jnp.maximum(i * hb - 1, 0), 0)),
        pl.BlockSpec((1, 8, d), lambda b, i: (b, jnp.minimum((i + 1) * hb, seq // 8 - 1), 0)),
        _mod_spec(0, d, is_ctx, ctx_row),
        _mod_spec(1, d, is_ctx, ctx_row),
        full(w_in_p), full(conv_w), full(aexp_row), full(dtb_row), full(gm384),
    ] + [full(a) for a in gmlp_args]
    args = [x, x, x, mod3, mod3, w_in_p, conv_w, aexp_row, dtb_row, gm384] + list(gmlp_args)
    if rotary:
        in_specs += [pl.BlockSpec((tm, LANES), lambda b, i: (i, 0))] * 2
        args += list(rope_tabs)
    widths = [GMLP_WIDTH, 3 * DN_WIDTH, DN_WIDTH, LANES, NA_COLS]
    return pl.pallas_call(
        functools.partial(_inproj_kernel, tm=tm, n_tiles=n_tiles, rotary=rotary),
        grid=(bsz, n_tiles),
        in_specs=in_specs,
        out_specs=[tok(w) for w in widths],
        out_shape=[jax.ShapeDtypeStruct((bsz, seq, w), BF16) for w in widths],
        compiler_params=_cparams(("parallel", "parallel")),
        name="in_proj",
    )(*args)


def _gmlp_mix(z, gmat, lng, lnb, wsp, bsx):
    tm = z.shape[0]
    z = jax.nn.gelu(z)
    u = z[:, :GMLP_WIDTH]
    v = z[:, GMLP_WIDTH:]
    vc = v - _head_sums(v, gmat)
    var = _head_sums(vc * vc, gmat)
    vn = (vc * lax.rsqrt(var + EPS) * lng + lnb).astype(BF16)
    nrow = GMLP_HEADS * GMLP_CHUNK
    ri = lax.broadcasted_iota(jnp.int32, (nrow, GMLP_WIDTH), 0) // GMLP_CHUNK
    ci = lax.broadcasted_iota(jnp.int32, (nrow, GMLP_WIDTH), 1) // HEAD_DIM
    diag = ri == ci
    outs = []
    for n in range(tm // GMLP_CHUNK):
        rows = slice(n * GMLP_CHUNK, (n + 1) * GMLP_CHUNK)
        vch = vn[rows]
        bd = jnp.where(diag, jnp.concatenate([vch] * GMLP_HEADS, axis=0), jnp.zeros_like(vch[:1, :1]))
        outs.append(u[rows] * (_dot(wsp, bd) + bsx))
    return jnp.concatenate(outs, axis=0)


GATE_COPIES = 3
GATE_COPY_STRIDE = 32
INV_BLOCK = 8
DN_GROUP_HEADS = 2
DN_CHUNKS_PER_STEP = 8


def _dn_intra(q, k, v, gcx, bx, masks):
    cs = DELTA_CHUNK
    causal, strict, eye, last_row, diag, diag_mul, inv_diag, inv_levels = masks
    grp = range(len(q))

    def bd(x):
        return jnp.concatenate([x.astype(BF16)] * DN_GROUP_HEADS, axis=0) * diag_mul

    qf = [q[g].astype(F32) for g in grp]
    kf = [k[g].astype(F32) for g in grp]
    vf = [v[g].astype(F32) for g in grp]
    gc_t = [jnp.sum(jnp.where(eye, gcx[g], 0.0), axis=0, keepdims=True) for g in grp]
    dm = [jnp.where(causal, jnp.exp(jnp.where(causal, gcx[g] - gc_t[g], 0.0)), 0.0) for g in grp]
    dms = [jnp.where(strict, dm[g], 0.0) for g in grp]
    gl = [jnp.sum(jnp.where(last_row, gcx[g], 0.0), axis=0, keepdims=True) for g in grp]
    egx = [jnp.exp(gcx[g]) for g in grp]
    kdx = [jnp.exp(gl[g] - gcx[g]) for g in grp]
    kb = [kf[g] * bx[g] for g in grp]
    vb = [vf[g] * bx[g] for g in grp]
    bk = [bd(k[g]) for g in grp]
    lhs = [jnp.concatenate([kb[g].astype(BF16), q[g]], axis=0) for g in grp]
    p1 = [_dot_nt(lhs[g], bk[g]) for g in grp]
    a_mat = [p1[g][:cs] * dms[g] for g in grp]
    attn = [p1[g][cs:] * dm[g] for g in grp]

    eye_f = jnp.where(eye, 1.0, 0.0)
    a_d = [a_mat[g] * inv_diag for g in grp]
    t_mat = [eye_f - a_d[g] for g in grp]
    b_pow = [_dot(a_d[g].astype(BF16), bd(a_d[g])) for g in grp]
    pp = [_dot(jnp.concatenate([t_mat[g], b_pow[g]], axis=0).astype(BF16), bd(b_pow[g])) for g in grp]
    t_mat = [t_mat[g] + pp[g][:cs] for g in grp]
    t_mat = [t_mat[g] + _dot(t_mat[g].astype(BF16), bd(pp[g][cs:])) for g in grp]
    for coupling in inv_levels:
        x = [_dot(t_mat[g].astype(BF16), bd(a_mat[g] * coupling)) for g in grp]
        t_mat = [t_mat[g] - _dot(x[g].astype(BF16), bd(t_mat[g])) for g in grp]

    tb = [t_mat[g].astype(BF16) for g in grp]
    u = [_dot(tb[g], bd(vb[g])) for g in grp]
    wk = [_dot(tb[g], bd(kb[g] * egx[g])) for g in grp]
    lhs2 = [jnp.concatenate([wk[g], qf[g] * egx[g]], axis=0).astype(BF16) for g in grp]
    kd = [(kf[g] * kdx[g]).astype(BF16) for g in grp]
    egl = [jnp.exp(gl[g]) for g in grp]
    attn_b = [attn[g].astype(BF16) for g in grp]
    return u, lhs2, attn_b, kd, egl


def _dn_inter(intra, st, masks):
    cs = DELTA_CHUNK
    diag, diag_mul = masks[4], masks[5]
    u, lhs2, attn_b, kd, egl = intra
    grp = range(len(u))
    p2 = [_dot(lhs2[g], st[g].astype(BF16)) for g in grp]
    v_new = [u[g] - p2[g][:cs] for g in grp]
    vnb = [v_new[g].astype(BF16) for g in grp]
    bdv = [jnp.concatenate([vnb[g]] * DN_GROUP_HEADS, axis=0) * diag_mul for g in grp]
    o = [p2[g][cs:] + _dot(attn_b[g], bdv[g]) for g in grp]
    st_new = [st[g] * egl[g] + jnp.where(diag, _dot_tn(kd[g], vnb[g]), 0.0) for g in grp]
    return o, st_new


def _dn_masks(reverse):
    cs = DELTA_CHUNK
    wg = DN_GROUP_HEADS * HEAD_DIM
    i3 = lax.broadcasted_iota(jnp.int32, (cs, wg), 0)
    j3 = lax.broadcasted_iota(jnp.int32, (cs, wg), 1) % HEAD_DIM
    if reverse:
        causal, strict, last_row = i3 <= j3, i3 < j3, i3 == 0
    else:
        causal, strict, last_row = i3 >= j3, i3 > j3, i3 == cs - 1
    ra = lax.broadcasted_iota(jnp.int32, (wg, wg), 0) // HEAD_DIM
    ca = lax.broadcasted_iota(jnp.int32, (wg, wg), 1) // HEAD_DIM
    diag = ra == ca
    diag_mul = jnp.where(diag, 1.0, 0.0).astype(BF16)
    inv_diag = jnp.where(i3 // INV_BLOCK == j3 // INV_BLOCK, 1.0, 0.0)
    inv_levels = []
    size = INV_BLOCK
    while size < cs:
        same_pair = i3 // (2 * size) == j3 // (2 * size)
        inv_levels.append(jnp.where(same_pair, jnp.where(i3 // size == j3 // size, 0.0, 1.0), 0.0))
        size *= 2
    return causal, strict, i3 == j3, last_row, diag, diag_mul, inv_diag, inv_levels


def _dn_scan_kernel(*refs, n_steps, nb, cps, reverse, finish):
    if finish:
        (qkv_ref, gp_ref, e_ref, s0_ref, of_ref, gate_ref, ng_ref, gm_ref, out_ref, s_out_ref, st_scr) = refs
    else:
        (qkv_ref, gp_ref, e_ref, s0_ref, out_ref, s_out_ref, st_scr) = refs
    w = DN_WIDTH
    wg = DN_GROUP_HEADS * HEAD_DIM
    gpc = w // wg
    n_grp = nb * gpc
    i = pl.program_id(1)

    @pl.when(i == 0)
    def _():
        st_scr[...] = s0_ref[...]

    masks = _dn_masks(reverse)
    e_comb = e_ref[...]
    cs = DELTA_CHUNK
    order = list(range(cps - 1, -1, -1)) if reverse else list(range(cps))

    def groups(arrs, off):
        return [a[:, off + p * wg:off + (p + 1) * wg] for a in arrs for p in range(gpc)]

    qs, ks, vs, gs, bs = [], [], [], [], []
    step_rows = cps * cs
    ex_all = _dot(gp_ref[...].reshape(nb * step_rows, LANES), e_comb)
    for c in order:
        rows = slice(c * cs, (c + 1) * cs)
        qkv = [qkv_ref[bb, rows, :] for bb in range(nb)]
        ex = [ex_all[bb * step_rows + c * cs:bb * step_rows + (c + 1) * cs] for bb in range(nb)]
        qs += groups(qkv, 0)
        ks += groups(qkv, w)
        vs += groups(qkv, 2 * w)
        gs += groups(ex, 0)
        bs += groups(ex, w)
    intra = _dn_intra(qs, ks, vs, gs, bs, masks)
    st = [st_scr[g] for g in range(n_grp)]
    for idx, c in enumerate(order):
        rows = slice(c * cs, (c + 1) * cs)
        part = [t[idx * n_grp:(idx + 1) * n_grp] for t in intra]
        o_g, st = _dn_inter(part, st, masks)
        for bb in range(nb):
            o = jnp.concatenate(o_g[bb * gpc:(bb + 1) * gpc], axis=1)
            if finish:
                ot = of_ref[bb, rows, :] + o
                ms = _head_sums(ot * ot, gm_ref[...])
                gate = gate_ref[bb, rows, :].astype(F32)
                out_ref[bb, rows, :] = (ot * lax.rsqrt(ms + EPS) * ng_ref[...] * _silu(gate)).astype(BF16)
            else:
                out_ref[bb, rows, :] = o
    for g in range(n_grp):
        st_scr[g] = st[g]

    @pl.when(i == n_steps - 1)
    def _():
        s_out_ref[...] = st_scr[...]


def _dn_scan_dir(qkv, gp, e_comb, s0, reverse, finish_args, nb):
    bsz, seq, _ = qkv.shape
    cps = min(DN_CHUNKS_PER_STEP, seq // DELTA_CHUNK)
    cs = cps * DELTA_CHUNK
    assert seq % cs == 0
    n = seq // cs
    wg = DN_GROUP_HEADS * HEAD_DIM
    n_grp = nb * (DN_WIDTH // wg)
    finish = finish_args is not None
    cidx = (lambda i: n - 1 - i) if reverse else (lambda i: i)
    full = lambda a: pl.BlockSpec(a.shape, lambda g, i: (0,) * a.ndim)
    tok = lambda wdt, col=0: pl.BlockSpec((nb, cs, wdt), lambda g, i: (g, cidx(i), col))
    st = pl.BlockSpec((None, n_grp, wg, wg), lambda g, i: (g, 0, 0, 0))
    in_specs = [tok(3 * DN_WIDTH), tok(LANES), full(e_comb), st]
    args = [qkv, gp, e_comb, s0]
    if finish:
        o_other, gate, ng_row, gm_mean = finish_args
        in_specs += [tok(DN_WIDTH), tok(DN_WIDTH), full(ng_row), full(gm_mean)]
        args += [o_other, gate, ng_row, gm_mean]
    return pl.pallas_call(
        functools.partial(_dn_scan_kernel, n_steps=n, nb=nb, cps=cps, reverse=reverse, finish=finish),
        grid=(bsz // nb, n),
        in_specs=in_specs,
        out_specs=[tok(DN_WIDTH), st],
        out_shape=[
            jax.ShapeDtypeStruct((bsz, seq, DN_WIDTH), BF16 if finish else F32),
            jax.ShapeDtypeStruct((bsz // nb, n_grp, wg, wg), F32),
        ],
        scratch_shapes=[pltpu.VMEM((n_grp, wg, wg), F32)],
        compiler_params=_cparams(("parallel", "arbitrary")),
        name="dn_scan_bwd" if reverse else "dn_scan_fwd",
    )(*args)


def _dn_scan(qkv, gate, gp, e_fwd, e_bwd, ng_row, gm_mean, s0, nb):
    o_f, s_f = _dn_scan_dir(qkv, gp, e_fwd, s0[0], False, None, nb)
    y, s_b = _dn_scan_dir(qkv, gp, e_bwd, s0[1], True, (o_f, gate, ng_row, gm_mean), nb)
    return y, (s_f, s_b)


NA_ROWS_PER_ITER = 2


def _na_kernel(q_ref, k_ref, v_ref, kc_ref, vc_ref, tbl_ref, o_ref, sc_scr, pc_scr, ow_scr, *,
               rows_per_step, n_rows, win_rows):
    i = pl.program_id(1)
    gw = GRID_W
    lane = lax.broadcasted_iota(jnp.int32, (1, LANES), 1)
    first_head = lane < HEAD_DIM
    scale = HEAD_DIM ** -0.5
    pairs = range(NA_WIDTH // LANES)
    heads = [(p, hh) for p in pairs for hh in range(2)]
    ls = [slice(p * LANES, (p + 1) * LANES) for p in pairs]
    zero = jnp.zeros((1, 1), BF16)

    def one_head(x, hh):
        return jnp.where(first_head if hh == 0 else jnp.logical_not(first_head), x, zero)

    kc = [kc_ref[0, :, ls[p]] for p in pairs]
    q_all = [q_ref[0, :, ls[p]] * scale for p in pairs]
    for u, (p, hh) in enumerate(heads):
        sc_scr[u] = _dot_nt(one_head(q_all[p], hh), kc[p])

    def row_body(it, carry):
        units = []
        for rr in range(NA_ROWS_PER_ITER):
            rq = it * NA_ROWS_PER_ITER + rr
            r = i * rows_per_step + rq
            rs = jnp.clip(r - win_rows // 2, 0, n_rows - win_rows)
            k0 = pl.multiple_of(rs * gw, gw)
            q0 = pl.multiple_of(rq * gw, gw)
            for p in pairs:
                qp = q_ref[0, pl.ds(q0, gw), ls[p]] * scale
                kw = k_ref[0, pl.ds(k0, win_rows * gw), ls[p]]
                vw = v_ref[0, pl.ds(k0, win_rows * gw), ls[p]]
                units.append((q0, r - rs, p, jnp.concatenate([one_head(qp, 0), one_head(qp, 1)], axis=0), kw, vw))
        n_u = range(len(units))
        s_w = [_dot_nt(q2, kw) + tbl_ref[p, var] for q0, var, p, q2, kw, vw in units]
        s_c = [jnp.concatenate([sc_scr[2 * p, pl.ds(q0, gw), :], sc_scr[2 * p + 1, pl.ds(q0, gw), :]], axis=0)
               for q0, var, p, q2, kw, vw in units]
        m = [jnp.maximum(jnp.max(s_w[u], axis=-1, keepdims=True), jnp.max(s_c[u], axis=-1, keepdims=True))
             for u in n_u]
        p_w = [jnp.exp(s_w[u] - m[u]) for u in n_u]
        p_c = [jnp.exp(s_c[u] - m[u]) for u in n_u]
        inv = [1.0 / (jnp.sum(p_w[u], axis=-1, keepdims=True) + jnp.sum(p_c[u], axis=-1, keepdims=True))
               for u in n_u]
        o_w = [_dot(p_w[u].astype(BF16), units[u][5]) * inv[u] for u in n_u]
        for u in n_u:
            q0, var, p = units[u][:3]
            pcn = (p_c[u] * inv[u]).astype(BF16)
            pc_scr[2 * p, pl.ds(q0, gw), :] = pcn[:gw]
            pc_scr[2 * p + 1, pl.ds(q0, gw), :] = pcn[gw:]
            ow_scr[p, pl.ds(q0, gw), :] = jnp.where(first_head, o_w[u][:gw], o_w[u][gw:])
        return carry

    lax.fori_loop(0, rows_per_step // NA_ROWS_PER_ITER, row_body, 0)

    for p in pairs:
        vc = vc_ref[0, :, ls[p]]
        oc = jnp.where(first_head, _dot(pc_scr[2 * p], vc), _dot(pc_scr[2 * p + 1], vc))
        o_ref[0, :, ls[p]] = (ow_scr[p] + oc).astype(BF16)


def _na(zna, zna_ctx, tbl, rows_per_step):
    bsz, seq, _ = zna.shape
    n_ctx = zna_ctx.shape[1]
    n_rows = seq // GRID_W
    win_rows = min(WIN_ROWS, n_rows)
    tq = rows_per_step * GRID_W
    assert rows_per_step % NA_ROWS_PER_ITER == 0
    return pl.pallas_call(
        functools.partial(_na_kernel, rows_per_step=rows_per_step, n_rows=n_rows, win_rows=win_rows),
        grid=(bsz, n_rows // rows_per_step),
        in_specs=[
            pl.BlockSpec((1, tq, NA_WIDTH), lambda b, i: (b, i, 0)),
            pl.BlockSpec((1, seq, NA_WIDTH), lambda b, i: (b, 0, 1)),
            pl.BlockSpec((1, seq, NA_WIDTH), lambda b, i: (b, 0, 2)),
            pl.BlockSpec((1, n_ctx, NA_WIDTH), lambda b, i: (b, 0, 1)),
            pl.BlockSpec((1, n_ctx, NA_WIDTH), lambda b, i: (b, 0, 2)),
            pl.BlockSpec(tbl.shape, lambda b, i: (0, 0, 0, 0)),
        ],
        out_specs=pl.BlockSpec((1, tq, NA_WIDTH), lambda b, i: (b, i, 0)),
        out_shape=jax.ShapeDtypeStruct((bsz, seq, NA_WIDTH), BF16),
        scratch_shapes=[pltpu.VMEM((NA_HEADS, tq, n_ctx), F32), pltpu.VMEM((NA_HEADS, tq, n_ctx), BF16),
                        pltpu.VMEM((NA_WIDTH // LANES, tq, LANES), F32)],
        compiler_params=_cparams(("parallel", "arbitrary")),
        name="nbr_attn",
    )(zna, zna, zna, zna_ctx, zna_ctx, tbl)


def _na_ctx_kernel(q_ref, k_ref, v_ref, o_ref):
    lane = lax.broadcasted_iota(jnp.int32, (1, LANES), 1)
    first_head = lane < HEAD_DIM
    scale = HEAD_DIM ** -0.5
    for p in range(NA_WIDTH // LANES):
        ls = slice(p * LANES, (p + 1) * LANES)
        qp = q_ref[0, :, ls] * scale
        kp = k_ref[0, :, ls]
        vp = v_ref[0, :, ls]
        outs = []
        for hh in range(2):
            msk = first_head if hh == 0 else jnp.logical_not(first_head)
            qm = jnp.where(msk, qp, jnp.zeros_like(qp[:1, :1]))
            s = _dot_nt(qm, kp)
            e = jnp.exp(s - jnp.max(s, axis=-1, keepdims=True))
            den = jnp.sum(e, axis=-1, keepdims=True)
            outs.append(_dot(e.astype(BF16), vp) / den)
        o_ref[0, :, ls] = jnp.where(first_head, outs[0], outs[1]).astype(BF16)


def _na_ctx(zna_ctx):
    bsz, n_ctx, _ = zna_ctx.shape
    col = lambda j: pl.BlockSpec((1, n_ctx, NA_WIDTH), lambda b: (b, 0, j))
    return pl.pallas_call(
        _na_ctx_kernel,
        grid=(bsz,),
        in_specs=[col(0), col(1), col(2)],
        out_specs=pl.BlockSpec((1, n_ctx, NA_WIDTH), lambda b: (b, 0, 0)),
        out_shape=jax.ShapeDtypeStruct((bsz, n_ctx, NA_WIDTH), BF16),
        compiler_params=_cparams(("parallel",)),
        name="ctx_attn",
    )(zna_ctx, zna_ctx, zna_ctx)


def _outproj_kernel(ya_ref, yb_ref, yn_ref, w_ref, x_ref, g_ref, lg_ref, lb_ref, o_ref, *, alpha):
    ycat = jnp.concatenate([ya_ref[0], yb_ref[0], yn_ref[0]], axis=1)
    y = _dot(ycat, w_ref[...])
    s = alpha * x_ref[0] + g_ref[0] * y
    o_ref[0] = _layer_norm_rows(s) * lg_ref[...] + lb_ref[...]


def _out_proj(ya, yb, yn, w_out_b, x, mod3, lg, lb, is_ctx, ctx_row, tm, alpha):
    bsz, seq, d = x.shape
    tok = lambda w: pl.BlockSpec((1, tm, w), lambda b, i: (b, i, 0))
    full = lambda a: pl.BlockSpec(a.shape, lambda b, i: (0,) * a.ndim)
    return pl.pallas_call(
        functools.partial(_outproj_kernel, alpha=alpha),
        grid=(bsz, seq // tm),
        in_specs=[tok(GMLP_WIDTH), tok(DN_WIDTH), tok(NA_WIDTH), full(w_out_b), tok(d),
                  _mod_spec(2, d, is_ctx, ctx_row), full(lg), full(lb)],
        out_specs=tok(d),
        out_shape=jax.ShapeDtypeStruct((bsz, seq, d), F32),
        compiler_params=_cparams(("parallel", "parallel")),
        name="out_proj",
    )(ya, yb, yn, w_out_b, x, mod3, lg, lb)


FFN_CHUNK = 256


def _ffn_kernel(x_ref, xp_ref, xn_ref, sh_ref, sc_ref, g_ref, wup_ref, cw_ref, cb_ref, wd_ref,
                lg_ref, lb_ref, o_ref, gated_ref, *, tm, n_tiles, d_ff, alpha):
    halo = 8
    i = pl.program_id(1)
    has_prev = (i > 0).astype(F32)
    has_next = (i < n_tiles - 1).astype(F32)
    x = x_ref[0]
    sc = 1.0 + sc_ref[0]
    sh = sh_ref[0]
    h_mid = _layer_norm_rows(x) * sc + sh
    h_prev = (_layer_norm_rows(xp_ref[0]) * sc + sh) * has_prev
    h_next = (_layer_norm_rows(xn_ref[0]) * sc + sh) * has_next
    h_ext = jnp.concatenate([h_prev, h_mid, h_next], axis=0).astype(BF16)
    h_b = h_mid.astype(BF16)
    n_ext = tm + 2 * halo
    for j in range(d_ff // FFN_CHUNK):
        cols = slice(j * FFN_CHUNK, (j + 1) * FFN_CHUNK)
        a = _dot(h_ext, wup_ref[:, cols])
        vv = _dot(h_b, wup_ref[:, d_ff + j * FFN_CHUNK:d_ff + (j + 1) * FFN_CHUNK])
        cw = cw_ref[:, cols]
        a_m1 = pltpu.roll(a, 1, 0)[halo:halo + tm]
        a_p1 = pltpu.roll(a, n_ext - 1, 0)[halo:halo + tm]
        conv = a_m1 * cw[0:1] + a[halo:halo + tm] * cw[1:2] + a_p1 * cw[2:3] + cb_ref[:, cols]
        gated_ref[:, cols] = (_silu(conv) * vv).astype(BF16)
    y = _dot(gated_ref[...], wd_ref[...])
    s = alpha * x + g_ref[0] * y
    o_ref[0] = _layer_norm_rows(s) * lg_ref[...] + lb_ref[...]


def _ffn(x, mod3, wup, cw, cb, wd, lg, lb, is_ctx, ctx_row, tm, alpha):
    bsz, seq, d = x.shape
    n_tiles = seq // tm
    d_ff = wd.shape[0]
    assert d_ff % FFN_CHUNK == 0
    hb = tm // 8
    tok = pl.BlockSpec((1, tm, d), lambda b, i: (b, i, 0))
    full = lambda a: pl.BlockSpec(a.shape, lambda b, i: (0,) * a.ndim)
    return pl.pallas_call(
        functools.partial(_ffn_kernel, tm=tm, n_tiles=n_tiles, d_ff=d_ff, alpha=alpha),
        grid=(bsz, n_tiles),
        in_specs=[
            tok,
            pl.BlockSpec((1, 8, d), lambda b, i: (b, jnp.maximum(i * hb - 1, 0), 0)),
            pl.BlockSpec((1, 8, d), lambda b, i: (b, jnp.minimum((i + 1) * hb, seq // 8 - 1), 0)),
            _mod_spec(3, d, is_ctx, ctx_row), _mod_spec(4, d, is_ctx, ctx_row), _mod_spec(5, d, is_ctx, ctx_row),
            full(wup), full(cw), full(cb), full(wd), full(lg), full(lb),
        ],
        out_specs=tok,
        out_shape=jax.ShapeDtypeStruct((bsz, seq, d), F32),
        scratch_shapes=[pltpu.VMEM((tm, d_ff), BF16)],
        compiler_params=_cparams(("parallel", "parallel")),
        name="conv_glu",
    )(x, x, x, mod3, mod3, mod3, wup, cw, cb, wd, lg, lb)


def _rope_tables(seq):
    half = HEAD_DIM // 2
    nf = half // 2
    pos = jnp.arange(seq)
    inv = ROPE_BASE ** (-jnp.arange(nf, dtype=F32) / nf)
    ang_r = (pos // GRID_W)[:, None].astype(F32) * inv
    ang_c = (pos % GRID_W)[:, None].astype(F32) * inv
    cos_h = jnp.concatenate([jnp.cos(ang_r)] * 2 + [jnp.cos(ang_c)] * 2, axis=1)
    sin_h = jnp.concatenate([-jnp.sin(ang_r), jnp.sin(ang_r), -jnp.sin(ang_c), jnp.sin(ang_c)], axis=1)
    reps = LANES // HEAD_DIM
    return jnp.tile(cos_h, (1, reps)), jnp.tile(sin_h, (1, reps))


def _bias_table(rpb, n_rows):
    wr = min(WIN_ROWS, n_rows)
    nh = rpb.shape[0]
    n_dr = 2 * WIN_ROWS - 1
    col = np.arange(GRID_W)
    cs = np.clip(col - WIN_COLS // 2, 0, GRID_W - WIN_COLS)
    valid = (col[None, :] >= cs[:, None]) & (col[None, :] < cs[:, None] + WIN_COLS)
    dc = col[None, :] - col[:, None] + (WIN_COLS - 1)
    onehot = np.zeros((2 * WIN_COLS - 1, GRID_W, GRID_W), np.float32)
    cc, kk = np.nonzero(valid)
    onehot[dc[cc, kk], cc, kk] = 1.0
    t = jnp.einsum('hdm,mck->hdck', rpb.astype(F32), jnp.asarray(onehot), precision=lax.Precision.HIGHEST)
    t = jnp.where(valid[None, None], t, NEG_BIG)
    per_var = []
    for var in range(wr):
        lo = WIN_ROWS - 1 - var
        per_var.append(t[:, lo:lo + wr].transpose(0, 2, 1, 3).reshape(nh // 2, 2 * GRID_W, wr * GRID_W))
    return jnp.stack(per_var, axis=1)


def _gate_mats():
    expand = np.zeros((2, LANES, 2 * DN_WIDTH), np.float32)
    for dd in range(2):
        for h in range(DN_HEADS):
            for c in range(GATE_COPIES):
                expand[dd, c * GATE_COPY_STRIDE + dd * DN_HEADS + h, h * HEAD_DIM:(h + 1) * HEAD_DIM] = 1.0
            for c in range(2):
                src = c * GATE_COPY_STRIDE + 2 * DN_HEADS + dd * DN_HEADS + h
                expand[dd, src, DN_WIDTH + h * HEAD_DIM:DN_WIDTH + (h + 1) * HEAD_DIM] = 1.0
    return jnp.asarray(expand[0], BF16), jnp.asarray(expand[1], BF16)


def _gate_row(v):
    row = jnp.zeros((1, LANES), F32)
    for c in range(GATE_COPIES):
        row = row.at[0, c * GATE_COPY_STRIDE:c * GATE_COPY_STRIDE + v.shape[0]].set(v.astype(F32))
    return row


def _token_tile(seq):
    return 512 if seq % 512 == 0 else 256


def kernel(x, c, ctx, c_ctx, w_ada, b_ada, w_in, gmlp_ln_g, gmlp_ln_b, gmlp_ws, gmlp_bs, dn_conv, dn_a_log,
           dn_dt_bias, dn_norm_g, na_rpb, w_out, ln1_g, ln1_b, ffn_up, ffn_conv, ffn_conv_b, ffn_down,
           ln2_g, ln2_b):
    depth = w_ada.shape[0]
    bsz, seq, d = x.shape
    n_ctx = ctx.shape[1]
    d_ff = ffn_down.shape[1]
    alpha = (2 * depth) ** 0.25
    ctx_row = bsz
    tm = _token_tile(seq)
    tm_c = _token_tile(n_ctx)

    mod = _mod_all(c, c_ctx, w_ada, b_ada)
    rope_tabs = _rope_tables(seq)
    e_fwd, e_bwd = _gate_mats()
    gm384_sum = jnp.asarray(_block_diag_np(LANES // HEAD_DIM, HEAD_DIM, HEAD_DIM, 1.0), BF16)
    gm384_mean = jnp.asarray(_block_diag_np(LANES // HEAD_DIM, HEAD_DIM, HEAD_DIM, 1.0 / HEAD_DIM), BF16)
    nb = 4 if bsz % 4 == 0 else (2 if bsz % 2 == 0 else 1)
    wg = DN_GROUP_HEADS * HEAD_DIM
    s_zero = jnp.zeros((bsz // nb, nb * (DN_WIDTH // wg), wg, wg), F32)
    o1 = GMLP_COLS
    o2 = o1 + DN_MAIN_COLS
    o3 = o2 + DN_AB_COLS

    for l in range(depth):
        ctx_out = l < depth - 1
        mod3 = mod[l].reshape(8, 1, 6 * d)
        wl = w_in[l]
        ab_cols = [wl[:, o2:o3], jnp.zeros((d, GATE_COPY_STRIDE - DN_AB_COLS), F32)] * GATE_COPIES
        ab_cols.append(jnp.zeros((d, LANES - GATE_COPIES * GATE_COPY_STRIDE), F32))
        w_in_p = jnp.concatenate([wl[:, :o2], wl[:, o3:]] + ab_cols, axis=1).astype(BF16)
        lng = gmlp_ln_g[l].reshape(1, GMLP_WIDTH)
        lnb = gmlp_ln_b[l].reshape(1, GMLP_WIDTH)
        wsp = gmlp_ws[l].transpose(1, 0, 2).reshape(GMLP_CHUNK, GMLP_HEADS * GMLP_CHUNK).astype(BF16)
        bsx = jnp.repeat(gmlp_bs[l].T, HEAD_DIM, axis=1)
        aexp_row = _gate_row(jnp.exp(dn_a_log[l].astype(F32)).reshape(-1))
        dtb_row = _gate_row(dn_dt_bias[l].reshape(-1))
        ng_row = jnp.tile(dn_norm_g[l].astype(F32), DN_HEADS).reshape(1, DN_WIDTH)
        tbl = _bias_table(na_rpb[l], seq // GRID_W)
        w_out_b = w_out[l].astype(BF16)
        lg1 = ln1_g[l].reshape(1, d)
        lb1 = ln1_b[l].reshape(1, d)
        lg2 = ln2_g[l].reshape(1, d)
        lb2 = ln2_b[l].reshape(1, d)
        wup = ffn_up[l].astype(BF16)
        wd = ffn_down[l].astype(BF16)
        cw = ffn_conv[l]
        cb = ffn_conv_b[l].reshape(1, d_ff)

        prep = (dn_conv[l], aexp_row, dtb_row, gm384_sum, (gm384_mean, lng, lnb, wsp, bsx))
        ya_c, qkv_c, gate_c, gp_c, zna_c = _in_proj(ctx, mod3, w_in_p, *prep, None, True, ctx_row, tm_c)
        yb_c, s_ctx = _dn_scan(qkv_c, gate_c, gp_c, e_fwd, e_bwd, ng_row, gm384_mean, (s_zero, s_zero), nb)

        ya, qkv, gate, gp, zna = _in_proj(x, mod3, w_in_p, *prep, rope_tabs, False, ctx_row, tm)
        yb, _ = _dn_scan(qkv, gate, gp, e_fwd, e_bwd, ng_row, gm384_mean, s_ctx, nb)
        yn = _na(zna, zna_c, tbl, 8)
        x1 = _out_proj(ya, yb, yn, w_out_b, x, mod3, lg1, lb1, False, ctx_row, tm, alpha)
        x_next = _ffn(x1, mod3, wup, cw, cb, wd, lg2, lb2, False, ctx_row, tm, alpha)

        if ctx_out:
            yn_c = _na_ctx(zna_c)
            c1 = _out_proj(ya_c, yb_c, yn_c, w_out_b, ctx, mod3, lg1, lb1, True, ctx_row, tm_c, alpha)
            ctx = _ffn(c1, mod3, wup, cw, cb, wd, lg2, lb2, True, ctx_row, tm_c, alpha)
        x = x_next
    return x
```

```python
import functools
import math

import numpy as np
import jax
import jax.numpy as jnp
from jax import lax
from jax.experimental import pallas as pl
from jax.experimental.pallas import tpu as pltpu

F32 = jnp.float32
BF16 = jnp.bfloat16

HEAD_DIM = 64
GRID_W = 64
GMLP_HEADS = 4
GMLP_WIDTH = GMLP_HEADS * HEAD_DIM
GMLP_CHUNK = 128
DN_HEADS = 6
DN_WIDTH = DN_HEADS * HEAD_DIM
NA_HEADS = 6
NA_WIDTH = NA_HEADS * HEAD_DIM
DELTA_CHUNK = 64
WIN_ROWS = 8
WIN_COLS = 16
ROPE_BASE = 10000.0
EPS = 1e-6
NEG_BIG = -1e30
LANES = 128
VMEM_LIMIT = 56 * 1024 * 1024

GMLP_COLS = 2 * GMLP_WIDTH
DN_MAIN_COLS = 4 * DN_WIDTH
DN_AB_COLS = 4 * DN_HEADS
NA_COLS = 3 * NA_WIDTH


def _cparams(sem):
    return pltpu.CompilerParams(dimension_semantics=sem, vmem_limit_bytes=VMEM_LIMIT)


def _split2(x):
    hi = x.astype(BF16)
    lo = (x - hi.astype(F32)).astype(BF16)
    return hi, lo


def _split3(x):
    a = x.astype(BF16)
    r = x - a.astype(F32)
    b = r.astype(BF16)
    c = (r - b.astype(F32)).astype(BF16)
    return a, b, c


def _dot(a, b):
    return jnp.dot(a, b, preferred_element_type=F32)


def _dot_nt(a, b):
    return lax.dot_general(a, b, (((1,), (1,)), ((), ())), preferred_element_type=F32)


def _dot_tn(a, b):
    return lax.dot_general(a, b, (((0,), (0,)), ((), ())), preferred_element_type=F32)


def _dot_x3(x, m):
    a, b, c = _split3(x)
    return _dot(a, m) + _dot(b, m) + _dot(c, m)


def _dot_x2(x, m):
    a, b = _split2(x)
    return _dot(a, m) + _dot(b, m)


def _head_sums(x, pair_mat):
    parts = [_dot(x[:, p * LANES:(p + 1) * LANES].astype(BF16), pair_mat) for p in range(x.shape[1] // LANES)]
    return jnp.concatenate(parts, axis=1)


def _silu(x):
    return x * jax.nn.sigmoid(x)


def _layer_norm_rows(x):
    mu = jnp.mean(x, axis=-1, keepdims=True)
    xc = x - mu
    var = jnp.mean(xc * xc, axis=-1, keepdims=True)
    return xc * lax.rsqrt(var + EPS)


def _block_diag_np(n_blocks, rows, cols, value):
    m = np.zeros((n_blocks * rows, n_blocks * cols), np.float32)
    for g in range(n_blocks):
        m[g * rows:(g + 1) * rows, g * cols:(g + 1) * cols] = value
    return m


def _mod_kernel(c_ref, w_ref, b_ref, o_ref):
    a = _silu(c_ref[...])
    w = w_ref[0]
    a1, a2 = _split2(a)
    w1, w2 = _split2(w)
    o_ref[0] = _dot(a1, w1) + _dot(a1, w2) + _dot(a2, w1) + b_ref[0]


def _mod_all(c, c_ctx, w_ada, b_ada):
    n_layers, d, n6 = w_ada.shape
    bsz = c.shape[0]
    rows = jnp.concatenate([c, c_ctx[None, :], jnp.zeros((8 - bsz - 1, d), F32)], axis=0)
    tn = 1536
    return pl.pallas_call(
        _mod_kernel,
        grid=(n_layers, n6 // tn),
        in_specs=[
            pl.BlockSpec((8, d), lambda l, j: (0, 0)),
            pl.BlockSpec((1, d, tn), lambda l, j: (l, 0, j)),
            pl.BlockSpec((1, 1, tn), lambda l, j: (l, 0, j)),
        ],
        out_specs=pl.BlockSpec((1, 8, tn), lambda l, j: (l, 0, j)),
        out_shape=jax.ShapeDtypeStruct((n_layers, 8, n6), F32),
        compiler_params=_cparams(("parallel", "parallel")),
        name="adaln_mod",
    )(rows, w_ada, b_ada.reshape(n_layers, 1, n6))


def _mod_spec(chunk, d, is_ctx, ctx_row):
    if is_ctx:
        return pl.BlockSpec((1, 1, d), lambda b, i: (ctx_row, 0, chunk))
    return pl.BlockSpec((1, 1, d), lambda b, i: (b, 0, chunk))


def _inproj_kernel(*refs, tm, n_tiles, rotary):
    if rotary:
        (x_ref, xp_ref, xn_ref, sh_ref, sc_ref, w_ref, cw_ref, aexp_ref, dtb_ref, g_ref, gmm_ref, lng_ref, lnb_ref,
         wsp_ref, bsx_ref, cos_ref, sin_ref, ya_ref, qkv_ref, gate_ref, gp_ref, zna_ref) = refs
    else:
        (x_ref, xp_ref, xn_ref, sh_ref, sc_ref, w_ref, cw_ref, aexp_ref, dtb_ref, g_ref, gmm_ref, lng_ref, lnb_ref,
         wsp_ref, bsx_ref, ya_ref, qkv_ref, gate_ref, gp_ref, zna_ref) = refs
    halo = 8
    i = pl.program_id(1)
    has_prev = (i > 0).astype(F32)
    has_next = (i < n_tiles - 1).astype(F32)
    sc = 1.0 + sc_ref[0]
    sh = sh_ref[0]
    h_mid = _layer_norm_rows(x_ref[0]) * sc + sh
    h_prev = (_layer_norm_rows(xp_ref[0]) * sc + sh) * has_prev
    h_next = (_layer_norm_rows(xn_ref[0]) * sc + sh) * has_next
    h_ext = jnp.concatenate([h_prev, h_mid, h_next], axis=0).astype(BF16)
    h = h_mid.astype(BF16)
    o_q = GMLP_COLS
    o_g = o_q + 3 * DN_WIDTH
    o_n = o_g + DN_WIDTH
    o_ab = o_n + NA_COLS

    x_ext = _dot(h_ext, w_ref[:, o_q:o_g])
    ab = _dot(h, w_ref[:, o_ab:o_ab + LANES])
    n_ext = tm + 2 * halo
    x_m1 = pltpu.roll(x_ext, 1, 0)[halo:halo + tm]
    x_p1 = pltpu.roll(x_ext, n_ext - 1, 0)[halo:halo + tm]
    x_0 = x_ext[halo:halo + tm]
    cw = cw_ref[...]
    y = _silu(x_m1 * cw[0:1] + x_0 * cw[1:2] + x_p1 * cw[2:3])
    ya_ref[0] = _gmlp_mix(_dot(h, w_ref[:, 0:o_q]), gmm_ref[...], lng_ref[...], lnb_ref[...], wsp_ref[...],
                          bsx_ref[...]).astype(BF16)
    q = y[:, 0:DN_WIDTH]
    k = y[:, DN_WIDTH:2 * DN_WIDTH]
    v = y[:, 2 * DN_WIDTH:3 * DN_WIDTH]
    gmat = g_ref[...]
    q = q * lax.rsqrt(_head_sums(q * q, gmat) + EPS)
    k = k * lax.rsqrt(_head_sums(k * k, gmat) + EPS)
    gate_ref[0] = _dot(h, w_ref[:, o_g:o_n]).astype(BF16)
    if rotary:
        cos2 = cos_ref[...]
        sin2 = sin_ref[...]
        lane = lax.broadcasted_iota(jnp.int32, (tm, LANES), 1)
        first = (lane % 32) < 16

        def rope(t):
            parts = []
            for p in range(DN_WIDTH // LANES):
                tp = t[:, p * LANES:(p + 1) * LANES]
                partner = jnp.where(first, pltpu.roll(tp, LANES - 16, 1), pltpu.roll(tp, 16, 1))
                parts.append(tp * cos2 + partner * sin2)
            return jnp.concatenate(parts, axis=1)

        q = rope(q)
        k = rope(k)
    q = q * (HEAD_DIM ** -0.5)
    qkv_ref[0, :, 0:DN_WIDTH] = q.astype(BF16)
    qkv_ref[0, :, DN_WIDTH:2 * DN_WIDTH] = k.astype(BF16)
    qkv_ref[0, :, 2 * DN_WIDTH:3 * DN_WIDTH] = v.astype(BF16)
    zna_ref[0] = _dot(h, w_ref[:, o_n:o_ab]).astype(BF16)

    lane_ab = lax.broadcasted_iota(jnp.int32, ab.shape, 1)
    in_copy = lane_ab % GATE_COPY_STRIDE
    copy = lane_ab // GATE_COPY_STRIDE
    xs = ab + dtb_ref[...]
    softplus = jnp.maximum(xs, 0.0) + jnp.log1p(jnp.exp(-jnp.abs(xs)))
    log_decay = -aexp_ref[...] * softplus
    beta = jax.nn.sigmoid(ab)
    blk = min(tm, 4 * DELTA_CHUNK)
    ti = lax.broadcasted_iota(jnp.int32, (blk, blk), 0)
    tj = lax.broadcasted_iota(jnp.int32, (blk, blk), 1)
    same_chunk = (ti // DELTA_CHUNK) == (tj // DELTA_CHUNK)
    l_fwd = jnp.where(same_chunk, jnp.where(ti >= tj, 1.0, 0.0), 0.0).astype(BF16)
    l_bwd = jnp.where(same_chunk, jnp.where(ti <= tj, 1.0, 0.0), 0.0).astype(BF16)
    g1, g2 = _split2(log_decay)
    gc_f, gc_b = [], []
    for rb in range(tm // blk):
        rows = slice(rb * blk, (rb + 1) * blk)
        gc_f.append(_dot(l_fwd, g1[rows]) + _dot(l_fwd, g2[rows]))
        gc_b.append(_dot(l_bwd, g1[rows]) + _dot(l_bwd, g2[rows]))
    gc = jnp.where(in_copy < DN_HEADS, jnp.concatenate(gc_f, axis=0), jnp.concatenate(gc_b, axis=0))
    c1, c2, c3 = _split3(gc)
    b1, b2 = _split2(beta)
    dec_piece = jnp.where(copy == 0, c1, jnp.where(copy == 1, c2, c3)).astype(F32)
    beta_piece = jnp.where(copy == 0, b1, jnp.where(copy == 1, b2, jnp.zeros_like(b2))).astype(F32)
    packed = jnp.where(in_copy < 2 * DN_HEADS, dec_piece, beta_piece)
    gp_ref[0] = packed.astype(BF16)


def _in_proj(x, mod3, w_in_p, conv_w, aexp_row, dtb_row, gm384, gmlp_args, rope_tabs, is_ctx, ctx_row, tm):
    bsz, seq, d = x.shape
    n_tiles = seq // tm
    hb = tm // 8
    rotary = rope_tabs is not None
    tok = lambda w: pl.BlockSpec((1, tm, w), lambda b, i: (b, i, 0))
    full = lambda a: pl.BlockSpec(a.shape, lambda b, i: (0,) * a.ndim)
    in_specs = [
        tok(d),
        pl.BlockSpec((1, 8, d), lambda b, i: (b, jnp.maximum(i * hb - 1, 0), 0)),
        pl.BlockSpec((1, 8, d), lambda b, i: (b, jnp.minimum((i + 1) * hb, seq // 8 - 1), 0)),
        _mod_spec(0, d, is_ctx, ctx_row),
        _mod_spec(1, d, is_ctx, ctx_row),
        full(w_in_p), full(conv_w), full(aexp_row), full(dtb_row), full(gm384),
    ] + [full(a) for a in gmlp_args]
    args = [x, x, x, mod3, mod3, w_in_p, conv_w, aexp_row, dtb_row, gm384] + list(gmlp_args)
    if rotary:
        in_specs += [pl.BlockSpec((tm, LANES), lambda b, i: (i, 0))] * 2
        args += list(rope_tabs)
    widths = [GMLP_WIDTH, 3 * DN_WIDTH, DN_WIDTH, LANES, NA_COLS]
    return pl.pallas_call(
        functools.partial(_inproj_kernel, tm=tm, n_tiles=n_tiles, rotary=rotary),
        grid=(bsz, n_tiles),
        in_specs=in_specs,
        out_specs=[tok(w) for w in widths],
        out_shape=[jax.ShapeDtypeStruct((bsz, seq, w), BF16) for w in widths],
        compiler_params=_cparams(("parallel", "parallel")),
        name="in_proj",
    )(*args)


def _gmlp_mix(z, gmat, lng, lnb, wsp, bsx):
    tm = z.shape[0]
    z = jax.nn.gelu(z)
    u = z[:, :GMLP_WIDTH]
    v = z[:, GMLP_WIDTH:]
    vc = v - _head_sums(v, gmat)
    var = _head_sums(vc * vc, gmat)
    vn = (vc * lax.rsqrt(var + EPS) * lng + lnb).astype(BF16)
    nrow = GMLP_HEADS * GMLP_CHUNK
    ri = lax.broadcasted_iota(jnp.int32, (nrow, GMLP_WIDTH), 0) // GMLP_CHUNK
    ci = lax.broadcasted_iota(jnp.int32, (nrow, GMLP_WIDTH), 1) // HEAD_DIM
    diag = ri == ci
    outs = []
    for n in range(tm // GMLP_CHUNK):
        rows = slice(n * GMLP_CHUNK, (n + 1) * GMLP_CHUNK)
        vch = vn[rows]
        bd = jnp.where(diag, jnp.concatenate([vch] * GMLP_HEADS, axis=0), jnp.zeros_like(vch[:1, :1]))
        outs.append(u[rows] * (_dot(wsp, bd) + bsx))
    return jnp.concatenate(outs, axis=0)


GATE_COPIES = 3
GATE_COPY_STRIDE = 32
INV_BLOCK = 8
DN_GROUP_HEADS = 2
DN_CHUNKS_PER_STEP = 8


def _dn_intra(q, k, v, gcx, bx, masks):
    cs = DELTA_CHUNK
    causal, strict, eye, last_row, diag, diag_mul, inv_diag, inv_levels = masks
    grp = range(len(q))

    def bd(x):
        return jnp.concatenate([x.astype(BF16)] * DN_GROUP_HEADS, axis=0) * diag_mul

    qf = [q[g].astype(F32) for g in grp]
    kf = [k[g].astype(F32) for g in grp]
    vf = [v[g].astype(F32) for g in grp]
    gc_t = [jnp.sum(jnp.where(eye, gcx[g], 0.0), axis=0, keepdims=True) for g in grp]
    dm = [jnp.where(causal, jnp.exp(jnp.where(causal, gcx[g] - gc_t[g], 0.0)), 0.0) for g in grp]
    dms = [jnp.where(strict, dm[g], 0.0) for g in grp]
    gl = [jnp.sum(jnp.where(last_row, gcx[g], 0.0), axis=0, keepdims=True) for g in grp]
    egx = [jnp.exp(gcx[g]) for g in grp]
    kdx = [jnp.exp(gl[g] - gcx[g]) for g in grp]
    kb = [kf[g] * bx[g] for g in grp]
    vb = [vf[g] * bx[g] for g in grp]
    bk = [bd(k[g]) for g in grp]
    lhs = [jnp.concatenate([kb[g].astype(BF16), q[g]], axis=0) for g in grp]
    p1 = [_dot_nt(lhs[g], bk[g]) for g in grp]
    a_mat = [p1[g][:cs] * dms[g] for g in grp]
    attn = [p1[g][cs:] * dm[g] for g in grp]

    eye_f = jnp.where(eye, 1.0, 0.0)
    a_d = [a_mat[g] * inv_diag for g in grp]
    t_mat = [eye_f - a_d[g] for g in grp]
    b_pow = [_dot(a_d[g].astype(BF16), bd(a_d[g])) for g in grp]
    pp = [_dot(jnp.concatenate([t_mat[g], b_pow[g]], axis=0).astype(BF16), bd(b_pow[g])) for g in grp]
    t_mat = [t_mat[g] + pp[g][:cs] for g in grp]
    t_mat = [t_mat[g] + _dot(t_mat[g].astype(BF16), bd(pp[g][cs:])) for g in grp]
    for coupling in inv_levels:
        x = [_dot(t_mat[g].astype(BF16), bd(a_mat[g] * coupling)) for g in grp]
        t_mat = [t_mat[g] - _dot(x[g].astype(BF16), bd(t_mat[g])) for g in grp]

    tb = [t_mat[g].astype(BF16) for g in grp]
    u = [_dot(tb[g], bd(vb[g])) for g in grp]
    wk = [_dot(tb[g], bd(kb[g] * egx[g])) for g in grp]
    lhs2 = [jnp.concatenate([wk[g], qf[g] * egx[g]], axis=0).astype(BF16) for g in grp]
    kd = [(kf[g] * kdx[g]).astype(BF16) for g in grp]
    egl = [jnp.exp(gl[g]) for g in grp]
    attn_b = [attn[g].astype(BF16) for g in grp]
    return u, lhs2, attn_b, kd, egl


def _dn_inter(intra, st, masks):
    cs = DELTA_CHUNK
    diag, diag_mul = masks[4], masks[5]
    u, lhs2, attn_b, kd, egl = intra
    grp = range(len(u))
    p2 = [_dot(lhs2[g], st[g].astype(BF16)) for g in grp]
    v_new = [u[g] - p2[g][:cs] for g in grp]
    vnb = [v_new[g].astype(BF16) for g in grp]
    bdv = [jnp.concatenate([vnb[g]] * DN_GROUP_HEADS, axis=0) * diag_mul for g in grp]
    o = [p2[g][cs:] + _dot(attn_b[g], bdv[g]) for g in grp]
    st_new = [st[g] * egl[g] + jnp.where(diag, _dot_tn(kd[g], vnb[g]), 0.0) for g in grp]
    return o, st_new


def _dn_masks(reverse):
    cs = DELTA_CHUNK
    wg = DN_GROUP_HEADS * HEAD_DIM
    i3 = lax.broadcasted_iota(jnp.int32, (cs, wg), 0)
    j3 = lax.broadcasted_iota(jnp.int32, (cs, wg), 1) % HEAD_DIM
    if reverse:
        causal, strict, last_row = i3 <= j3, i3 < j3, i3 == 0
    else:
        causal, strict, last_row = i3 >= j3, i3 > j3, i3 == cs - 1
    ra = lax.broadcasted_iota(jnp.int32, (wg, wg), 0) // HEAD_DIM
    ca = lax.broadcasted_iota(jnp.int32, (wg, wg), 1) // HEAD_DIM
    diag = ra == ca
    diag_mul = jnp.where(diag, 1.0, 0.0).astype(BF16)
    inv_diag = jnp.where(i3 // INV_BLOCK == j3 // INV_BLOCK, 1.0, 0.0)
    inv_levels = []
    size = INV_BLOCK
    while size < cs:
        same_pair = i3 // (2 * size) == j3 // (2 * size)
        inv_levels.append(jnp.where(same_pair, jnp.where(i3 // size == j3 // size, 0.0, 1.0), 0.0))
        size *= 2
    return causal, strict, i3 == j3, last_row, diag, diag_mul, inv_diag, inv_levels


def _dn_scan_kernel(*refs, n_steps, nb, cps, reverse, finish):
    if finish:
        (qkv_ref, gp_ref, e_ref, s0_ref, of_ref, gate_ref, ng_ref, gm_ref, out_ref, s_out_ref, st_scr) = refs
    else:
        (qkv_ref, gp_ref, e_ref, s0_ref, out_ref, s_out_ref, st_scr) = refs
    w = DN_WIDTH
    wg = DN_GROUP_HEADS * HEAD_DIM
    gpc = w // wg
    n_grp = nb * gpc
    i = pl.program_id(1)

    @pl.when(i == 0)
    def _():
        st_scr[...] = s0_ref[...]

    masks = _dn_masks(reverse)
    e_comb = e_ref[...]
    cs = DELTA_CHUNK
    order = list(range(cps - 1, -1, -1)) if reverse else list(range(cps))

    def groups(arrs, off):
        return [a[:, off + p * wg:off + (p + 1) * wg] for a in arrs for p in range(gpc)]

    qs, ks, vs, gs, bs = [], [], [], [], []
    step_rows = cps * cs
    ex_all = _dot(gp_ref[...].reshape(nb * step_rows, LANES), e_comb)
    for c in order:
        rows = slice(c * cs, (c + 1) * cs)
        qkv = [qkv_ref[bb, rows, :] for bb in range(nb)]
        ex = [ex_all[bb * step_rows + c * cs:bb * step_rows + (c + 1) * cs] for bb in range(nb)]
        qs += groups(qkv, 0)
        ks += groups(qkv, w)
        vs += groups(qkv, 2 * w)
        gs += groups(ex, 0)
        bs += groups(ex, w)
    intra = _dn_intra(qs, ks, vs, gs, bs, masks)
    st = [st_scr[g] for g in range(n_grp)]
    totals = {}
    for idx, c in enumerate(order):
        rows = slice(c * cs, (c + 1) * cs)
        part = [t[idx * n_grp:(idx + 1) * n_grp] for t in intra]
        o_g, st = _dn_inter(part, st, masks)
        for bb in range(nb):
            o = jnp.concatenate(o_g[bb * gpc:(bb + 1) * gpc], axis=1)
            if finish:
                totals[bb, c] = of_ref[bb, rows, :] + o
            else:
                out_ref[bb, rows, :] = o
    for g in range(n_grp):
        st_scr[g] = st[g]
    if finish:
        for bb in range(nb):
            ot = jnp.concatenate([totals[bb, c] for c in range(cps)], axis=0)
            ms = _head_sums(ot * ot, gm_ref[...])
            gate = gate_ref[bb].astype(F32)
            out_ref[bb] = (ot * lax.rsqrt(ms + EPS) * ng_ref[...] * _silu(gate)).astype(BF16)

    @pl.when(i == n_steps - 1)
    def _():
        s_out_ref[...] = st_scr[...]


def _dn_scan_dir(qkv, gp, e_comb, s0, reverse, finish_args, nb):
    bsz, seq, _ = qkv.shape
    cps = min(DN_CHUNKS_PER_STEP, seq // DELTA_CHUNK)
    cs = cps * DELTA_CHUNK
    assert seq % cs == 0
    n = seq // cs
    wg = DN_GROUP_HEADS * HEAD_DIM
    n_grp = nb * (DN_WIDTH // wg)
    finish = finish_args is not None
    cidx = (lambda i: n - 1 - i) if reverse else (lambda i: i)
    full = lambda a: pl.BlockSpec(a.shape, lambda g, i: (0,) * a.ndim)
    tok = lambda wdt, col=0: pl.BlockSpec((nb, cs, wdt), lambda g, i: (g, cidx(i), col))
    st = pl.BlockSpec((None, n_grp, wg, wg), lambda g, i: (g, 0, 0, 0))
    in_specs = [tok(3 * DN_WIDTH), tok(LANES), full(e_comb), st]
    args = [qkv, gp, e_comb, s0]
    if finish:
        o_other, gate, ng_row, gm_mean = finish_args
        in_specs += [tok(DN_WIDTH), tok(DN_WIDTH), full(ng_row), full(gm_mean)]
        args += [o_other, gate, ng_row, gm_mean]
    return pl.pallas_call(
        functools.partial(_dn_scan_kernel, n_steps=n, nb=nb, cps=cps, reverse=reverse, finish=finish),
        grid=(bsz // nb, n),
        in_specs=in_specs,
        out_specs=[tok(DN_WIDTH), st],
        out_shape=[
            jax.ShapeDtypeStruct((bsz, seq, DN_WIDTH), BF16 if finish else F32),
            jax.ShapeDtypeStruct((bsz // nb, n_grp, wg, wg), F32),
        ],
        scratch_shapes=[pltpu.VMEM((n_grp, wg, wg), F32)],
        compiler_params=_cparams(("parallel", "arbitrary")),
        name="dn_scan_bwd" if reverse else "dn_scan_fwd",
    )(*args)


def _dn_scan(qkv, gate, gp, e_fwd, e_bwd, ng_row, gm_mean, s0, nb):
    o_f, s_f = _dn_scan_dir(qkv, gp, e_fwd, s0[0], False, None, nb)
    y, s_b = _dn_scan_dir(qkv, gp, e_bwd, s0[1], True, (o_f, gate, ng_row, gm_mean), nb)
    return y, (s_f, s_b)


NA_ROWS_PER_ITER = 2


def _na_kernel(q_ref, k_ref, v_ref, kc_ref, vc_ref, tbl_ref, o_ref, sc_scr, pc_scr, ow_scr, *,
               rows_per_step, n_rows, win_rows):
    i = pl.program_id(1)
    gw = GRID_W
    lane = lax.broadcasted_iota(jnp.int32, (1, LANES), 1)
    first_head = lane < HEAD_DIM
    scale = HEAD_DIM ** -0.5
    pairs = range(NA_WIDTH // LANES)
    heads = [(p, hh) for p in pairs for hh in range(2)]
    ls = [slice(p * LANES, (p + 1) * LANES) for p in pairs]
    zero = jnp.zeros((1, 1), BF16)

    def one_head(x, hh):
        return jnp.where(first_head if hh == 0 else jnp.logical_not(first_head), x, zero)

    kc = [kc_ref[0, :, ls[p]] for p in pairs]
    q_all = [q_ref[0, :, ls[p]] * scale for p in pairs]
    for u, (p, hh) in enumerate(heads):
        sc_scr[u] = _dot_nt(one_head(q_all[p], hh), kc[p])

    def row_body(it, carry):
        units = []
        for rr in range(NA_ROWS_PER_ITER):
            rq = it * NA_ROWS_PER_ITER + rr
            r = i * rows_per_step + rq
            rs = jnp.clip(r - win_rows // 2, 0, n_rows - win_rows)
            k0 = pl.multiple_of(rs * gw, gw)
            q0 = pl.multiple_of(rq * gw, gw)
            for p in pairs:
                qp = q_ref[0, pl.ds(q0, gw), ls[p]] * scale
                kw = k_ref[0, pl.ds(k0, win_rows * gw), ls[p]]
                vw = v_ref[0, pl.ds(k0, win_rows * gw), ls[p]]
                units.append((q0, r - rs, p, jnp.concatenate([one_head(qp, 0), one_head(qp, 1)], axis=0), kw, vw))
        n_u = range(len(units))
        s_w = [_dot_nt(q2, kw) + tbl_ref[p, var] for q0, var, p, q2, kw, vw in units]
        s_c = [jnp.concatenate([sc_scr[2 * p, pl.ds(q0, gw), :], sc_scr[2 * p + 1, pl.ds(q0, gw), :]], axis=0)
               for q0, var, p, q2, kw, vw in units]
        m = [jnp.maximum(jnp.max(s_w[u], axis=-1, keepdims=True), jnp.max(s_c[u], axis=-1, keepdims=True))
             for u in n_u]
        p_w = [jnp.exp(s_w[u] - m[u]) for u in n_u]
        p_c = [jnp.exp(s_c[u] - m[u]) for u in n_u]
        inv = [1.0 / (jnp.sum(p_w[u], axis=-1, keepdims=True) + jnp.sum(p_c[u], axis=-1, keepdims=True))
               for u in n_u]
        o_w = [_dot(p_w[u].astype(BF16), units[u][5]) * inv[u] for u in n_u]
        for u in n_u:
            q0, var, p = units[u][:3]
            pcn = (p_c[u] * inv[u]).astype(BF16)
            pc_scr[2 * p, pl.ds(q0, gw), :] = pcn[:gw]
            pc_scr[2 * p + 1, pl.ds(q0, gw), :] = pcn[gw:]
            ow_scr[p, pl.ds(q0, gw), :] = jnp.where(first_head, o_w[u][:gw], o_w[u][gw:])
        return carry

    lax.fori_loop(0, rows_per_step // NA_ROWS_PER_ITER, row_body, 0)

    for p in pairs:
        vc = vc_ref[0, :, ls[p]]
        oc = jnp.where(first_head, _dot(pc_scr[2 * p], vc), _dot(pc_scr[2 * p + 1], vc))
        o_ref[0, :, ls[p]] = (ow_scr[p] + oc).astype(BF16)


def _na(zna, zna_ctx, tbl, rows_per_step):
    bsz, seq, _ = zna.shape
    n_ctx = zna_ctx.shape[1]
    n_rows = seq // GRID_W
    win_rows = min(WIN_ROWS, n_rows)
    tq = rows_per_step * GRID_W
    assert rows_per_step % NA_ROWS_PER_ITER == 0
    return pl.pallas_call(
        functools.partial(_na_kernel, rows_per_step=rows_per_step, n_rows=n_rows, win_rows=win_rows),
        grid=(bsz, n_rows // rows_per_step),
        in_specs=[
            pl.BlockSpec((1, tq, NA_WIDTH), lambda b, i: (b, i, 0)),
            pl.BlockSpec((1, seq, NA_WIDTH), lambda b, i: (b, 0, 1)),
            pl.BlockSpec((1, seq, NA_WIDTH), lambda b, i: (b, 0, 2)),
            pl.BlockSpec((1, n_ctx, NA_WIDTH), lambda b, i: (b, 0, 1)),
            pl.BlockSpec((1, n_ctx, NA_WIDTH), lambda b, i: (b, 0, 2)),
            pl.BlockSpec(tbl.shape, lambda b, i: (0, 0, 0, 0)),
        ],
        out_specs=pl.BlockSpec((1, tq, NA_WIDTH), lambda b, i: (b, i, 0)),
        out_shape=jax.ShapeDtypeStruct((bsz, seq, NA_WIDTH), BF16),
        scratch_shapes=[pltpu.VMEM((NA_HEADS, tq, n_ctx), F32), pltpu.VMEM((NA_HEADS, tq, n_ctx), BF16),
                        pltpu.VMEM((NA_WIDTH // LANES, tq, LANES), F32)],
        compiler_params=_cparams(("parallel", "arbitrary")),
        name="nbr_attn",
    )(zna, zna, zna, zna_ctx, zna_ctx, tbl)


def _na_ctx_kernel(q_ref, k_ref, v_ref, o_ref):
    lane = lax.broadcasted_iota(jnp.int32, (1, LANES), 1)
    first_head = lane < HEAD_DIM
    scale = HEAD_DIM ** -0.5
    for p in range(NA_WIDTH // LANES):
        ls = slice(p * LANES, (p + 1) * LANES)
        qp = q_ref[0, :, ls] * scale
        kp = k_ref[0, :, ls]
        vp = v_ref[0, :, ls]
        outs = []
        for hh in range(2):
            msk = first_head if hh == 0 else jnp.logical_not(first_head)
            qm = jnp.where(msk, qp, jnp.zeros_like(qp[:1, :1]))
            s = _dot_nt(qm, kp)
            e = jnp.exp(s - jnp.max(s, axis=-1, keepdims=True))
            den = jnp.sum(e, axis=-1, keepdims=True)
            outs.append(_dot(e.astype(BF16), vp) / den)
        o_ref[0, :, ls] = jnp.where(first_head, outs[0], outs[1]).astype(BF16)


def _na_ctx(zna_ctx):
    bsz, n_ctx, _ = zna_ctx.shape
    col = lambda j: pl.BlockSpec((1, n_ctx, NA_WIDTH), lambda b: (b, 0, j))
    return pl.pallas_call(
        _na_ctx_kernel,
        grid=(bsz,),
        in_specs=[col(0), col(1), col(2)],
        out_specs=pl.BlockSpec((1, n_ctx, NA_WIDTH), lambda b: (b, 0, 0)),
        out_shape=jax.ShapeDtypeStruct((bsz, n_ctx, NA_WIDTH), BF16),
        compiler_params=_cparams(("parallel",)),
        name="ctx_attn",
    )(zna_ctx, zna_ctx, zna_ctx)


def _outproj_kernel(ya_ref, yb_ref, yn_ref, w_ref, x_ref, g_ref, lg_ref, lb_ref, o_ref, *, alpha):
    ycat = jnp.concatenate([ya_ref[0], yb_ref[0], yn_ref[0]], axis=1)
    y = _dot(ycat, w_ref[...])
    s = alpha * x_ref[0] + g_ref[0] * y
    o_ref[0] = _layer_norm_rows(s) * lg_ref[...] + lb_ref[...]


def _out_proj(ya, yb, yn, w_out_b, x, mod3, lg, lb, is_ctx, ctx_row, tm, alpha):
    bsz, seq, d = x.shape
    tok = lambda w: pl.BlockSpec((1, tm, w), lambda b, i: (b, i, 0))
    full = lambda a: pl.BlockSpec(a.shape, lambda b, i: (0,) * a.ndim)
    return pl.pallas_call(
        functools.partial(_outproj_kernel, alpha=alpha),
        grid=(bsz, seq // tm),
        in_specs=[tok(GMLP_WIDTH), tok(DN_WIDTH), tok(NA_WIDTH), full(w_out_b), tok(d),
                  _mod_spec(2, d, is_ctx, ctx_row), full(lg), full(lb)],
        out_specs=tok(d),
        out_shape=jax.ShapeDtypeStruct((bsz, seq, d), F32),
        compiler_params=_cparams(("parallel", "parallel")),
        name="out_proj",
    )(ya, yb, yn, w_out_b, x, mod3, lg, lb)


FFN_CHUNK = 256


def _ffn_kernel(x_ref, xp_ref, xn_ref, sh_ref, sc_ref, g_ref, wup_ref, cw_ref, cb_ref, wd_ref,
                lg_ref, lb_ref, o_ref, gated_ref, *, tm, n_tiles, d_ff, alpha):
    halo = 8
    i = pl.program_id(1)
    has_prev = (i > 0).astype(F32)
    has_next = (i < n_tiles - 1).astype(F32)
    x = x_ref[0]
    sc = 1.0 + sc_ref[0]
    sh = sh_ref[0]
    h_mid = _layer_norm_rows(x) * sc + sh
    h_prev = (_layer_norm_rows(xp_ref[0]) * sc + sh) * has_prev
    h_next = (_layer_norm_rows(xn_ref[0]) * sc + sh) * has_next
    h_ext = jnp.concatenate([h_prev, h_mid, h_next], axis=0).astype(BF16)
    h_b = h_mid.astype(BF16)
    n_ext = tm + 2 * halo
    for j in range(d_ff // FFN_CHUNK):
        cols = slice(j * FFN_CHUNK, (j + 1) * FFN_CHUNK)
        a = _dot(h_ext, wup_ref[:, cols])
        vv = _dot(h_b, wup_ref[:, d_ff + j * FFN_CHUNK:d_ff + (j + 1) * FFN_CHUNK])
        cw = cw_ref[:, cols]
        a_m1 = pltpu.roll(a, 1, 0)[halo:halo + tm]
        a_p1 = pltpu.roll(a, n_ext - 1, 0)[halo:halo + tm]
        conv = a_m1 * cw[0:1] + a[halo:halo + tm] * cw[1:2] + a_p1 * cw[2:3] + cb_ref[:, cols]
        gated_ref[:, cols] = (_silu(conv) * vv).astype(BF16)
    y = _dot(gated_ref[...], wd_ref[...])
    s = alpha * x + g_ref[0] * y
    o_ref[0] = _layer_norm_rows(s) * lg_ref[...] + lb_ref[...]


def _ffn(x, mod3, wup, cw, cb, wd, lg, lb, is_ctx, ctx_row, tm, alpha):
    bsz, seq, d = x.shape
    n_tiles = seq // tm
    d_ff = wd.shape[0]
    assert d_ff % FFN_CHUNK == 0
    hb = tm // 8
    tok = pl.BlockSpec((1, tm, d), lambda b, i: (b, i, 0))
    full = lambda a: pl.BlockSpec(a.shape, lambda b, i: (0,) * a.ndim)
    return pl.pallas_call(
        functools.partial(_ffn_kernel, tm=tm, n_tiles=n_tiles, d_ff=d_ff, alpha=alpha),
        grid=(bsz, n_tiles),
        in_specs=[
            tok,
            pl.BlockSpec((1, 8, d), lambda b, i: (b, jnp.maximum(i * hb - 1, 0), 0)),
            pl.BlockSpec((1, 8, d), lambda b, i: (b, jnp.minimum((i + 1) * hb, seq // 8 - 1), 0)),
            _mod_spec(3, d, is_ctx, ctx_row), _mod_spec(4, d, is_ctx, ctx_row), _mod_spec(5, d, is_ctx, ctx_row),
            full(wup), full(cw), full(cb), full(wd), full(lg), full(lb),
        ],
        out_specs=tok,
        out_shape=jax.ShapeDtypeStruct((bsz, seq, d), F32),
        scratch_shapes=[pltpu.VMEM((tm, d_ff), BF16)],
        compiler_params=_cparams(("parallel", "parallel")),
        name="conv_glu",
    )(x, x, x, mod3, mod3, mod3, wup, cw, cb, wd, lg, lb)


def _rope_tables(seq):
    half = HEAD_DIM // 2
    nf = half // 2
    pos = jnp.arange(seq)
    inv = ROPE_BASE ** (-jnp.arange(nf, dtype=F32) / nf)
    ang_r = (pos // GRID_W)[:, None].astype(F32) * inv
    ang_c = (pos % GRID_W)[:, None].astype(F32) * inv
    cos_h = jnp.concatenate([jnp.cos(ang_r)] * 2 + [jnp.cos(ang_c)] * 2, axis=1)
    sin_h = jnp.concatenate([-jnp.sin(ang_r), jnp.sin(ang_r), -jnp.sin(ang_c), jnp.sin(ang_c)], axis=1)
    reps = LANES // HEAD_DIM
    return jnp.tile(cos_h, (1, reps)), jnp.tile(sin_h, (1, reps))


def _bias_table(rpb, n_rows):
    wr = min(WIN_ROWS, n_rows)
    nh = rpb.shape[0]
    n_dr = 2 * WIN_ROWS - 1
    col = np.arange(GRID_W)
    cs = np.clip(col - WIN_COLS // 2, 0, GRID_W - WIN_COLS)
    valid = (col[None, :] >= cs[:, None]) & (col[None, :] < cs[:, None] + WIN_COLS)
    dc = col[None, :] - col[:, None] + (WIN_COLS - 1)
    onehot = np.zeros((2 * WIN_COLS - 1, GRID_W, GRID_W), np.float32)
    cc, kk = np.nonzero(valid)
    onehot[dc[cc, kk], cc, kk] = 1.0
    t = jnp.einsum('hdm,mck->hdck', rpb.astype(F32), jnp.asarray(onehot), precision=lax.Precision.HIGHEST)
    t = jnp.where(valid[None, None], t, NEG_BIG)
    per_var = []
    for var in range(wr):
        lo = WIN_ROWS - 1 - var
        per_var.append(t[:, lo:lo + wr].transpose(0, 2, 1, 3).reshape(nh // 2, 2 * GRID_W, wr * GRID_W))
    return jnp.stack(per_var, axis=1)


def _gate_mats():
    expand = np.zeros((2, LANES, 2 * DN_WIDTH), np.float32)
    for dd in range(2):
        for h in range(DN_HEADS):
            for c in range(GATE_COPIES):
                expand[dd, c * GATE_COPY_STRIDE + dd * DN_HEADS + h, h * HEAD_DIM:(h + 1) * HEAD_DIM] = 1.0
            for c in range(2):
                src = c * GATE_COPY_STRIDE + 2 * DN_HEADS + dd * DN_HEADS + h
                expand[dd, src, DN_WIDTH + h * HEAD_DIM:DN_WIDTH + (h + 1) * HEAD_DIM] = 1.0
    return jnp.asarray(expand[0], BF16), jnp.asarray(expand[1], BF16)


def _gate_row(v):
    row = jnp.zeros((1, LANES), F32)
    for c in range(GATE_COPIES):
        row = row.at[0, c * GATE_COPY_STRIDE:c * GATE_COPY_STRIDE + v.shape[0]].set(v.astype(F32))
    return row


def _token_tile(seq):
    return 512 if seq % 512 == 0 else 256


def kernel(x, c, ctx, c_ctx, w_ada, b_ada, w_in, gmlp_ln_g, gmlp_ln_b, gmlp_ws, gmlp_bs, dn_conv, dn_a_log,
           dn_dt_bias, dn_norm_g, na_rpb, w_out, ln1_g, ln1_b, ffn_up, ffn_conv, ffn_conv_b, ffn_down,
           ln2_g, ln2_b):
    depth = w_ada.shape[0]
    bsz, seq, d = x.shape
    n_ctx = ctx.shape[1]
    d_ff = ffn_down.shape[1]
    alpha = (2 * depth) ** 0.25
    ctx_row = bsz
    tm = _token_tile(seq)
    tm_c = _token_tile(n_ctx)

    mod = _mod_all(c, c_ctx, w_ada, b_ada)
    rope_tabs = _rope_tables(seq)
    n_pairs = NA_WIDTH // LANES
    tbl_all = _bias_table(na_rpb.reshape((depth * NA_HEADS,) + na_rpb.shape[2:]), seq // GRID_W)
    e_fwd, e_bwd = _gate_mats()
    gm384_sum = jnp.asarray(_block_diag_np(LANES // HEAD_DIM, HEAD_DIM, HEAD_DIM, 1.0), BF16)
    gm384_mean = jnp.asarray(_block_diag_np(LANES // HEAD_DIM, HEAD_DIM, HEAD_DIM, 1.0 / HEAD_DIM), BF16)
    nb = 4 if bsz % 4 == 0 else (2 if bsz % 2 == 0 else 1)
    wg = DN_GROUP_HEADS * HEAD_DIM
    s_zero = jnp.zeros((bsz // nb, nb * (DN_WIDTH // wg), wg, wg), F32)
    o1 = GMLP_COLS
    o2 = o1 + DN_MAIN_COLS
    o3 = o2 + DN_AB_COLS

    for l in range(depth):
        ctx_out = l < depth - 1
        mod3 = mod[l].reshape(8, 1, 6 * d)
        wl = w_in[l]
        ab_cols = [wl[:, o2:o3], jnp.zeros((d, GATE_COPY_STRIDE - DN_AB_COLS), F32)] * GATE_COPIES
        ab_cols.append(jnp.zeros((d, LANES - GATE_COPIES * GATE_COPY_STRIDE), F32))
        w_in_p = jnp.concatenate([wl[:, :o2], wl[:, o3:]] + ab_cols, axis=1).astype(BF16)
        lng = gmlp_ln_g[l].reshape(1, GMLP_WIDTH)
        lnb = gmlp_ln_b[l].reshape(1, GMLP_WIDTH)
        wsp = gmlp_ws[l].transpose(1, 0, 2).reshape(GMLP_CHUNK, GMLP_HEADS * GMLP_CHUNK).astype(BF16)
        bsx = jnp.repeat(gmlp_bs[l].T, HEAD_DIM, axis=1)
        aexp_row = _gate_row(jnp.exp(dn_a_log[l].astype(F32)).reshape(-1))
        dtb_row = _gate_row(dn_dt_bias[l].reshape(-1))
        ng_row = jnp.tile(dn_norm_g[l].astype(F32), DN_HEADS).reshape(1, DN_WIDTH)
        tbl = tbl_all[l * n_pairs:(l + 1) * n_pairs]
        w_out_b = w_out[l].astype(BF16)
        lg1 = ln1_g[l].reshape(1, d)
        lb1 = ln1_b[l].reshape(1, d)
        lg2 = ln2_g[l].reshape(1, d)
        lb2 = ln2_b[l].reshape(1, d)
        wup = ffn_up[l].astype(BF16)
        wd = ffn_down[l].astype(BF16)
        cw = ffn_conv[l]
        cb = ffn_conv_b[l].reshape(1, d_ff)

        prep = (dn_conv[l], aexp_row, dtb_row, gm384_sum, (gm384_mean, lng, lnb, wsp, bsx))
        ya_c, qkv_c, gate_c, gp_c, zna_c = _in_proj(ctx, mod3, w_in_p, *prep, None, True, ctx_row, tm_c)
        yb_c, s_ctx = _dn_scan(qkv_c, gate_c, gp_c, e_fwd, e_bwd, ng_row, gm384_mean, (s_zero, s_zero), nb)

        ya, qkv, gate, gp, zna = _in_proj(x, mod3, w_in_p, *prep, rope_tabs, False, ctx_row, tm)
        yb, _ = _dn_scan(qkv, gate, gp, e_fwd, e_bwd, ng_row, gm384_mean, s_ctx, nb)
        yn = _na(zna, zna_c, tbl, 8)
        x1 = _out_proj(ya, yb, yn, w_out_b, x, mod3, lg1, lb1, False, ctx_row, tm, alpha)
        x_next = _ffn(x1, mod3, wup, cw, cb, wd, lg2, lb2, False, ctx_row, tm, alpha)

        if ctx_out:
            yn_c = _na_ctx(zna_c)
            c1 = _out_proj(ya_c, yb_c, yn_c, w_out_b, ctx, mod3, lg1, lb1, True, ctx_row, tm_c, alpha)
            ctx = _ffn(c1, mod3, wup, cw, cb, wd, lg2, lb2, True, ctx_row, tm_c, alpha)
        x = x_next
    return x
```

```python
import functools
import math

import numpy as np
import jax
import jax.numpy as jnp
from jax import lax
from jax.experimental import pallas as pl
from jax.experimental.pallas import tpu as pltpu

F32 = jnp.float32
BF16 = jnp.bfloat16

HEAD_DIM = 64
GRID_W = 64
GMLP_HEADS = 4
GMLP_WIDTH = GMLP_HEADS * HEAD_DIM
GMLP_CHUNK = 128
DN_HEADS = 6
DN_WIDTH = DN_HEADS * HEAD_DIM
NA_HEADS = 6
NA_WIDTH = NA_HEADS * HEAD_DIM
DELTA_CHUNK = 64
WIN_ROWS = 8
WIN_COLS = 16
ROPE_BASE = 10000.0
EPS = 1e-6
NEG_BIG = -1e30
LANES = 128
VMEM_LIMIT = 56 * 1024 * 1024

GMLP_COLS = 2 * GMLP_WIDTH
DN_MAIN_COLS = 4 * DN_WIDTH
DN_AB_COLS = 4 * DN_HEADS
NA_COLS = 3 * NA_WIDTH


def _cparams(sem):
    return pltpu.CompilerParams(dimension_semantics=sem, vmem_limit_bytes=VMEM_LIMIT)


def _split2(x):
    hi = x.astype(BF16)
    lo = (x - hi.astype(F32)).astype(BF16)
    return hi, lo


def _split3(x):
    a = x.astype(BF16)
    r = x - a.astype(F32)
    b = r.astype(BF16)
    c = (r - b.astype(F32)).astype(BF16)
    return a, b, c


def _dot(a, b):
    return jnp.dot(a, b, preferred_element_type=F32)


def _dot_nt(a, b):
    return lax.dot_general(a, b, (((1,), (1,)), ((), ())), preferred_element_type=F32)


def _dot_tn(a, b):
    return lax.dot_general(a, b, (((0,), (0,)), ((), ())), preferred_element_type=F32)


def _dot_x3(x, m):
    a, b, c = _split3(x)
    return _dot(a, m) + _dot(b, m) + _dot(c, m)


def _dot_x2(x, m):
    a, b = _split2(x)
    return _dot(a, m) + _dot(b, m)


def _head_sums(x, pair_mat):
    parts = [_dot(x[:, p * LANES:(p + 1) * LANES].astype(BF16), pair_mat) for p in range(x.shape[1] // LANES)]
    return jnp.concatenate(parts, axis=1)


def _silu(x):
    return x * jax.nn.sigmoid(x)


def _layer_norm_rows(x):
    mu = jnp.mean(x, axis=-1, keepdims=True)
    xc = x - mu
    var = jnp.mean(xc * xc, axis=-1, keepdims=True)
    return xc * lax.rsqrt(var + EPS)


def _block_diag_np(n_blocks, rows, cols, value):
    m = np.zeros((n_blocks * rows, n_blocks * cols), np.float32)
    for g in range(n_blocks):
        m[g * rows:(g + 1) * rows, g * cols:(g + 1) * cols] = value
    return m


def _mod_kernel(c_ref, w_ref, b_ref, o_ref):
    a = _silu(c_ref[...])
    w = w_ref[0]
    a1, a2 = _split2(a)
    w1, w2 = _split2(w)
    o_ref[0] = _dot(a1, w1) + _dot(a1, w2) + _dot(a2, w1) + b_ref[0]


def _mod_all(c, c_ctx, w_ada, b_ada):
    n_layers, d, n6 = w_ada.shape
    bsz = c.shape[0]
    rows = jnp.concatenate([c, c_ctx[None, :], jnp.zeros((8 - bsz - 1, d), F32)], axis=0)
    tn = 1536
    return pl.pallas_call(
        _mod_kernel,
        grid=(n_layers, n6 // tn),
        in_specs=[
            pl.BlockSpec((8, d), lambda l, j: (0, 0)),
            pl.BlockSpec((1, d, tn), lambda l, j: (l, 0, j)),
            pl.BlockSpec((1, 1, tn), lambda l, j: (l, 0, j)),
        ],
        out_specs=pl.BlockSpec((1, 8, tn), lambda l, j: (l, 0, j)),
        out_shape=jax.ShapeDtypeStruct((n_layers, 8, n6), F32),
        compiler_params=_cparams(("parallel", "parallel")),
        name="adaln_mod",
    )(rows, w_ada, b_ada.reshape(n_layers, 1, n6))


def _mod_spec(chunk, d, is_ctx, ctx_row):
    if is_ctx:
        return pl.BlockSpec((1, 1, d), lambda b, i: (ctx_row, 0, chunk))
    return pl.BlockSpec((1, 1, d), lambda b, i: (b, 0, chunk))


def _inproj_kernel(*refs, tm, n_tiles, rotary):
    if rotary:
        (x_ref, xp_ref, xn_ref, sh_ref, sc_ref, w_ref, cw_ref, aexp_ref, dtb_ref, g_ref, gmm_ref, lng_ref, lnb_ref,
         wsp_ref, bsx_ref, cos_ref, sin_ref, ya_ref, qkv_ref, gate_ref, gp_ref, zna_ref) = refs
    else:
        (x_ref, xp_ref, xn_ref, sh_ref, sc_ref, w_ref, cw_ref, aexp_ref, dtb_ref, g_ref, gmm_ref, lng_ref, lnb_ref,
         wsp_ref, bsx_ref, ya_ref, qkv_ref, gate_ref, gp_ref, zna_ref) = refs
    halo = 8
    i = pl.program_id(1)
    has_prev = (i > 0).astype(F32)
    has_next = (i < n_tiles - 1).astype(F32)
    sc = 1.0 + sc_ref[0]
    sh = sh_ref[0]
    h_mid = _layer_norm_rows(x_ref[0]) * sc + sh
    h_prev = (_layer_norm_rows(xp_ref[0]) * sc + sh) * has_prev
    h_next = (_layer_norm_rows(xn_ref[0]) * sc + sh) * has_next
    h_ext = jnp.concatenate([h_prev, h_mid, h_next], axis=0).astype(BF16)
    h = h_mid.astype(BF16)
    o_q = GMLP_COLS
    o_g = o_q + 3 * DN_WIDTH
    o_n = o_g + DN_WIDTH
    o_ab = o_n + NA_COLS

    x_ext = _dot(h_ext, w_ref[:, o_q:o_g])
    ab = _dot(h, w_ref[:, o_ab:o_ab + LANES])
    n_ext = tm + 2 * halo
    x_m1 = pltpu.roll(x_ext, 1, 0)[halo:halo + tm]
    x_p1 = pltpu.roll(x_ext, n_ext - 1, 0)[halo:halo + tm]
    x_0 = x_ext[halo:halo + tm]
    cw = cw_ref[...]
    y = _silu(x_m1 * cw[0:1] + x_0 * cw[1:2] + x_p1 * cw[2:3])
    ya_ref[0] = _gmlp_mix(_dot(h, w_ref[:, 0:o_q]), gmm_ref[...], lng_ref[...], lnb_ref[...], wsp_ref[...],
                          bsx_ref[...]).astype(BF16)
    q = y[:, 0:DN_WIDTH]
    k = y[:, DN_WIDTH:2 * DN_WIDTH]
    v = y[:, 2 * DN_WIDTH:3 * DN_WIDTH]
    gmat = g_ref[...]
    q = q * lax.rsqrt(_head_sums(q * q, gmat) + EPS)
    k = k * lax.rsqrt(_head_sums(k * k, gmat) + EPS)
    gate_ref[0] = _dot(h, w_ref[:, o_g:o_n]).astype(BF16)
    if rotary:
        cos2 = cos_ref[...]
        sin2 = sin_ref[...]
        lane = lax.broadcasted_iota(jnp.int32, (tm, LANES), 1)
        first = (lane % 32) < 16

        def rope(t):
            parts = []
            for p in range(DN_WIDTH // LANES):
                tp = t[:, p * LANES:(p + 1) * LANES]
                partner = jnp.where(first, pltpu.roll(tp, LANES - 16, 1), pltpu.roll(tp, 16, 1))
                parts.append(tp * cos2 + partner * sin2)
            return jnp.concatenate(parts, axis=1)

        q = rope(q)
        k = rope(k)
    q = q * (HEAD_DIM ** -0.5)
    qkv_ref[0, :, 0:DN_WIDTH] = q.astype(BF16)
    qkv_ref[0, :, DN_WIDTH:2 * DN_WIDTH] = k.astype(BF16)
    qkv_ref[0, :, 2 * DN_WIDTH:3 * DN_WIDTH] = v.astype(BF16)
    zna_ref[0] = _dot(h, w_ref[:, o_n:o_ab]).astype(BF16)

    lane_ab = lax.broadcasted_iota(jnp.int32, ab.shape, 1)
    in_copy = lane_ab % GATE_COPY_STRIDE
    copy = lane_ab // GATE_COPY_STRIDE
    xs = ab + dtb_ref[...]
    softplus = jnp.maximum(xs, 0.0) + jnp.log1p(jnp.exp(-jnp.abs(xs)))
    log_decay = -aexp_ref[...] * softplus
    beta = jax.nn.sigmoid(ab)
    blk = min(tm, 4 * DELTA_CHUNK)
    ti = lax.broadcasted_iota(jnp.int32, (blk, blk), 0)
    tj = lax.broadcasted_iota(jnp.int32, (blk, blk), 1)
    same_chunk = (ti // DELTA_CHUNK) == (tj // DELTA_CHUNK)
    l_fwd = jnp.where(same_chunk, jnp.where(ti >= tj, 1.0, 0.0), 0.0).astype(BF16)
    l_bwd = jnp.where(same_chunk, jnp.where(ti <= tj, 1.0, 0.0), 0.0).astype(BF16)
    g1, g2 = _split2(log_decay)
    gc_f, gc_b = [], []
    for rb in range(tm // blk):
        rows = slice(rb * blk, (rb + 1) * blk)
        gc_f.append(_dot(l_fwd, g1[rows]) + _dot(l_fwd, g2[rows]))
        gc_b.append(_dot(l_bwd, g1[rows]) + _dot(l_bwd, g2[rows]))
    gc = jnp.where(in_copy < DN_HEADS, jnp.concatenate(gc_f, axis=0), jnp.concatenate(gc_b, axis=0))
    c1, c2, c3 = _split3(gc)
    b1, b2 = _split2(beta)
    dec_piece = jnp.where(copy == 0, c1, jnp.where(copy == 1, c2, c3)).astype(F32)
    beta_piece = jnp.where(copy == 0, b1, jnp.where(copy == 1, b2, jnp.zeros_like(b2))).astype(F32)
    packed = jnp.where(in_copy < 2 * DN_HEADS, dec_piece, beta_piece)
    gp_ref[0] = packed.astype(BF16)


def _in_proj(x, mod3, w_in_p, conv_w, aexp_row, dtb_row, gm384, gmlp_args, rope_tabs, is_ctx, ctx_row, tm):
    bsz, seq, d = x.shape
    n_tiles = seq // tm
    hb = tm // 8
    rotary = rope_tabs is not None
    tok = lambda w: pl.BlockSpec((1, tm, w), lambda b, i: (b, i, 0))
    full = lambda a: pl.BlockSpec(a.shape, lambda b, i: (0,) * a.ndim)
    in_specs = [
        tok(d),
        pl.BlockSpec((1, 8, d), lambda b, i: (b, jnp.maximum(i * hb - 1, 0), 0)),
        pl.BlockSpec((1, 8, d), lambda b, i: (b, jnp.minimum((i + 1) * hb, seq // 8 - 1), 0)),
        _mod_spec(0, d, is_ctx, ctx_row),
        _mod_spec(1, d, is_ctx, ctx_row),
        full(w_in_p), full(conv_w), full(aexp_row), full(dtb_row), full(gm384),
    ] + [full(a) for a in gmlp_args]
    args = [x, x, x, mod3, mod3, w_in_p, conv_w, aexp_row, dtb_row, gm384] + list(gmlp_args)
    if rotary:
        in_specs += [pl.BlockSpec((tm, LANES), lambda b, i: (i, 0))] * 2
        args += list(rope_tabs)
    widths = [GMLP_WIDTH, 3 * DN_WIDTH, DN_WIDTH, LANES, NA_COLS]
    return pl.pallas_call(
        functools.partial(_inproj_kernel, tm=tm, n_tiles=n_tiles, rotary=rotary),
        grid=(bsz, n_tiles),
        in_specs=in_specs,
        out_specs=[tok(w) for w in widths],
        out_shape=[jax.ShapeDtypeStruct((bsz, seq, w), BF16) for w in widths],
        compiler_params=_cparams(("parallel", "parallel")),
        name="in_proj",
    )(*args)


def _gmlp_mix(z, gmat, lng, lnb, wsp, bsx):
    tm = z.shape[0]
    z = jax.nn.gelu(z)
    u = z[:, :GMLP_WIDTH]
    v = z[:, GMLP_WIDTH:]
    vc = v - _head_sums(v, gmat)
    var = _head_sums(vc * vc, gmat)
    vn = (vc * lax.rsqrt(var + EPS) * lng + lnb).astype(BF16)
    nrow = GMLP_HEADS * GMLP_CHUNK
    ri = lax.broadcasted_iota(jnp.int32, (nrow, GMLP_WIDTH), 0) // GMLP_CHUNK
    ci = lax.broadcasted_iota(jnp.int32, (nrow, GMLP_WIDTH), 1) // HEAD_DIM
    diag = ri == ci
    outs = []
    for n in range(tm // GMLP_CHUNK):
        rows = slice(n * GMLP_CHUNK, (n + 1) * GMLP_CHUNK)
        vch = vn[rows]
        bd = jnp.where(diag, jnp.concatenate([vch] * GMLP_HEADS, axis=0), jnp.zeros_like(vch[:1, :1]))
        outs.append(u[rows] * (_dot(wsp, bd) + bsx))
    return jnp.concatenate(outs, axis=0)


GATE_COPIES = 3
GATE_COPY_STRIDE = 32
INV_BLOCK = 8
DN_GROUP_HEADS = 2
DN_CHUNKS_PER_STEP = 8


def _dn_intra(q, k, v, gcx, bx, masks):
    cs = DELTA_CHUNK
    causal, strict, eye, last_row, diag, diag_mul, inv_diag, inv_levels = masks
    grp = range(len(q))

    def bd(x):
        return jnp.concatenate([x.astype(BF16)] * DN_GROUP_HEADS, axis=0) * diag_mul

    qf = [q[g].astype(F32) for g in grp]
    kf = [k[g].astype(F32) for g in grp]
    vf = [v[g].astype(F32) for g in grp]
    gc_t = [jnp.sum(jnp.where(eye, gcx[g], 0.0), axis=0, keepdims=True) for g in grp]
    dm = [jnp.where(causal, jnp.exp(jnp.where(causal, gcx[g] - gc_t[g], 0.0)), 0.0) for g in grp]
    dms = [jnp.where(strict, dm[g], 0.0) for g in grp]
    gl = [jnp.sum(jnp.where(last_row, gcx[g], 0.0), axis=0, keepdims=True) for g in grp]
    egx = [jnp.exp(gcx[g]) for g in grp]
    kdx = [jnp.exp(gl[g] - gcx[g]) for g in grp]
    kb = [kf[g] * bx[g] for g in grp]
    vb = [vf[g] * bx[g] for g in grp]
    bk = [bd(k[g]) for g in grp]
    lhs = [jnp.concatenate([kb[g].astype(BF16), q[g]], axis=0) for g in grp]
    p1 = [_dot_nt(lhs[g], bk[g]) for g in grp]
    a_mat = [p1[g][:cs] * dms[g] for g in grp]
    attn = [p1[g][cs:] * dm[g] for g in grp]

    eye_f = jnp.where(eye, 1.0, 0.0)
    a_d = [a_mat[g] * inv_diag for g in grp]
    t_mat = [eye_f - a_d[g] for g in grp]
    b_pow = [_dot(a_d[g].astype(BF16), bd(a_d[g])) for g in grp]
    pp = [_dot(jnp.concatenate([t_mat[g], b_pow[g]], axis=0).astype(BF16), bd(b_pow[g])) for g in grp]
    t_mat = [t_mat[g] + pp[g][:cs] for g in grp]
    t_mat = [t_mat[g] + _dot(t_mat[g].astype(BF16), bd(pp[g][cs:])) for g in grp]
    for coupling in inv_levels:
        x = [_dot(t_mat[g].astype(BF16), bd(a_mat[g] * coupling)) for g in grp]
        t_mat = [t_mat[g] - _dot(x[g].astype(BF16), bd(t_mat[g])) for g in grp]

    tb = [t_mat[g].astype(BF16) for g in grp]
    u = [_dot(tb[g], bd(vb[g])) for g in grp]
    wk = [_dot(tb[g], bd(kb[g] * egx[g])) for g in grp]
    lhs2 = [jnp.concatenate([wk[g], qf[g] * egx[g]], axis=0).astype(BF16) for g in grp]
    kd = [(kf[g] * kdx[g]).astype(BF16) for g in grp]
    egl = [jnp.exp(gl[g]) for g in grp]
    attn_b = [attn[g].astype(BF16) for g in grp]
    return u, lhs2, attn_b, kd, egl


def _dn_inter(intra, st, masks):
    cs = DELTA_CHUNK
    diag, diag_mul = masks[4], masks[5]
    u, lhs2, attn_b, kd, egl = intra
    grp = range(len(u))
    p2 = [_dot(lhs2[g], st[g].astype(BF16)) for g in grp]
    v_new = [u[g] - p2[g][:cs] for g in grp]
    vnb = [v_new[g].astype(BF16) for g in grp]
    bdv = [jnp.concatenate([vnb[g]] * DN_GROUP_HEADS, axis=0) * diag_mul for g in grp]
    o = [p2[g][cs:] + _dot(attn_b[g], bdv[g]) for g in grp]
    st_new = [st[g] * egl[g] + jnp.where(diag, _dot_tn(kd[g], vnb[g]), 0.0) for g in grp]
    return o, st_new


def _dn_masks(reverse):
    cs = DELTA_CHUNK
    wg = DN_GROUP_HEADS * HEAD_DIM
    i3 = lax.broadcasted_iota(jnp.int32, (cs, wg), 0)
    j3 = lax.broadcasted_iota(jnp.int32, (cs, wg), 1) % HEAD_DIM
    if reverse:
        causal, strict, last_row = i3 <= j3, i3 < j3, i3 == 0
    else:
        causal, strict, last_row = i3 >= j3, i3 > j3, i3 == cs - 1
    ra = lax.broadcasted_iota(jnp.int32, (wg, wg), 0) // HEAD_DIM
    ca = lax.broadcasted_iota(jnp.int32, (wg, wg), 1) // HEAD_DIM
    diag = ra == ca
    diag_mul = jnp.where(diag, 1.0, 0.0).astype(BF16)
    inv_diag = jnp.where(i3 // INV_BLOCK == j3 // INV_BLOCK, 1.0, 0.0)
    inv_levels = []
    size = INV_BLOCK
    while size < cs:
        same_pair = i3 // (2 * size) == j3 // (2 * size)
        inv_levels.append(jnp.where(same_pair, jnp.where(i3 // size == j3 // size, 0.0, 1.0), 0.0))
        size *= 2
    return causal, strict, i3 == j3, last_row, diag, diag_mul, inv_diag, inv_levels


def _dn_scan_kernel(*refs, n_steps, nb, cps, reverse, finish):
    if finish:
        (qkv_ref, gp_ref, e_ref, s0_ref, of_ref, gate_ref, ng_ref, gm_ref, out_ref, s_out_ref, st_scr) = refs
    else:
        (qkv_ref, gp_ref, e_ref, s0_ref, out_ref, s_out_ref, st_scr) = refs
    w = DN_WIDTH
    wg = DN_GROUP_HEADS * HEAD_DIM
    gpc = w // wg
    n_grp = nb * gpc
    i = pl.program_id(1)

    @pl.when(i == 0)
    def _():
        st_scr[...] = s0_ref[...]

    masks = _dn_masks(reverse)
    e_comb = e_ref[...]
    cs = DELTA_CHUNK
    order = list(range(cps - 1, -1, -1)) if reverse else list(range(cps))

    def groups(arrs, off):
        return [a[:, off + p * wg:off + (p + 1) * wg] for a in arrs for p in range(gpc)]

    qs, ks, vs, gs, bs = [], [], [], [], []
    step_rows = cps * cs
    ex_all = _dot(gp_ref[...].reshape(nb * step_rows, LANES), e_comb)
    for c in order:
        rows = slice(c * cs, (c + 1) * cs)
        qkv = [qkv_ref[bb, rows, :] for bb in range(nb)]
        ex = [ex_all[bb * step_rows + c * cs:bb * step_rows + (c + 1) * cs] for bb in range(nb)]
        qs += groups(qkv, 0)
        ks += groups(qkv, w)
        vs += groups(qkv, 2 * w)
        gs += groups(ex, 0)
        bs += groups(ex, w)
    intra = _dn_intra(qs, ks, vs, gs, bs, masks)
    st = [st_scr[g] for g in range(n_grp)]
    for idx, c in enumerate(order):
        rows = slice(c * cs, (c + 1) * cs)
        part = [t[idx * n_grp:(idx + 1) * n_grp] for t in intra]
        o_g, st = _dn_inter(part, st, masks)
        for bb in range(nb):
            o = jnp.concatenate(o_g[bb * gpc:(bb + 1) * gpc], axis=1)
            if finish:
                ot = of_ref[bb, rows, :] + o
                ms = _head_sums(ot * ot, gm_ref[...])
                gate = gate_ref[bb, rows, :].astype(F32)
                out_ref[bb, rows, :] = (ot * lax.rsqrt(ms + EPS) * ng_ref[...] * _silu(gate)).astype(BF16)
            else:
                out_ref[bb, rows, :] = o
    for g in range(n_grp):
        st_scr[g] = st[g]

    @pl.when(i == n_steps - 1)
    def _():
        s_out_ref[...] = st_scr[...]


def _dn_scan_dir(qkv, gp, e_comb, s0, reverse, finish_args, nb):
    bsz, seq, _ = qkv.shape
    cps = min(DN_CHUNKS_PER_STEP, seq // DELTA_CHUNK)
    cs = cps * DELTA_CHUNK
    assert seq % cs == 0
    n = seq // cs
    wg = DN_GROUP_HEADS * HEAD_DIM
    n_grp = nb * (DN_WIDTH // wg)
    finish = finish_args is not None
    cidx = (lambda i: n - 1 - i) if reverse else (lambda i: i)
    full = lambda a: pl.BlockSpec(a.shape, lambda g, i: (0,) * a.ndim)
    tok = lambda wdt, col=0: pl.BlockSpec((nb, cs, wdt), lambda g, i: (g, cidx(i), col))
    st = pl.BlockSpec((None, n_grp, wg, wg), lambda g, i: (g, 0, 0, 0))
    in_specs = [tok(3 * DN_WIDTH), tok(LANES), full(e_comb), st]
    args = [qkv, gp, e_comb, s0]
    if finish:
        o_other, gate, ng_row, gm_mean = finish_args
        in_specs += [tok(DN_WIDTH), tok(DN_WIDTH), full(ng_row), full(gm_mean)]
        args += [o_other, gate, ng_row, gm_mean]
    return pl.pallas_call(
        functools.partial(_dn_scan_kernel, n_steps=n, nb=nb, cps=cps, reverse=reverse, finish=finish),
        grid=(bsz // nb, n),
        in_specs=in_specs,
        out_specs=[tok(DN_WIDTH), st],
        out_shape=[
            jax.ShapeDtypeStruct((bsz, seq, DN_WIDTH), BF16 if finish else F32),
            jax.ShapeDtypeStruct((bsz // nb, n_grp, wg, wg), F32),
        ],
        scratch_shapes=[pltpu.VMEM((n_grp, wg, wg), F32)],
        compiler_params=_cparams(("parallel", "arbitrary")),
        name="dn_scan_bwd" if reverse else "dn_scan_fwd",
    )(*args)


def _dn_scan(qkv, gate, gp, e_fwd, e_bwd, ng_row, gm_mean, s0, nb):
    o_f, s_f = _dn_scan_dir(qkv, gp, e_fwd, s0[0], False, None, nb)
    y, s_b = _dn_scan_dir(qkv, gp, e_bwd, s0[1], True, (o_f, gate, ng_row, gm_mean), nb)
    return y, (s_f, s_b)


NA_ROWS_PER_ITER = 2


def _na_kernel(q_ref, k_ref, v_ref, kc_ref, vc_ref, tbl_ref, o_ref, sc_scr, pc_scr, ow_scr, *,
               rows_per_step, n_rows, win_rows):
    i = pl.program_id(1)
    gw = GRID_W
    lane = lax.broadcasted_iota(jnp.int32, (1, LANES), 1)
    first_head = lane < HEAD_DIM
    scale = HEAD_DIM ** -0.5
    pairs = range(NA_WIDTH // LANES)
    heads = [(p, hh) for p in pairs for hh in range(2)]
    ls = [slice(p * LANES, (p + 1) * LANES) for p in pairs]
    zero = jnp.zeros((1, 1), BF16)

    def one_head(x, hh):
        return jnp.where(first_head if hh == 0 else jnp.logical_not(first_head), x, zero)

    kc = [kc_ref[0, :, ls[p]] for p in pairs]
    q_all = [q_ref[0, :, ls[p]] * scale for p in pairs]
    for u, (p, hh) in enumerate(heads):
        sc_scr[u] = _dot_nt(one_head(q_all[p], hh), kc[p])

    def row_body(it, carry):
        units = []
        for rr in range(NA_ROWS_PER_ITER):
            rq = it * NA_ROWS_PER_ITER + rr
            r = i * rows_per_step + rq
            rs = jnp.clip(r - win_rows // 2, 0, n_rows - win_rows)
            k0 = pl.multiple_of(rs * gw, gw)
            q0 = pl.multiple_of(rq * gw, gw)
            for p in pairs:
                qp = q_ref[0, pl.ds(q0, gw), ls[p]] * scale
                kw = k_ref[0, pl.ds(k0, win_rows * gw), ls[p]]
                vw = v_ref[0, pl.ds(k0, win_rows * gw), ls[p]]
                units.append((q0, r - rs, p, jnp.concatenate([one_head(qp, 0), one_head(qp, 1)], axis=0), kw, vw))
        n_u = range(len(units))
        s_w = [_dot_nt(q2, kw) + tbl_ref[p, var] for q0, var, p, q2, kw, vw in units]
        s_c = [jnp.concatenate([sc_scr[2 * p, pl.ds(q0, gw), :], sc_scr[2 * p + 1, pl.ds(q0, gw), :]], axis=0)
               for q0, var, p, q2, kw, vw in units]
        m = [jnp.maximum(jnp.max(s_w[u], axis=-1, keepdims=True), jnp.max(s_c[u], axis=-1, keepdims=True))
             for u in n_u]
        p_w = [jnp.exp(s_w[u] - m[u]) for u in n_u]
        p_c = [jnp.exp(s_c[u] - m[u]) for u in n_u]
        inv = [1.0 / (jnp.sum(p_w[u], axis=-1, keepdims=True) + jnp.sum(p_c[u], axis=-1, keepdims=True))
               for u in n_u]
        o_w = [_dot(p_w[u].astype(BF16), units[u][5]) * inv[u] for u in n_u]
        for u in n_u:
            q0, var, p = units[u][:3]
            pcn = (p_c[u] * inv[u]).astype(BF16)
            pc_scr[2 * p, pl.ds(q0, gw), :] = pcn[:gw]
            pc_scr[2 * p + 1, pl.ds(q0, gw), :] = pcn[gw:]
            ow_scr[p, pl.ds(q0, gw), :] = jnp.where(first_head, o_w[u][:gw], o_w[u][gw:])
        return carry

    lax.fori_loop(0, rows_per_step // NA_ROWS_PER_ITER, row_body, 0)

    for p in pairs:
        vc = vc_ref[0, :, ls[p]]
        oc = jnp.where(first_head, _dot(pc_scr[2 * p], vc), _dot(pc_scr[2 * p + 1], vc))
        o_ref[0, :, ls[p]] = (ow_scr[p] + oc).astype(BF16)


def _na(zna, zna_ctx, tbl, rows_per_step):
    bsz, seq, _ = zna.shape
    n_ctx = zna_ctx.shape[1]
    n_rows = seq // GRID_W
    win_rows = min(WIN_ROWS, n_rows)
    tq = rows_per_step * GRID_W
    assert rows_per_step % NA_ROWS_PER_ITER == 0
    return pl.pallas_call(
        functools.partial(_na_kernel, rows_per_step=rows_per_step, n_rows=n_rows, win_rows=win_rows),
        grid=(bsz, n_rows // rows_per_step),
        in_specs=[
            pl.BlockSpec((1, tq, NA_WIDTH), lambda b, i: (b, i, 0)),
            pl.BlockSpec((1, seq, NA_WIDTH), lambda b, i: (b, 0, 1)),
            pl.BlockSpec((1, seq, NA_WIDTH), lambda b, i: (b, 0, 2)),
            pl.BlockSpec((1, n_ctx, NA_WIDTH), lambda b, i: (b, 0, 1)),
            pl.BlockSpec((1, n_ctx, NA_WIDTH), lambda b, i: (b, 0, 2)),
            pl.BlockSpec(tbl.shape, lambda b, i: (0, 0, 0, 0)),
        ],
        out_specs=pl.BlockSpec((1, tq, NA_WIDTH), lambda b, i: (b, i, 0)),
        out_shape=jax.ShapeDtypeStruct((bsz, seq, NA_WIDTH), BF16),
        scratch_shapes=[pltpu.VMEM((NA_HEADS, tq, n_ctx), F32), pltpu.VMEM((NA_HEADS, tq, n_ctx), BF16),
                        pltpu.VMEM((NA_WIDTH // LANES, tq, LANES), F32)],
        compiler_params=_cparams(("parallel", "arbitrary")),
        name="nbr_attn",
    )(zna, zna, zna, zna_ctx, zna_ctx, tbl)


def _na_ctx_kernel(q_ref, k_ref, v_ref, o_ref):
    lane = lax.broadcasted_iota(jnp.int32, (1, LANES), 1)
    first_head = lane < HEAD_DIM
    scale = HEAD_DIM ** -0.5
    for p in range(NA_WIDTH // LANES):
        ls = slice(p * LANES, (p + 1) * LANES)
        qp = q_ref[0, :, ls] * scale
        kp = k_ref[0, :, ls]
        vp = v_ref[0, :, ls]
        outs = []
        for hh in range(2):
            msk = first_head if hh == 0 else jnp.logical_not(first_head)
            qm = jnp.where(msk, qp, jnp.zeros_like(qp[:1, :1]))
            s = _dot_nt(qm, kp)
            e = jnp.exp(s - jnp.max(s, axis=-1, keepdims=True))
            den = jnp.sum(e, axis=-1, keepdims=True)
            outs.append(_dot(e.astype(BF16), vp) / den)
        o_ref[0, :, ls] = jnp.where(first_head, outs[0], outs[1]).astype(BF16)


def _na_ctx(zna_ctx):
    bsz, n_ctx, _ = zna_ctx.shape
    col = lambda j: pl.BlockSpec((1, n_ctx, NA_WIDTH), lambda b: (b, 0, j))
    return pl.pallas_call(
        _na_ctx_kernel,
        grid=(bsz,),
        in_specs=[col(0), col(1), col(2)],
        out_specs=pl.BlockSpec((1, n_ctx, NA_WIDTH), lambda b: (b, 0, 0)),
        out_shape=jax.ShapeDtypeStruct((bsz, n_ctx, NA_WIDTH), BF16),
        compiler_params=_cparams(("parallel",)),
        name="ctx_attn",
    )(zna_ctx, zna_ctx, zna_ctx)


def _outproj_kernel(ya_ref, yb_ref, yn_ref, w_ref, x_ref, g_ref, lg_ref, lb_ref, o_ref, *, alpha):
    ycat = jnp.concatenate([ya_ref[0], yb_ref[0], yn_ref[0]], axis=1)
    y = _dot(ycat, w_ref[...])
    s = alpha * x_ref[0] + g_ref[0] * y
    o_ref[0] = _layer_norm_rows(s) * lg_ref[...] + lb_ref[...]


def _out_proj(ya, yb, yn, w_out_b, x, mod3, lg, lb, is_ctx, ctx_row, tm, alpha):
    bsz, seq, d = x.shape
    tok = lambda w: pl.BlockSpec((1, tm, w), lambda b, i: (b, i, 0))
    full = lambda a: pl.BlockSpec(a.shape, lambda b, i: (0,) * a.ndim)
    return pl.pallas_call(
        functools.partial(_outproj_kernel, alpha=alpha),
        grid=(bsz, seq // tm),
        in_specs=[tok(GMLP_WIDTH), tok(DN_WIDTH), tok(NA_WIDTH), full(w_out_b), tok(d),
                  _mod_spec(2, d, is_ctx, ctx_row), full(lg), full(lb)],
        out_specs=tok(d),
        out_shape=jax.ShapeDtypeStruct((bsz, seq, d), F32),
        compiler_params=_cparams(("parallel", "parallel")),
        name="out_proj",
    )(ya, yb, yn, w_out_b, x, mod3, lg, lb)


FFN_CHUNK = 256


def _ffn_kernel(x_ref, xp_ref, xn_ref, sh_ref, sc_ref, g_ref, wup_ref, cw_ref, cb_ref, wd_ref,
                lg_ref, lb_ref, o_ref, gated_ref, *, tm, n_tiles, d_ff, alpha):
    halo = 8
    i = pl.program_id(1)
    has_prev = (i > 0).astype(F32)
    has_next = (i < n_tiles - 1).astype(F32)
    x = x_ref[0]
    sc = 1.0 + sc_ref[0]
    sh = sh_ref[0]
    h_mid = _layer_norm_rows(x) * sc + sh
    h_prev = (_layer_norm_rows(xp_ref[0]) * sc + sh) * has_prev
    h_next = (_layer_norm_rows(xn_ref[0]) * sc + sh) * has_next
    h_ext = jnp.concatenate([h_prev, h_mid, h_next], axis=0).astype(BF16)
    h_b = h_mid.astype(BF16)
    n_ext = tm + 2 * halo
    for j in range(d_ff // FFN_CHUNK):
        cols = slice(j * FFN_CHUNK, (j + 1) * FFN_CHUNK)
        a = _dot(h_ext, wup_ref[:, cols])
        vv = _dot(h_b, wup_ref[:, d_ff + j * FFN_CHUNK:d_ff + (j + 1) * FFN_CHUNK])
        cw = cw_ref[:, cols]
        a_m1 = pltpu.roll(a, 1, 0)[halo:halo + tm]
        a_p1 = pltpu.roll(a, n_ext - 1, 0)[halo:halo + tm]
        conv = a_m1 * cw[0:1] + a[halo:halo + tm] * cw[1:2] + a_p1 * cw[2:3] + cb_ref[:, cols]
        gated_ref[:, cols] = (_silu(conv) * vv).astype(BF16)
    y = _dot(gated_ref[...], wd_ref[...])
    s = alpha * x + g_ref[0] * y
    o_ref[0] = _layer_norm_rows(s) * lg_ref[...] + lb_ref[...]


def _ffn(x, mod3, wup, cw, cb, wd, lg, lb, is_ctx, ctx_row, tm, alpha):
    bsz, seq, d = x.shape
    n_tiles = seq // tm
    d_ff = wd.shape[0]
    assert d_ff % FFN_CHUNK == 0
    hb = tm // 8
    tok = pl.BlockSpec((1, tm, d), lambda b, i: (b, i, 0))
    full = lambda a: pl.BlockSpec(a.shape, lambda b, i: (0,) * a.ndim)
    once = lambda a: pl.BlockSpec(a.shape, lambda b, i: (0,) * a.ndim, pipeline_mode=pl.Buffered(1))
    return pl.pallas_call(
        functools.partial(_ffn_kernel, tm=tm, n_tiles=n_tiles, d_ff=d_ff, alpha=alpha),
        grid=(bsz, n_tiles),
        in_specs=[
            tok,
            pl.BlockSpec((1, 8, d), lambda b, i: (b, jnp.maximum(i * hb - 1, 0), 0)),
            pl.BlockSpec((1, 8, d), lambda b, i: (b, jnp.minimum((i + 1) * hb, seq // 8 - 1), 0)),
            _mod_spec(3, d, is_ctx, ctx_row), _mod_spec(4, d, is_ctx, ctx_row), _mod_spec(5, d, is_ctx, ctx_row),
            once(wup), full(cw), full(cb), once(wd), full(lg), full(lb),
        ],
        out_specs=tok,
        out_shape=jax.ShapeDtypeStruct((bsz, seq, d), F32),
        scratch_shapes=[pltpu.VMEM((tm, d_ff), BF16)],
        compiler_params=_cparams(("parallel", "parallel")),
        name="conv_glu",
    )(x, x, x, mod3, mod3, mod3, wup, cw, cb, wd, lg, lb)


def _rope_tables(seq):
    half = HEAD_DIM // 2
    nf = half // 2
    pos = jnp.arange(seq)
    inv = ROPE_BASE ** (-jnp.arange(nf, dtype=F32) / nf)
    ang_r = (pos // GRID_W)[:, None].astype(F32) * inv
    ang_c = (pos % GRID_W)[:, None].astype(F32) * inv
    cos_h = jnp.concatenate([jnp.cos(ang_r)] * 2 + [jnp.cos(ang_c)] * 2, axis=1)
    sin_h = jnp.concatenate([-jnp.sin(ang_r), jnp.sin(ang_r), -jnp.sin(ang_c), jnp.sin(ang_c)], axis=1)
    reps = LANES // HEAD_DIM
    return jnp.tile(cos_h, (1, reps)), jnp.tile(sin_h, (1, reps))


def _bias_table(rpb, n_rows):
    wr = min(WIN_ROWS, n_rows)
    nh = rpb.shape[0]
    n_dr = 2 * WIN_ROWS - 1
    col = np.arange(GRID_W)
    cs = np.clip(col - WIN_COLS // 2, 0, GRID_W - WIN_COLS)
    valid = (col[None, :] >= cs[:, None]) & (col[None, :] < cs[:, None] + WIN_COLS)
    dc = col[None, :] - col[:, None] + (WIN_COLS - 1)
    onehot = np.zeros((2 * WIN_COLS - 1, GRID_W, GRID_W), np.float32)
    cc, kk = np.nonzero(valid)
    onehot[dc[cc, kk], cc, kk] = 1.0
    t = jnp.einsum('hdm,mck->hdck', rpb.astype(F32), jnp.asarray(onehot), precision=lax.Precision.HIGHEST)
    t = jnp.where(valid[None, None], t, NEG_BIG)
    per_var = []
    for var in range(wr):
        lo = WIN_ROWS - 1 - var
        per_var.append(t[:, lo:lo + wr].transpose(0, 2, 1, 3).reshape(nh // 2, 2 * GRID_W, wr * GRID_W))
    return jnp.stack(per_var, axis=1)


def _gate_mats():
    expand = np.zeros((2, LANES, 2 * DN_WIDTH), np.float32)
    for dd in range(2):
        for h in range(DN_HEADS):
            for c in range(GATE_COPIES):
                expand[dd, c * GATE_COPY_STRIDE + dd * DN_HEADS + h, h * HEAD_DIM:(h + 1) * HEAD_DIM] = 1.0
            for c in range(2):
                src = c * GATE_COPY_STRIDE + 2 * DN_HEADS + dd * DN_HEADS + h
                expand[dd, src, DN_WIDTH + h * HEAD_DIM:DN_WIDTH + (h + 1) * HEAD_DIM] = 1.0
    return jnp.asarray(expand[0], BF16), jnp.asarray(expand[1], BF16)


def _gate_row(v):
    row = jnp.zeros((1, LANES), F32)
    for c in range(GATE_COPIES):
        row = row.at[0, c * GATE_COPY_STRIDE:c * GATE_COPY_STRIDE + v.shape[0]].set(v.astype(F32))
    return row


def _token_tile(seq):
    return 512 if seq % 512 == 0 else 256


def kernel(x, c, ctx, c_ctx, w_ada, b_ada, w_in, gmlp_ln_g, gmlp_ln_b, gmlp_ws, gmlp_bs, dn_conv, dn_a_log,
           dn_dt_bias, dn_norm_g, na_rpb, w_out, ln1_g, ln1_b, ffn_up, ffn_conv, ffn_conv_b, ffn_down,
           ln2_g, ln2_b):
    depth = w_ada.shape[0]
    bsz, seq, d = x.shape
    n_ctx = ctx.shape[1]
    d_ff = ffn_down.shape[1]
    alpha = (2 * depth) ** 0.25
    ctx_row = bsz
    tm = _token_tile(seq)
    tm_c = _token_tile(n_ctx)
    tm_ffn = 1024 if seq % 1024 == 0 else tm

    mod = _mod_all(c, c_ctx, w_ada, b_ada)
    rope_tabs = _rope_tables(seq)
    e_fwd, e_bwd = _gate_mats()
    gm384_sum = jnp.asarray(_block_diag_np(LANES // HEAD_DIM, HEAD_DIM, HEAD_DIM, 1.0), BF16)
    gm384_mean = jnp.asarray(_block_diag_np(LANES // HEAD_DIM, HEAD_DIM, HEAD_DIM, 1.0 / HEAD_DIM), BF16)
    nb = 4 if bsz % 4 == 0 else (2 if bsz % 2 == 0 else 1)
    wg = DN_GROUP_HEADS * HEAD_DIM
    s_zero = jnp.zeros((bsz // nb, nb * (DN_WIDTH // wg), wg, wg), F32)
    o1 = GMLP_COLS
    o2 = o1 + DN_MAIN_COLS
    o3 = o2 + DN_AB_COLS

    for l in range(depth):
        ctx_out = l < depth - 1
        mod3 = mod[l].reshape(8, 1, 6 * d)
        wl = w_in[l]
        ab_cols = [wl[:, o2:o3], jnp.zeros((d, GATE_COPY_STRIDE - DN_AB_COLS), F32)] * GATE_COPIES
        ab_cols.append(jnp.zeros((d, LANES - GATE_COPIES * GATE_COPY_STRIDE), F32))
        w_in_p = jnp.concatenate([wl[:, :o2], wl[:, o3:]] + ab_cols, axis=1).astype(BF16)
        lng = gmlp_ln_g[l].reshape(1, GMLP_WIDTH)
        lnb = gmlp_ln_b[l].reshape(1, GMLP_WIDTH)
        wsp = gmlp_ws[l].transpose(1, 0, 2).reshape(GMLP_CHUNK, GMLP_HEADS * GMLP_CHUNK).astype(BF16)
        bsx = jnp.repeat(gmlp_bs[l].T, HEAD_DIM, axis=1)
        aexp_row = _gate_row(jnp.exp(dn_a_log[l].astype(F32)).reshape(-1))
        dtb_row = _gate_row(dn_dt_bias[l].reshape(-1))
        ng_row = jnp.tile(dn_norm_g[l].astype(F32), DN_HEADS).reshape(1, DN_WIDTH)
        tbl = _bias_table(na_rpb[l], seq // GRID_W)
        w_out_b = w_out[l].astype(BF16)
        lg1 = ln1_g[l].reshape(1, d)
        lb1 = ln1_b[l].reshape(1, d)
        lg2 = ln2_g[l].reshape(1, d)
        lb2 = ln2_b[l].reshape(1, d)
        wup = ffn_up[l].astype(BF16)
        wd = ffn_down[l].astype(BF16)
        cw = ffn_conv[l]
        cb = ffn_conv_b[l].reshape(1, d_ff)

        prep = (dn_conv[l], aexp_row, dtb_row, gm384_sum, (gm384_mean, lng, lnb, wsp, bsx))
        ya_c, qkv_c, gate_c, gp_c, zna_c = _in_proj(ctx, mod3, w_in_p, *prep, None, True, ctx_row, tm_c)
        yb_c, s_ctx = _dn_scan(qkv_c, gate_c, gp_c, e_fwd, e_bwd, ng_row, gm384_mean, (s_zero, s_zero), nb)

        ya, qkv, gate, gp, zna = _in_proj(x, mod3, w_in_p, *prep, rope_tabs, False, ctx_row, tm)
        yb, _ = _dn_scan(qkv, gate, gp, e_fwd, e_bwd, ng_row, gm384_mean, s_ctx, nb)
        yn = _na(zna, zna_c, tbl, 8)
        x1 = _out_proj(ya, yb, yn, w_out_b, x, mod3, lg1, lb1, False, ctx_row, tm, alpha)
        x_next = _ffn(x1, mod3, wup, cw, cb, wd, lg2, lb2, False, ctx_row, tm_ffn, alpha)

        if ctx_out:
            yn_c = _na_ctx(zna_c)
            c1 = _out_proj(ya_c, yb_c, yn_c, w_out_b, ctx, mod3, lg1, lb1, True, ctx_row, tm_c, alpha)
            ctx = _ffn(c1, mod3, wup, cw, cb, wd, lg2, lb2, True, ctx_row, tm_c, alpha)
        x = x_next
    return x
```
